```python
import numpy as np
import jax, jax.numpy as jnp
from jax import lax

D_MODEL = 1024
BATCH = 2
SEQ = 8192
DEPTH = 2

HEAD_DIM = 64
ROPE_THETA = 10000.0
RMS_EPS = 1e-6
Q_BLOCK = 128
NEG_INF = -1e30
FORCE = 1e30

POOL_WINDOWS = (2, 4, 8, 16)
POOL_GROUPS = len(POOL_WINDOWS)
POOL_GROUP_DIM = D_MODEL // 16
POOL_DIM = POOL_GROUPS * POOL_GROUP_DIM
POOL_MAXW = max(POOL_WINDOWS)

NSA_HEADS = D_MODEL // 128
NSA_KV_GROUPS = 2
NSA_HPG = NSA_HEADS // NSA_KV_GROUPS
NSA_DIM = NSA_HEADS * HEAD_DIM
NSA_KV_DIM = NSA_KV_GROUPS * HEAD_DIM
CMP_BLOCK = 32
CMP_STRIDE = 16
CMP_HIDDEN = 256
SEL_BLOCK = 64
SEL_COUNT = 16
WINDOW = 512

MOBA_HEADS = 4
MOBA_DIM = MOBA_HEADS * HEAD_DIM
MOBA_BLOCK = 256
MOBA_TOPK = 3

D_FF = ((8 * D_MODEL + 3 * 256 - 1) // (3 * 256)) * 256

IN_SIZES = (POOL_DIM, NSA_DIM, 6 * NSA_KV_DIM, 3 * NSA_HEADS, 3 * MOBA_DIM, 3 * D_MODEL)
IN_COLS = sum(IN_SIZES)

kernel_name = "hybrid_pool_nsa_moba_block"


def rms_norm(x, g):
    xf = x.astype(jnp.float32)
    y = xf * lax.rsqrt(jnp.mean(xf * xf, axis=-1, keepdims=True) + RMS_EPS)
    return (y * g.astype(jnp.float32)).astype(x.dtype)


def rope_tables(seq):
    pos = jnp.arange(seq, dtype=jnp.float32)
    inv_freq = ROPE_THETA ** (-jnp.arange(0, HEAD_DIM, 2, dtype=jnp.float32) / HEAD_DIM)
    ang = pos[:, None] * inv_freq[None, :]
    return jnp.cos(ang), jnp.sin(ang)


def apply_rope(x, cos, sin):
    half = HEAD_DIM // 2
    x1, x2 = x[..., :half], x[..., half:]
    c = cos[:, None, :].astype(x.dtype)
    s = sin[:, None, :].astype(x.dtype)
    return jnp.concatenate([x1 * c - x2 * s, x2 * c + x1 * s], axis=-1)


def masked_softmax(scores, mask):
    s = jnp.where(mask, scores.astype(jnp.float32), NEG_INF)
    m = jnp.max(s, axis=-1, keepdims=True)
    p = jnp.exp(s - m) * mask
    return p / jnp.maximum(jnp.sum(p, axis=-1, keepdims=True), 1e-30)


def pool_mixer(u, pool_w, pool_scale):
    B, S, C = u.shape
    uf = u.astype(jnp.float32)
    csum = jnp.concatenate([jnp.zeros((B, POOL_MAXW, C), jnp.float32), jnp.cumsum(uf, axis=1)], axis=1)
    t = jnp.arange(S)
    pooled = []
    for gi, w in enumerate(POOL_WINDOWS):
        sl = slice(gi * POOL_GROUP_DIM, (gi + 1) * POOL_GROUP_DIM)
        total = csum[:, POOL_MAXW:POOL_MAXW + S, sl] - csum[:, POOL_MAXW - w:POOL_MAXW - w + S, sl]
        count = jnp.minimum(t + 1, w).astype(jnp.float32)
        pooled.append(total / count[None, :, None])
    d = jnp.concatenate(pooled, axis=-1) - uf
    d = d.reshape(B, S, POOL_GROUPS, POOL_GROUP_DIM)
    y = jnp.einsum("bsgc,gce->bsge", d, pool_w.astype(jnp.float32)).reshape(B, S, C)
    return (y * pool_scale.astype(jnp.float32)).astype(u.dtype)


def nsa_compress(kv, pos_emb, w1, b1, w2, b2):
    B, S, G, D = kv.shape
    n_cmp = (S - CMP_BLOCK) // CMP_STRIDE + 1
    idx = jnp.arange(n_cmp)[:, None] * CMP_STRIDE + jnp.arange(CMP_BLOCK)[None, :]
    blocks = kv[:, idx] + pos_emb[:, None, :]
    flat = blocks.transpose(0, 1, 3, 2, 4).reshape(B, n_cmp, G, CMP_BLOCK * D)
    hid = jax.nn.gelu(flat @ w1 + b1)
    return hid @ w2 + b2


def nsa_mixer(q, k_cmp, v_cmp, k_sel, v_sel, k_win, v_win, gate_logits, cmp_pos, cmp_w1, cmp_b1, cmp_w2, cmp_b2):
    B, S = q.shape[0], q.shape[1]
    G, Hg, D = NSA_KV_GROUPS, NSA_HPG, HEAD_DIM
    scale = HEAD_DIM ** -0.5
    qg = q.reshape(B, S, G, Hg, D)
    kc = nsa_compress(k_cmp, cmp_pos[0], cmp_w1[0], cmp_b1[0], cmp_w2[0], cmp_b2[0])
    vc = nsa_compress(v_cmp, cmp_pos[1], cmp_w1[1], cmp_b1[1], cmp_w2[1], cmp_b2[1])
    n_cmp = kc.shape[1]
    n_sel = S // SEL_BLOCK
    n_top = min(SEL_COUNT, n_sel)
    ratio = SEL_BLOCK // CMP_STRIDE
    overlap = CMP_BLOCK // CMP_STRIDE
    agg_w = np.convolve(np.ones(ratio), np.ones(overlap))
    pad_front = overlap - 1
    pad_back = ratio * n_sel + len(agg_w) - 1 - pad_front - n_cmp
    cmp_end = jnp.arange(n_cmp) * CMP_STRIDE + CMP_BLOCK - 1
    k_sel_blk = k_sel.transpose(0, 2, 1, 3).reshape(B, G, n_sel, SEL_BLOCK, D)
    v_sel_blk = v_sel.transpose(0, 2, 1, 3).reshape(B, G, n_sel, SEL_BLOCK, D)
    k_win_pad = jnp.pad(k_win, ((0, 0), (WINDOW, 0), (0, 0), (0, 0)))
    v_win_pad = jnp.pad(v_win, ((0, 0), (WINDOW, 0), (0, 0), (0, 0)))
    b_idx = jnp.arange(B)[:, None, None, None]
    g_idx = jnp.arange(G)[None, :, None, None]
    blk = jnp.arange(n_sel)

    def chunk(c):
        q0 = c * Q_BLOCK
        tq = q0 + jnp.arange(Q_BLOCK)
        qc = lax.dynamic_slice_in_dim(qg, q0, Q_BLOCK, axis=1)
        s_cmp = jnp.einsum("bqghd,bngd->bghqn", qc, kc) * scale
        p_cmp = masked_softmax(s_cmp, cmp_end[None, :] <= tq[:, None])
        o_cmp = jnp.einsum("bghqn,bngd->bqghd", p_cmp.astype(vc.dtype), vc)
        p_grp = jnp.sum(p_cmp, axis=2)
        p_pad = jnp.pad(p_grp, ((0, 0), (0, 0), (0, 0), (pad_front, pad_back)))
        p_slc = sum(float(agg_w[w]) * p_pad[..., w:w + ratio * n_sel:ratio] for w in range(len(agg_w)))
        cur = tq // SEL_BLOCK
        started = blk[None, :] <= cur[:, None]
        forced = (blk[None, :] == 0) | (blk[None, :] == cur[:, None]) | (blk[None, :] == cur[:, None] - 1)
        s_slc = jnp.where(forced, FORCE, jnp.where(started, p_slc, NEG_INF))
        _, sel = lax.top_k(s_slc, n_top)
        kb = k_sel_blk[b_idx, g_idx, sel].reshape(B, G, Q_BLOCK, n_top * SEL_BLOCK, D)
        vb = v_sel_blk[b_idx, g_idx, sel].reshape(B, G, Q_BLOCK, n_top * SEL_BLOCK, D)
        kpos = (sel[..., None] * SEL_BLOCK + jnp.arange(SEL_BLOCK)).reshape(B, G, Q_BLOCK, n_top * SEL_BLOCK)
        m_sel = (kpos <= tq[:, None])[:, :, None]
        s_sel = jnp.einsum("bqghd,bgqnd->bghqn", qc, kb) * scale
        p_sel = masked_softmax(s_sel, m_sel)
        o_sel = jnp.einsum("bghqn,bgqnd->bqghd", p_sel.astype(vb.dtype), vb)
        kw = lax.dynamic_slice_in_dim(k_win_pad, q0, Q_BLOCK + WINDOW, axis=1)
        vw = lax.dynamic_slice_in_dim(v_win_pad, q0, Q_BLOCK + WINDOW, axis=1)
        kpos_w = q0 - WINDOW + jnp.arange(Q_BLOCK + WINDOW)
        m_win = (kpos_w[None, :] <= tq[:, None]) & (kpos_w[None, :] > tq[:, None] - WINDOW) & (kpos_w[None, :] >= 0)
        s_win = jnp.einsum("bqghd,bkgd->bghqk", qc, kw) * scale
        p_win = masked_softmax(s_win, m_win)
        o_win = jnp.einsum("bghqk,bkgd->bqghd", p_win.astype(vw.dtype), vw)
        return o_cmp, o_sel, o_win

    o_cmp, o_sel, o_win = lax.map(chunk, jnp.arange(S // Q_BLOCK))

    def unchunk(y):
        return jnp.moveaxis(y, 0, 1).reshape(B, S, NSA_HEADS, D)

    g = jax.nn.sigmoid(gate_logits)
    out = g[..., 0:1] * unchunk(o_cmp) + g[..., 1:2] * unchunk(o_sel) + g[..., 2:3] * unchunk(o_win)
    return out.reshape(B, S, NSA_DIM)


def moba_mixer(q, k, v):
    B, S, H, D = q.shape
    scale = HEAD_DIM ** -0.5
    n_blk = -(-S // MOBA_BLOCK)
    s_pad = n_blk * MOBA_BLOCK - S
    k_pad = jnp.pad(k, ((0, 0), (0, s_pad), (0, 0), (0, 0)))
    v_pad = jnp.pad(v, ((0, 0), (0, s_pad), (0, 0), (0, 0)))
    k_blk = k_pad.reshape(B, n_blk, MOBA_BLOCK, H, D).transpose(0, 3, 1, 2, 4)
    v_blk = v_pad.reshape(B, n_blk, MOBA_BLOCK, H, D).transpose(0, 3, 1, 2, 4)
    k_mean = jnp.mean(k_blk.astype(jnp.float32), axis=3).astype(k.dtype)
    n_top = max(1, min(MOBA_TOPK, n_blk - 1))
    b_idx = jnp.arange(B)[:, None, None, None]
    h_idx = jnp.arange(H)[None, :, None, None]
    blk = jnp.arange(n_blk)

    def chunk(c):
        q0 = c * Q_BLOCK
        tq = q0 + jnp.arange(Q_BLOCK)
        own = q0 // MOBA_BLOCK
        qc = lax.dynamic_slice_in_dim(q, q0, Q_BLOCK, axis=1)
        s_gate = jnp.einsum("bqhd,bhnd->bhqn", qc, k_mean).astype(jnp.float32)
        s_gate = jnp.where(blk < own, s_gate, NEG_INF)
        _, sel = lax.top_k(s_gate, n_top)
        sel_ok = sel < own
        kb = k_blk[b_idx, h_idx, sel].reshape(B, H, Q_BLOCK, n_top * MOBA_BLOCK, D)
        vb = v_blk[b_idx, h_idx, sel].reshape(B, H, Q_BLOCK, n_top * MOBA_BLOCK, D)
        s_sel = jnp.einsum("bqhd,bhqnd->bhqn", qc, kb) * scale
        m_sel = jnp.repeat(sel_ok, MOBA_BLOCK, axis=-1)
        ko = lax.dynamic_slice_in_dim(k_pad, own * MOBA_BLOCK, MOBA_BLOCK, axis=1)
        vo = lax.dynamic_slice_in_dim(v_pad, own * MOBA_BLOCK, MOBA_BLOCK, axis=1)
        s_own = jnp.einsum("bqhd,bkhd->bhqk", qc, ko) * scale
        kpos = own * MOBA_BLOCK + jnp.arange(MOBA_BLOCK)
        m_own = jnp.broadcast_to(kpos[None, :] <= tq[:, None], (B, H, Q_BLOCK, MOBA_BLOCK))
        p = masked_softmax(jnp.concatenate([s_sel, s_own], axis=-1),
                           jnp.concatenate([m_sel, m_own], axis=-1)).astype(v.dtype)
        n_s = n_top * MOBA_BLOCK
        return (jnp.einsum("bhqn,bhqnd->bqhd", p[..., :n_s], vb)
                + jnp.einsum("bhqk,bkhd->bqhd", p[..., n_s:], vo))

    out = lax.map(chunk, jnp.arange(S // Q_BLOCK))
    return jnp.moveaxis(out, 0, 1).reshape(B, S, H * D)


def setup_inputs(seed: int = 0) -> dict:
    key = jax.random.key(seed)
    ks = jax.random.split(key, 20)
    f32 = jnp.float32

    def nrm(k, shape, s):
        return jax.random.normal(k, shape, f32) * s

    return {
        "x": nrm(ks[0], (BATCH, SEQ, D_MODEL), 1.0),
        "attn_norm": 1.0 + nrm(ks[1], (DEPTH, D_MODEL), 0.05),
        "w_in": nrm(ks[2], (DEPTH, D_MODEL, IN_COLS), D_MODEL ** -0.5),
        "b_in": nrm(ks[3], (DEPTH, IN_COLS), 0.02),
        "pool_w": nrm(ks[4], (DEPTH, POOL_GROUPS, POOL_GROUP_DIM, POOL_GROUP_DIM), POOL_GROUP_DIM ** -0.5),
        "pool_scale": 1.0 + nrm(ks[5], (DEPTH, POOL_DIM), 0.05),
        "cmp_pos": nrm(ks[6], (DEPTH, 2, CMP_BLOCK, HEAD_DIM), 0.1),
        "cmp_w1": nrm(ks[7], (DEPTH, 2, CMP_BLOCK * HEAD_DIM, CMP_HIDDEN), (CMP_BLOCK * HEAD_DIM) ** -0.5),
        "cmp_b1": nrm(ks[8], (DEPTH, 2, CMP_HIDDEN), 0.02),
        "cmp_w2": nrm(ks[9], (DEPTH, 2, CMP_HIDDEN, HEAD_DIM), CMP_HIDDEN ** -0.5),
        "cmp_b2": nrm(ks[10], (DEPTH, 2, HEAD_DIM), 0.02),
        "w_br_pool": nrm(ks[11], (DEPTH, POOL_DIM, D_MODEL), POOL_DIM ** -0.5),
        "w_br_nsa": nrm(ks[12], (DEPTH, NSA_DIM, D_MODEL), NSA_DIM ** -0.5),
        "w_br_moba": nrm(ks[13], (DEPTH, MOBA_DIM, D_MODEL), MOBA_DIM ** -0.5),
        "w_out": nrm(ks[14], (DEPTH, D_MODEL, D_MODEL), D_MODEL ** -0.5),
        "ffn_norm": 1.0 + nrm(ks[15], (DEPTH, D_MODEL), 0.05),
        "w_gate": nrm(ks[16], (DEPTH, D_MODEL, D_FF), D_MODEL ** -0.5),
        "w_up": nrm(ks[17], (DEPTH, D_MODEL, D_FF), D_MODEL ** -0.5),
        "w_down": nrm(ks[18], (DEPTH, D_FF, D_MODEL), D_FF ** -0.5),
        "final_norm": 1.0 + nrm(ks[19], (D_MODEL,), 0.05),
    }


def reference(x, attn_norm, w_in, b_in, pool_w, pool_scale, cmp_pos, cmp_w1, cmp_b1, cmp_w2, cmp_b2,
              w_br_pool, w_br_nsa, w_br_moba, w_out, ffn_norm, w_gate, w_up, w_down, final_norm):
    B, S, _ = x.shape
    cos, sin = rope_tables(S)
    split_at = np.cumsum(IN_SIZES)[:-1].tolist()
    for l in range(DEPTH):
        h = rms_norm(x, attn_norm[l])
        proj = h @ w_in[l] + b_in[l]
        u_pool, q_nsa, kv_nsa, g_nsa, qkv_moba, g_br = jnp.split(proj, split_at, axis=-1)

        a = pool_mixer(u_pool, pool_w[l], pool_scale[l])

        q_n = apply_rope(q_nsa.reshape(B, S, NSA_HEADS, HEAD_DIM), cos, sin)
        kv = kv_nsa.reshape(B, S, 6, NSA_KV_GROUPS, HEAD_DIM)
        k_c = apply_rope(kv[:, :, 0], cos, sin)
        k_s = apply_rope(kv[:, :, 2], cos, sin)
        k_w = apply_rope(kv[:, :, 4], cos, sin)
        b = nsa_mixer(q_n, k_c, kv[:, :, 1], k_s, kv[:, :, 3], k_w, kv[:, :, 5],
                      g_nsa.reshape(B, S, NSA_HEADS, 3),
                      cmp_pos[l], cmp_w1[l], cmp_b1[l], cmp_w2[l], cmp_b2[l])

        qkv = qkv_moba.reshape(B, S, 3, MOBA_HEADS, HEAD_DIM)
        c = moba_mixer(apply_rope(qkv[:, :, 0], cos, sin), apply_rope(qkv[:, :, 1], cos, sin),
                       qkv[:, :, 2])

        gate = jax.nn.sigmoid(g_br)
        g_a, g_b, g_c = jnp.split(gate, 3, axis=-1)
        merged = g_a * (a @ w_br_pool[l]) + g_b * (b @ w_br_nsa[l]) + g_c * (c @ w_br_moba[l])
        x = x + merged @ w_out[l]

        h2 = rms_norm(x, ffn_norm[l])
        x = x + (jax.nn.silu(h2 @ w_gate[l]) * (h2 @ w_up[l])) @ w_down[l]
    return rms_norm(x, final_norm)
```

```python
import functools

import numpy as np
import jax
import jax.numpy as jnp
from jax import lax
from jax.experimental import pallas as pl
from jax.experimental.pallas import tpu as pltpu

F32 = jnp.float32
BF16 = jnp.bfloat16

D_MODEL = 1024
HEAD_DIM = 64
ROPE_THETA = 10000.0
RMS_EPS = 1e-6
NEG_INF = -1e30
FORCE = 1e30
REMOVED = -3e38

POOL_WINDOWS = (2, 4, 8, 16)
POOL_GROUP_DIM = 64
POOL_DIM = 256
POOL_MAXW = 16

NSA_HEADS = 8
NSA_KV_GROUPS = 2
NSA_HPG = 4
NSA_DIM = 512
CMP_BLOCK = 32
CMP_STRIDE = 16
CMP_HIDDEN = 256
SEL_BLOCK = 64
SEL_COUNT = 16
WINDOW = 512

MOBA_HEADS = 4
MOBA_DIM = 256
MOBA_BLOCK = 256
MOBA_TOPK = 3

Q_TILE = 128
KV_CHUNK = 512
ROW_TILE = 512
FFN_ROW_TILE = 1024
FFN_COL_TILE = 256
VMEM_LIMIT = 56 * 1024 * 1024

C_POOL = 0
C_CMP = 256
C_QN = 512
C_KS = 1024
C_VS = 1152
C_KW = 1280
C_VW = 1408
C_GN = 1536
C_QM = 1664
C_KM = 1920
C_VM = 2176
C_GBR = 2432
C_TOTAL = 5504


def _dot(a, b):
    return jnp.dot(a, b, preferred_element_type=F32)


def _rms(x, gamma):
    return x * lax.rsqrt(jnp.mean(x * x, axis=-1, keepdims=True) + RMS_EPS) * gamma


def _proj_kernel(x_ref, gam_ref, w_ref, b_ref, cos_ref, sin_ref,
                 upool_ref, cmpin_ref, qtn_ref, ksel_ref, vselt_ref, kwin_ref, vwint_ref, gt_ref,
                 qtm_ref, km_ref, vtm_ref, kmean_ref, gbr_ref):
    tm = x_ref.shape[0]
    h = _rms(x_ref[...], gam_ref[...]).astype(BF16)
    cos = cos_ref[...]
    sin = sin_ref[...]
    lane = lax.broadcasted_iota(jnp.int32, (tm, 128), 1)
    first_half = (lane & 32) == 0
    scale = HEAD_DIM ** -0.5

    def seg(a, b):
        return _dot(h, w_ref[:, a:b]) + b_ref[:, a:b]

    def rope(y):
        swap = jnp.where(first_half, pltpu.roll(y, 96, 1), pltpu.roll(y, 32, 1))
        return y * cos + swap * sin

    y = seg(C_POOL, C_POOL + 512)
    upool_ref[...] = y[:, :256]
    cmpin_ref[...] = jnp.concatenate([rope(y[:, 256:384]), y[:, 384:512]], axis=1)

    y = seg(C_QN, C_QN + 512)
    for j in range(4):
        q = rope(y[:, 128 * j:128 * j + 128]) * scale
        qtn_ref[128 * j:128 * j + 128, :] = q.T.astype(BF16)

    y = seg(C_KS, C_KS + 512)
    ksel_ref[...] = rope(y[:, 0:128]).astype(BF16)
    vselt_ref[0] = y[:, 128:256].T.astype(BF16)
    kwin_ref[...] = rope(y[:, 256:384]).astype(BF16)
    vwt = y[:, 384:512].T.astype(BF16)
    for j in range(tm // 128):
        vwint_ref[j] = vwt[:, 128 * j:128 * j + 128]

    y = seg(C_GN, C_GN + 128)
    gt_ref[...] = y.T

    y = seg(C_QM, C_QM + 256)
    for j in range(2):
        q = rope(y[:, 128 * j:128 * j + 128]) * scale
        qtm_ref[128 * j:128 * j + 128, :] = q.T.astype(BF16)

    y = seg(C_KM, C_KM + 256)
    km = jnp.concatenate([rope(y[:, 0:128]), rope(y[:, 128:256])], axis=1)
    km_ref[...] = km.astype(BF16)
    nblk = tm // MOBA_BLOCK
    means = [jnp.sum(km[MOBA_BLOCK * j:MOBA_BLOCK * (j + 1), :], axis=0, keepdims=True) * (1.0 / MOBA_BLOCK)
             for j in range(nblk)]
    kmean_ref[...] = jnp.concatenate(means, axis=0)

    y = seg(C_VM, C_VM + 256)
    vtm_ref[0] = y.T.astype(BF16)

    for j in range(6):
        a = C_GBR + 512 * j
        gbr_ref[:, 512 * j:512 * j + 512] = seg(a, a + 512)


def _proj_call(x2, gamma, w, bias, cos_t, sin_t, B, S):
    N = B * S
    tm = ROW_TILE
    nt = S // tm
    f = lambda shape, dt: jax.ShapeDtypeStruct(shape, dt)
    out_shape = (
        f((N, 256), F32),
        f((N, 256), F32),
        f((B, NSA_DIM, S), BF16),
        f((N, 128), BF16),
        f((B, S // KV_CHUNK, 128, KV_CHUNK), BF16),
        f((N, 128), BF16),
        f((B, S // 128, 128, 128), BF16),
        f((B, 128, S), F32),
        f((B, MOBA_DIM, S), BF16),
        f((N, MOBA_DIM), BF16),
        f((B, S // KV_CHUNK, MOBA_DIM, KV_CHUNK), BF16),
        f((B, nt, tm // MOBA_BLOCK, MOBA_DIM), F32),
        f((N, 3 * D_MODEL), F32),
    )
    row = lambda w_: pl.BlockSpec((tm, w_), lambda i: (i, 0))
    const = lambda shape: pl.BlockSpec(shape, lambda i: (0,) * len(shape))
    in_specs = [
        row(D_MODEL),
        const((1, D_MODEL)),
        const((D_MODEL, C_TOTAL)),
        const((1, C_TOTAL)),
        pl.BlockSpec((tm, 128), lambda i: (i % nt, 0)),
        pl.BlockSpec((tm, 128), lambda i: (i % nt, 0)),
    ]
    out_specs = (
        row(256),
        row(256),
        pl.BlockSpec((None, NSA_DIM, tm), lambda i: (i // nt, 0, i % nt)),
        row(128),
        pl.BlockSpec((None, tm // KV_CHUNK, 128, KV_CHUNK), lambda i: (i // nt, i % nt, 0, 0)),
        row(128),
        pl.BlockSpec((None, tm // 128, 128, 128), lambda i: (i // nt, i % nt, 0, 0)),
        pl.BlockSpec((None, 128, tm), lambda i: (i // nt, 0, i % nt)),
        pl.BlockSpec((None, MOBA_DIM, tm), lambda i: (i // nt, 0, i % nt)),
        row(MOBA_DIM),
        pl.BlockSpec((None, tm // KV_CHUNK, MOBA_DIM, KV_CHUNK), lambda i: (i // nt, i % nt, 0, 0)),
        pl.BlockSpec((None, None, tm // MOBA_BLOCK, MOBA_DIM), lambda i: (i // nt, i % nt, 0, 0)),
        row(3 * D_MODEL),
    )
    return pl.pallas_call(
        _proj_kernel,
        out_shape=out_shape,
        grid=(N // tm,),
        in_specs=in_specs,
        out_specs=out_specs,
        compiler_params=pltpu.CompilerParams(
            dimension_semantics=("arbitrary",), vmem_limit_bytes=VMEM_LIMIT),
        name="proj",
    )(x2, gamma, w, bias, cos_t, sin_t)


def _gelu_tanh(x):
    return x * (0.5 * (1.0 + jnp.tanh(np.sqrt(2.0 / np.pi).astype(np.float32) * (x + 0.044715 * (x * x * x)))))


def _compress_kernel(kr_ref, pos_ref, w1_ref, b1_ref, w2_ref, b2_ref, kc_ref, vct_ref):
    R = kr_ref.shape[1]
    half = CMP_STRIDE * HEAD_DIM
    outs = []
    for t in range(2):
        for g in range(NSA_KV_GROUPS):
            kr = kr_ref[2 * t + g]
            lo = (kr + pos_ref[t, :, :half]).astype(BF16)
            hi = (kr + pos_ref[t, :, half:]).astype(BF16)
            a = _dot(lo, w1_ref[t, :half, :])
            bm = _dot(hi, w1_ref[t, half:, :])
            bm = pltpu.roll(bm, R - 1, 0)
            hid = _gelu_tanh(a + bm + b1_ref[t])
            outs.append(_dot(hid.astype(BF16), w2_ref[t]) + b2_ref[t])
    kc_ref[...] = jnp.concatenate(outs[0:2], axis=1).astype(BF16)
    vct_ref[...] = jnp.concatenate(outs[2:4], axis=1).T.astype(BF16)


def _compress_call(kr, pos, w1, b1, w2, b2, B, R):
    return pl.pallas_call(
        _compress_kernel,
        out_shape=(jax.ShapeDtypeStruct((B, R, 128), BF16), jax.ShapeDtypeStruct((B, 128, R), BF16)),
        grid=(B,),
        in_specs=[
            pl.BlockSpec((None, 4, R, CMP_STRIDE * HEAD_DIM), lambda b: (b, 0, 0, 0)),
            pl.BlockSpec((2, 1, CMP_BLOCK * HEAD_DIM), lambda b: (0, 0, 0)),
            pl.BlockSpec((2, CMP_BLOCK * HEAD_DIM, CMP_HIDDEN), lambda b: (0, 0, 0)),
            pl.BlockSpec((2, 1, CMP_HIDDEN), lambda b: (0, 0, 0)),
            pl.BlockSpec((2, CMP_HIDDEN, HEAD_DIM), lambda b: (0, 0, 0)),
            pl.BlockSpec((2, 1, HEAD_DIM), lambda b: (0, 0, 0)),
        ],
        out_specs=(pl.BlockSpec((None, R, 128), lambda b: (b, 0, 0)),
                   pl.BlockSpec((None, 128, R), lambda b: (b, 0, 0))),
        compiler_params=pltpu.CompilerParams(
            dimension_semantics=("arbitrary",), vmem_limit_bytes=VMEM_LIMIT),
        name="nsa_compress",
    )(kr, pos, w1, b1, w2, b2)


def _softmax_block(s):
    m = jnp.max(s, axis=0, keepdims=True)
    m_use = jnp.where(m == NEG_INF, 0.0, m)
    p = jnp.exp(s - m_use)
    return p, jnp.sum(p, axis=0, keepdims=True)


def _flash_update(s, vt, m, l, acc_ref):
    m_new = jnp.maximum(m, jnp.max(s, axis=0, keepdims=True))
    m_use = jnp.where(m_new == NEG_INF, 0.0, m_new)
    alpha = jnp.exp(m - m_use)
    p = jnp.exp(s - m_use)
    l_new = alpha * l + jnp.sum(p, axis=0, keepdims=True)
    acc_ref[...] = alpha * acc_ref[...] + _dot(vt, p.astype(BF16))
    return m_new, l_new


def _topk_mask(scores, rows_f, k):
    work = scores
    sel = jnp.zeros(scores.shape, F32)
    for _ in range(k):
        cm = jnp.max(work, axis=0, keepdims=True)
        idx = jnp.min(jnp.where(work == cm, rows_f, 1e9), axis=0, keepdims=True)
        hit = rows_f == idx
        sel = jnp.where(hit, 1.0, sel)
        work = jnp.where(hit, REMOVED, work)
    return sel


def _nsa_kernel(qt_ref, gt_ref, kc_ref, vct_ref, ksel_ref, vselt_ref, kwin_ref, vwint_ref, out_ref,
                qt_s, pg_s, bias_s, acc_s):
    qi = pl.program_id(1)
    q0 = qi * Q_TILE
    L = NSA_HEADS * Q_TILE
    n_cmp_rows = kc_ref.shape[0]
    n_sel = bias_s.shape[0]

    zeros = jnp.zeros((HEAD_DIM, Q_TILE), BF16)
    for h in range(NSA_HEADS):
        blk = qt_ref[HEAD_DIM * h:HEAD_DIM * (h + 1), :]
        parts = [blk, zeros] if h < NSA_HPG else [zeros, blk]
        qt_s[:, Q_TILE * h:Q_TILE * (h + 1)] = jnp.concatenate(parts, axis=0)
    qt = qt_s[...]
    tq = q0 + (lax.broadcasted_iota(jnp.int32, (1, L), 1) & (Q_TILE - 1))
    tq1 = q0 + lax.broadcasted_iota(jnp.int32, (1, Q_TILE), 1)

    s = _dot(kc_ref[...], qt)
    n_idx = lax.broadcasted_iota(jnp.int32, (n_cmp_rows, 1), 0)
    s = jnp.where(n_idx * CMP_STRIDE + (CMP_BLOCK - 1) <= tq, s, NEG_INF)
    p, l = _softmax_block(s)
    den = jnp.maximum(l, 1e-30)
    o_cmp = _dot(vct_ref[...], p.astype(BF16)) / den
    pn = p / den

    blk_i = lax.broadcasted_iota(jnp.int32, (n_sel, 1), 0)
    blk_f = blk_i.astype(F32)
    cur = tq1 // SEL_BLOCK
    forced = (blk_i == 0) | (blk_i == cur) | (blk_i == cur - 1)
    started = blk_i <= cur
    n_top = min(SEL_COUNT, n_sel)
    for g in range(NSA_KV_GROUPS):
        base = g * NSA_HPG * Q_TILE
        pg = pn[:, base:base + Q_TILE]
        for h in range(1, NSA_HPG):
            pg = pg + pn[:, base + h * Q_TILE:base + (h + 1) * Q_TILE]
        pg_s[g, 0:8, :] = jnp.zeros((8, Q_TILE), F32)
        pg_s[g, 8:8 + n_cmp_rows, :] = pg

        def tap(w):
            return pg_s[g, pl.ds(8 + w, n_sel, stride=SEL_BLOCK // CMP_STRIDE), :]

        t = tap(-1) + 2.0 * tap(0)
        t = t + 2.0 * tap(1)
        t = t + 2.0 * tap(2)
        t = t + tap(3)
        s_slc = jnp.where(forced, FORCE, jnp.where(started, t, NEG_INF))
        sel = _topk_mask(s_slc, jnp.broadcast_to(blk_f, (n_sel, Q_TILE)), n_top)
        bias = jnp.where(sel > 0.5, 0.0, NEG_INF)
        for h in range(NSA_HPG):
            bias_s[:, base + h * Q_TILE:base + (h + 1) * Q_TILE] = bias

    blocks_per_chunk = KV_CHUNK // SEL_BLOCK

    def sel_scores(c):
        k = ksel_ref[pl.ds(pl.multiple_of(c * KV_CHUNK, KV_CHUNK), KV_CHUNK), :]
        sc = _dot(k, qt)
        b8 = bias_s[pl.ds(pl.multiple_of(c * blocks_per_chunk, blocks_per_chunk), blocks_per_chunk), :]
        parts = [sc[SEL_BLOCK * j:SEL_BLOCK * (j + 1), :] + b8[j:j + 1, :] for j in range(blocks_per_chunk)]
        return jnp.concatenate(parts, axis=0)

    acc_s[...] = jnp.zeros_like(acc_s)
    m0 = jnp.full((1, L), NEG_INF, F32)
    l0 = jnp.zeros((1, L), F32)
    c_last = q0 // KV_CHUNK

    def body(c, carry):
        m, l_ = carry
        return _flash_update(sel_scores(c), vselt_ref[c], m, l_, acc_s)

    m, l_ = lax.fori_loop(0, c_last, body, (m0, l0))
    kpos = c_last * KV_CHUNK + lax.broadcasted_iota(jnp.int32, (KV_CHUNK, 1), 0)
    s = jnp.where(kpos <= tq, sel_scores(c_last), NEG_INF)
    m, l_ = _flash_update(s, vselt_ref[c_last], m, l_, acc_s)
    o_sel = acc_s[...] / jnp.maximum(l_, 1e-30)

    nw = (WINDOW + Q_TILE) // 128
    wc0 = jnp.maximum(qi - WINDOW // Q_TILE, 0)
    wstart = wc0 * 128
    k = kwin_ref[pl.ds(pl.multiple_of(wstart, 128), WINDOW + Q_TILE), :]
    s = _dot(k, qt)
    kpos = wstart + lax.broadcasted_iota(jnp.int32, (WINDOW + Q_TILE, 1), 0)
    s = jnp.where((kpos <= tq) & (kpos > tq - WINDOW), s, NEG_INF)
    p, l = _softmax_block(s)
    vt = jnp.concatenate([vwint_ref[wc0 + j] for j in range(nw)], axis=1)
    o_win = _dot(vt, p.astype(BF16)) / jnp.maximum(l, 1e-30)

    gates = jax.nn.sigmoid(gt_ref[0:3 * NSA_HEADS, :])
    outs = []
    for h in range(NSA_HEADS):
        r0 = HEAD_DIM * (h // NSA_HPG)
        rows = slice(r0, r0 + HEAD_DIM)
        cols = slice(Q_TILE * h, Q_TILE * (h + 1))
        o = gates[3 * h:3 * h + 1, :] * o_cmp[rows, cols]
        o = o + gates[3 * h + 1:3 * h + 2, :] * o_sel[rows, cols]
        o = o + gates[3 * h + 2:3 * h + 3, :] * o_win[rows, cols]
        outs.append(o)
    out_ref[...] = jnp.concatenate(outs, axis=0).T.astype(BF16)


def _nsa_call(qtn, gt, kc, vct, ksel, vselt, kwin, vwint, B, S):
    R = S // CMP_STRIDE
    n_sel = S // SEL_BLOCK
    L = NSA_HEADS * Q_TILE
    return pl.pallas_call(
        _nsa_kernel,
        out_shape=jax.ShapeDtypeStruct((B * S, NSA_DIM), BF16),
        grid=(B, S // Q_TILE),
        in_specs=[
            pl.BlockSpec((None, NSA_DIM, Q_TILE), lambda b, q: (b, 0, q)),
            pl.BlockSpec((None, 128, Q_TILE), lambda b, q: (b, 0, q)),
            pl.BlockSpec((None, R, 128), lambda b, q: (b, 0, 0)),
            pl.BlockSpec((None, 128, R), lambda b, q: (b, 0, 0)),
            pl.BlockSpec((S, 128), lambda b, q: (b, 0)),
            pl.BlockSpec((None, S // KV_CHUNK, 128, KV_CHUNK), lambda b, q: (b, 0, 0, 0)),
            pl.BlockSpec((S, 128), lambda b, q: (b, 0)),
            pl.BlockSpec((None, S // 128, 128, 128), lambda b, q: (b, 0, 0, 0)),
        ],
        out_specs=pl.BlockSpec((Q_TILE, NSA_DIM), lambda b, q: (b * (S // Q_TILE) + q, 0)),
        scratch_shapes=[
            pltpu.VMEM((128, L), BF16),
            pltpu.VMEM((NSA_KV_GROUPS, 8 + R, Q_TILE), F32),
            pltpu.VMEM((n_sel, L), F32),
            pltpu.VMEM((128, L), F32),
        ],
        compiler_params=pltpu.CompilerParams(
            dimension_semantics=("arbitrary", "arbitrary"), vmem_limit_bytes=VMEM_LIMIT),
        name="nsa_attn",
    )(qtn, gt, kc, vct, ksel, vselt, kwin, vwint)


def _moba_kernel(qt_ref, kmean_ref, k_ref, vt_ref, out_ref, qt_s, bias_s, acc_s):
    qi = pl.program_id(1)
    q0 = qi * Q_TILE
    L = MOBA_HEADS * Q_TILE
    n_blk = kmean_ref.shape[0]
    n_top = max(1, min(MOBA_TOPK, n_blk - 1))

    qt_s[...] = jnp.zeros_like(qt_s)
    for h in range(MOBA_HEADS):
        rows = slice(HEAD_DIM * h, HEAD_DIM * (h + 1))
        qt_s[rows, Q_TILE * h:Q_TILE * (h + 1)] = qt_ref[rows, :]
    qt = qt_s[...]
    tq = q0 + (lax.broadcasted_iota(jnp.int32, (1, L), 1) & (Q_TILE - 1))

    own = q0 // MOBA_BLOCK
    blk_i = lax.broadcasted_iota(jnp.int32, (n_blk, 1), 0)
    sg = _dot(kmean_ref[...].astype(BF16), qt)
    sg = jnp.where(blk_i < own, sg, NEG_INF)
    sel = _topk_mask(sg, jnp.broadcast_to(blk_i.astype(F32), (n_blk, L)), n_top)
    allow = ((sel > 0.5) & (blk_i < own)) | (blk_i == own)
    bias = jnp.where(allow, 0.0, NEG_INF)
    for j in range(n_blk):
        bias_s[j] = jnp.broadcast_to(bias[j:j + 1, :], (8, L))

    bpc = KV_CHUNK // MOBA_BLOCK

    def scores(c):
        k = k_ref[pl.ds(pl.multiple_of(c * KV_CHUNK, KV_CHUNK), KV_CHUNK), :]
        sc = _dot(k, qt)
        parts = []
        for j in range(bpc):
            b8 = bias_s[c * bpc + j]
            bb = jnp.concatenate([b8] * (MOBA_BLOCK // 8), axis=0)
            parts.append(sc[MOBA_BLOCK * j:MOBA_BLOCK * (j + 1), :] + bb)
        return jnp.concatenate(parts, axis=0)

    acc_s[...] = jnp.zeros_like(acc_s)
    m0 = jnp.full((1, L), NEG_INF, F32)
    l0 = jnp.zeros((1, L), F32)
    c_last = q0 // KV_CHUNK

    def body(c, carry):
        m, l_ = carry
        return _flash_update(scores(c), vt_ref[c], m, l_, acc_s)

    m, l_ = lax.fori_loop(0, c_last, body, (m0, l0))
    kpos = c_last * KV_CHUNK + lax.broadcasted_iota(jnp.int32, (KV_CHUNK, 1), 0)
    s = jnp.where(kpos <= tq, scores(c_last), NEG_INF)
    m, l_ = _flash_update(s, vt_ref[c_last], m, l_, acc_s)
    den = jnp.maximum(l_, 1e-30)
    outs = []
    for h in range(MOBA_HEADS):
        rows = slice(HEAD_DIM * h, HEAD_DIM * (h + 1))
        cols = slice(Q_TILE * h, Q_TILE * (h + 1))
        outs.append(acc_s[rows, cols] / den[:, cols])
    out_ref[...] = jnp.concatenate(outs, axis=0).T.astype(BF16)


def _moba_call(qtm, kmean, km, vtm, B, S):
    n_blk = S // MOBA_BLOCK
    L = MOBA_HEADS * Q_TILE
    return pl.pallas_call(
        _moba_kernel,
        out_shape=jax.ShapeDtypeStruct((B * S, MOBA_DIM), BF16),
        grid=(B, S // Q_TILE),
        in_specs=[
            pl.BlockSpec((None, MOBA_DIM, Q_TILE), lambda b, q: (b, 0, q)),
            pl.BlockSpec((None, n_blk, MOBA_DIM), lambda b, q: (b, 0, 0)),
            pl.BlockSpec((S, MOBA_DIM), lambda b, q: (b, 0)),
            pl.BlockSpec((None, S // KV_CHUNK, MOBA_DIM, KV_CHUNK), lambda b, q: (b, 0, 0, 0)),
        ],
        out_specs=pl.BlockSpec((Q_TILE, MOBA_DIM), lambda b, q: (b * (S // Q_TILE) + q, 0)),
        scratch_shapes=[
            pltpu.VMEM((MOBA_DIM, L), BF16),
            pltpu.VMEM((n_blk, 8, L), F32),
            pltpu.VMEM((MOBA_DIM, L), F32),
        ],
        compiler_params=pltpu.CompilerParams(
            dimension_semantics=("arbitrary", "arbitrary"), vmem_limit_bytes=VMEM_LIMIT),
        name="moba_attn",
    )(qtm, kmean, km, vtm)


def _merge_kernel(u_ref, halo_ref, b_ref, c_ref, gbr_ref, x_ref, pw_ref, ps_ref, wa_ref, wb_ref, wc_ref,
                  wo_ref, gam_ref, x1_ref, h2_ref, ext_s, *, tiles_per_seq):
    tm = u_ref.shape[0]
    i = pl.program_id(0)
    first = (i % tiles_per_seq) == 0
    u = u_ref[...]
    ext_s[0:POOL_MAXW, :] = jnp.where(first, 0.0, halo_ref[...])
    ext_s[POOL_MAXW:POOL_MAXW + tm, :] = u

    def tail_sum(col, k0, k1):
        tot = None
        for k in range(k0, k1):
            v = ext_s[pl.ds(POOL_MAXW - k, tm), 128 * col:128 * col + 128]
            tot = v if tot is None else tot + v
        return tot

    t_glob = (i % tiles_per_seq) * tm + lax.broadcasted_iota(jnp.int32, (tm, 1), 0)
    lane = lax.broadcasted_iota(jnp.int32, (1, 128), 1)
    low = lane < POOL_GROUP_DIM
    pooled = []
    for col in range(2):
        wa_, wb_ = POOL_WINDOWS[2 * col], POOL_WINDOWS[2 * col + 1]
        sa = tail_sum(col, 0, wa_)
        sb = sa + tail_sum(col, wa_, wb_)
        ca = jnp.minimum(t_glob + 1, wa_).astype(F32)
        cb = jnp.minimum(t_glob + 1, wb_).astype(F32)
        pooled.append(jnp.where(low, sa, sb) / jnp.where(low, ca, cb))
    d = jnp.concatenate(pooled, axis=1) - u
    a = _dot(d.astype(BF16), pw_ref[...]) * ps_ref[...]

    av = _dot(a.astype(BF16), wa_ref[...])
    bv = _dot(b_ref[...], wb_ref[...])
    cv = _dot(c_ref[...], wc_ref[...])
    merged = jax.nn.sigmoid(gbr_ref[:, 0:D_MODEL]) * av
    merged = merged + jax.nn.sigmoid(gbr_ref[:, D_MODEL:2 * D_MODEL]) * bv
    merged = merged + jax.nn.sigmoid(gbr_ref[:, 2 * D_MODEL:3 * D_MODEL]) * cv
    x1 = x_ref[...] + _dot(merged.astype(BF16), wo_ref[...])
    x1_ref[...] = x1
    h2_ref[...] = _rms(x1, gam_ref[...]).astype(BF16)


def _merge_call(upool, bn, cm, gbr, x2, pw, ps, wa, wb, wc, wo, gamma, B, S):
    N = B * S
    tm = ROW_TILE
    nt = S // tm
    hb = tm // POOL_MAXW
    row = lambda w_: pl.BlockSpec((tm, w_), lambda i: (i, 0))
    const = lambda shape: pl.BlockSpec(shape, lambda i: (0,) * len(shape))
    return pl.pallas_call(
        functools.partial(_merge_kernel, tiles_per_seq=nt),
        out_shape=(jax.ShapeDtypeStruct((N, D_MODEL), F32), jax.ShapeDtypeStruct((N, D_MODEL), BF16)),
        grid=(N // tm,),
        in_specs=[
            row(POOL_DIM),
            pl.BlockSpec((POOL_MAXW, POOL_DIM), lambda i: (jnp.maximum(i * hb - 1, 0), 0)),
            row(NSA_DIM),
            row(MOBA_DIM),
            row(3 * D_MODEL),
            row(D_MODEL),
            const((POOL_DIM, POOL_DIM)),
            const((1, POOL_DIM)),
            const((POOL_DIM, D_MODEL)),
            const((NSA_DIM, D_MODEL)),
            const((MOBA_DIM, D_MODEL)),
            const((D_MODEL, D_MODEL)),
            const((1, D_MODEL)),
        ],
        out_specs=(row(D_MODEL), row(D_MODEL)),
        scratch_shapes=[pltpu.VMEM((POOL_MAXW + tm, POOL_DIM), F32)],
        compiler_params=pltpu.CompilerParams(
            dimension_semantics=("arbitrary",), vmem_limit_bytes=VMEM_LIMIT),
        name="merge",
    )(upool, upool, bn, cm, gbr, x2, pw, ps, wa, wb, wc, wo, gamma)


def _ffn_kernel(h_ref, x_ref, wg_ref, wu_ref, wd_ref, gam_ref, out_ref, acc_s, *, final_norm):
    f = pl.program_id(1)

    @pl.when(f == 0)
    def _():
        acc_s[...] = jnp.zeros_like(acc_s)

    h = h_ref[...]
    g = _dot(h, wg_ref[...])
    u = _dot(h, wu_ref[...])
    act = (g * jax.nn.sigmoid(g)) * u
    acc_s[...] += _dot(act.astype(BF16), wd_ref[...])

    @pl.when(f == pl.num_programs(1) - 1)
    def _():
        y = x_ref[...] + acc_s[...]
        if final_norm:
            y = _rms(y, gam_ref[...])
        out_ref[...] = y


def _ffn_call(h2, x1, wg, wu, wd, gamma, final_norm):
    N = x1.shape[0]
    F = wg.shape[1]
    tm, tf = FFN_ROW_TILE, FFN_COL_TILE
    return pl.pallas_call(
        functools.partial(_ffn_kernel, final_norm=final_norm),
        out_shape=jax.ShapeDtypeStruct((N, D_MODEL), F32),
        grid=(N // tm, F // tf),
        in_specs=[
            pl.BlockSpec((tm, D_MODEL), lambda i, f: (i, 0)),
            pl.BlockSpec((tm, D_MODEL), lambda i, f: (i, 0)),
            pl.BlockSpec((D_MODEL, tf), lambda i, f: (0, f)),
            pl.BlockSpec((D_MODEL, tf), lambda i, f: (0, f)),
            pl.BlockSpec((tf, D_MODEL), lambda i, f: (f, 0)),
            pl.BlockSpec((1, D_MODEL), lambda i, f: (0, 0)),
        ],
        out_specs=pl.BlockSpec((tm, D_MODEL), lambda i, f: (i, 0)),
        scratch_shapes=[pltpu.VMEM((tm, D_MODEL), F32)],
        compiler_params=pltpu.CompilerParams(
            dimension_semantics=("arbitrary", "arbitrary"), vmem_limit_bytes=VMEM_LIMIT),
        name="ffn",
    )(h2, x1, wg, wu, wd, gamma)


def _reorder_in_proj(w, b):
    o_qn = POOL_DIM
    o_kv = o_qn + NSA_DIM
    o_gn = o_kv + 6 * 128
    o_mo = o_gn + 3 * NSA_HEADS
    o_gb = o_mo + 3 * MOBA_DIM
    pad = 128 - 3 * NSA_HEADS

    def pick(a):
        parts = [a[..., 0:o_qn], a[..., o_kv:o_kv + 256], a[..., o_qn:o_kv], a[..., o_kv + 256:o_gn],
                 a[..., o_gn:o_mo], jnp.zeros(a.shape[:-1] + (pad,), a.dtype), a[..., o_mo:o_gb], a[..., o_gb:]]
        return jnp.concatenate(parts, axis=-1)

    return pick(w), pick(b)


def _rope_tables(S):
    pos = jnp.arange(S, dtype=F32)
    inv_freq = ROPE_THETA ** (-jnp.arange(0, HEAD_DIM, 2, dtype=F32) / HEAD_DIM)
    ang = pos[:, None] * inv_freq[None, :]
    cos, sin = jnp.cos(ang), jnp.sin(ang)
    cos_t = jnp.tile(cos, (1, 4))
    sin_t = jnp.tile(jnp.concatenate([-sin, sin], axis=1), (1, 2))
    return cos_t, sin_t


def kernel(x, attn_norm, w_in, b_in, pool_w, pool_scale, cmp_pos, cmp_w1, cmp_b1, cmp_w2, cmp_b2,
           w_br_pool, w_br_nsa, w_br_moba, w_out, ffn_norm, w_gate, w_up, w_down, final_norm):
    B, S, D = x.shape
    depth = w_in.shape[0]
    assert D == D_MODEL and S % KV_CHUNK == 0 and S % ROW_TILE == 0 and S >= WINDOW + Q_TILE
    assert (B * S) % FFN_ROW_TILE == 0 and w_gate.shape[2] % FFN_COL_TILE == 0
    N = B * S
    R = S // CMP_STRIDE
    cos_t, sin_t = _rope_tables(S)
    x2 = x.reshape(N, D)
    for l in range(depth):
        w_all, b_all = _reorder_in_proj(w_in[l], b_in[l])
        (upool, cmpin, qtn, ksel, vselt, kwin, vwint, gt, qtm, km, vtm, kmean, gbr) = _proj_call(
            x2, attn_norm[l][None, :], w_all.astype(BF16), b_all[None, :], cos_t, sin_t, B, S)

        kr = cmpin.reshape(B, S, 4, HEAD_DIM).transpose(0, 2, 1, 3).reshape(B, 4, R, CMP_STRIDE * HEAD_DIM)
        kc, vct = _compress_call(
            kr, cmp_pos[l].reshape(2, 1, CMP_BLOCK * HEAD_DIM), cmp_w1[l].astype(BF16),
            cmp_b1[l][:, None, :], cmp_w2[l].astype(BF16), cmp_b2[l][:, None, :], B, R)

        bn = _nsa_call(qtn, gt, kc, vct, ksel, vselt, kwin, vwint, B, S)
        cm = _moba_call(qtm, kmean.reshape(B, S // MOBA_BLOCK, MOBA_DIM), km, vtm, B, S)

        pw_bd = jax.scipy.linalg.block_diag(*[pool_w[l, g] for g in range(len(POOL_WINDOWS))])
        x1, h2 = _merge_call(
            upool, bn, cm, gbr, x2, pw_bd.astype(BF16), pool_scale[l][None, :],
            w_br_pool[l].astype(BF16), w_br_nsa[l].astype(BF16), w_br_moba[l].astype(BF16),
            w_out[l].astype(BF16), ffn_norm[l][None, :], B, S)

        x2 = _ffn_call(h2, x1, w_gate[l].astype(BF16), w_up[l].astype(BF16), w_down[l].astype(BF16),
                       final_norm[None, :], final_norm=(l == depth - 1))
    return x2.reshape(B, S, D)
```

```python
import functools

import numpy as np
import jax
import jax.numpy as jnp
from jax import lax
from jax.experimental import pallas as pl
from jax.experimental.pallas import tpu as pltpu

F32 = jnp.float32
BF16 = jnp.bfloat16

D_MODEL = 1024
HEAD_DIM = 64
ROPE_THETA = 10000.0
RMS_EPS = 1e-6
NEG_INF = -1e30
FORCE = 1e30
LOG2E = 1.4426950408889634
LOOKAHEAD = 2
ONES_ROWS = 16
REMOVED = -3e38

POOL_WINDOWS = (2, 4, 8, 16)
POOL_GROUP_DIM = 64
POOL_DIM = 256
POOL_MAXW = 16

NSA_HEADS = 8
NSA_KV_GROUPS = 2
NSA_HPG = 4
NSA_DIM = 512
CMP_BLOCK = 32
CMP_STRIDE = 16
CMP_HIDDEN = 256
SEL_BLOCK = 64
SEL_COUNT = 16
WINDOW = 512

MOBA_HEADS = 4
MOBA_DIM = 256
MOBA_BLOCK = 256
MOBA_TOPK = 3

Q_TILE = 128
MOBA_Q_TILE = 512
KV_CHUNK = 512
KV_SUB = 256
ROW_TILE = 512
FFN_ROW_TILE = 1024
FFN_COL_TILE = 256
VMEM_LIMIT = 56 * 1024 * 1024

C_POOL = 0
C_CMP = 256
C_QN = 512
C_KS = 1024
C_VS = 1152
C_KW = 1280
C_VW = 1408
C_GN = 1536
C_QM = 1664
C_KM = 1920
C_VM = 2176
C_GBR = 2432
C_TOTAL = 5504


def _dot(a, b):
    return jnp.dot(a, b, preferred_element_type=F32)


def _rms(x, gamma):
    return x * lax.rsqrt(jnp.mean(x * x, axis=-1, keepdims=True) + RMS_EPS) * gamma


def _proj_kernel(x_ref, gam_ref, w_ref, b_ref, cos_ref, sin_ref,
                 upool_ref, cmpin_ref, qtn_ref, ksel_ref, vselt_ref, kwin_ref, vwint_ref, gt_ref,
                 qtm_ref, km_ref, vtm_ref, kmean_ref, gbr_ref):
    tm = x_ref.shape[0]
    h = _rms(x_ref[...], gam_ref[...]).astype(BF16)
    cos = cos_ref[...]
    sin = sin_ref[...]
    lane = lax.broadcasted_iota(jnp.int32, (tm, 128), 1)
    first_half = (lane & 32) == 0
    scale = HEAD_DIM ** -0.5 * LOG2E

    def seg(a, b):
        return _dot(h, w_ref[:, a:b]) + b_ref[:, a:b]

    def rope(y):
        swap = jnp.where(first_half, pltpu.roll(y, 96, 1), pltpu.roll(y, 32, 1))
        return y * cos + swap * sin

    y = seg(C_POOL, C_POOL + 512)
    upool_ref[...] = y[:, :256]
    cmpin_ref[...] = jnp.concatenate([rope(y[:, 256:384]), y[:, 384:512]], axis=1)

    y = seg(C_QN, C_QN + 512)
    for j in range(4):
        q = rope(y[:, 128 * j:128 * j + 128]) * scale
        qtn_ref[128 * j:128 * j + 128, :] = q.T.astype(BF16)

    y = seg(C_KS, C_KS + 512)
    ksel_ref[...] = rope(y[:, 0:128]).astype(BF16)
    vselt_ref[0] = y[:, 128:256].T.astype(BF16)
    kwin_ref[...] = rope(y[:, 256:384]).astype(BF16)
    vwt = y[:, 384:512].T.astype(BF16)
    for j in range(tm // 128):
        vwint_ref[j] = vwt[:, 128 * j:128 * j + 128]

    y = seg(C_GN, C_GN + 128)
    gt_ref[...] = y.T

    y = seg(C_QM, C_QM + 256)
    for j in range(2):
        q = rope(y[:, 128 * j:128 * j + 128]) * scale
        qtm_ref[128 * j:128 * j + 128, :] = q.T.astype(BF16)

    y = seg(C_KM, C_KM + 256)
    km = jnp.concatenate([rope(y[:, 0:128]), rope(y[:, 128:256])], axis=1)
    km_ref[...] = km.astype(BF16)
    nblk = tm // MOBA_BLOCK
    means = [jnp.sum(km[MOBA_BLOCK * j:MOBA_BLOCK * (j + 1), :], axis=0, keepdims=True) * (1.0 / MOBA_BLOCK)
             for j in range(nblk)]
    kmean_ref[...] = jnp.concatenate(means, axis=0)

    y = seg(C_VM, C_VM + 256)
    vtm_ref[0] = y.T.astype(BF16)

    for j in range(6):
        a = C_GBR + 512 * j
        gbr_ref[:, 512 * j:512 * j + 512] = seg(a, a + 512)


def _proj_call(x2, gamma, w, bias, cos_t, sin_t, B, S):
    N = B * S
    tm = ROW_TILE
    nt = S // tm
    f = lambda shape, dt: jax.ShapeDtypeStruct(shape, dt)
    out_shape = (
        f((N, 256), F32),
        f((N, 256), F32),
        f((B, NSA_DIM, S), BF16),
        f((N, 128), BF16),
        f((B, S // KV_CHUNK, 128, KV_CHUNK), BF16),
        f((N, 128), BF16),
        f((B, S // 128, 128, 128), BF16),
        f((B, 128, S), F32),
        f((B, MOBA_DIM, S), BF16),
        f((N, MOBA_DIM), BF16),
        f((B, S // KV_CHUNK, MOBA_DIM, KV_CHUNK), BF16),
        f((B, nt, tm // MOBA_BLOCK, MOBA_DIM), F32),
        f((N, 3 * D_MODEL), F32),
    )
    row = lambda w_: pl.BlockSpec((tm, w_), lambda i: (i, 0))
    const = lambda shape: pl.BlockSpec(shape, lambda i: (0,) * len(shape))
    in_specs = [
        row(D_MODEL),
        const((1, D_MODEL)),
        const((D_MODEL, C_TOTAL)),
        const((1, C_TOTAL)),
        pl.BlockSpec((tm, 128), lambda i: (i % nt, 0)),
        pl.BlockSpec((tm, 128), lambda i: (i % nt, 0)),
    ]
    out_specs = (
        row(256),
        row(256),
        pl.BlockSpec((None, NSA_DIM, tm), lambda i: (i // nt, 0, i % nt)),
        row(128),
        pl.BlockSpec((None, tm // KV_CHUNK, 128, KV_CHUNK), lambda i: (i // nt, i % nt, 0, 0)),
        row(128),
        pl.BlockSpec((None, tm // 128, 128, 128), lambda i: (i // nt, i % nt, 0, 0)),
        pl.BlockSpec((None, 128, tm), lambda i: (i // nt, 0, i % nt)),
        pl.BlockSpec((None, MOBA_DIM, tm), lambda i: (i // nt, 0, i % nt)),
        row(MOBA_DIM),
        pl.BlockSpec((None, tm // KV_CHUNK, MOBA_DIM, KV_CHUNK), lambda i: (i // nt, i % nt, 0, 0)),
        pl.BlockSpec((None, None, tm // MOBA_BLOCK, MOBA_DIM), lambda i: (i // nt, i % nt, 0, 0)),
        row(3 * D_MODEL),
    )
    return pl.pallas_call(
        _proj_kernel,
        out_shape=out_shape,
        grid=(N // tm,),
        in_specs=in_specs,
        out_specs=out_specs,
        compiler_params=pltpu.CompilerParams(
            dimension_semantics=("arbitrary",), vmem_limit_bytes=VMEM_LIMIT),
        name="proj",
    )(x2, gamma, w, bias, cos_t, sin_t)


def _gelu_tanh(x):
    return x * (0.5 * (1.0 + jnp.tanh(np.sqrt(2.0 / np.pi).astype(np.float32) * (x + 0.044715 * (x * x * x)))))


def _compress_kernel(kr_ref, pos_ref, w1_ref, b1_ref, w2_ref, b2_ref, kc_ref, vct_ref):
    R = kr_ref.shape[1]
    half = CMP_STRIDE * HEAD_DIM
    outs = []
    for t in range(2):
        for g in range(NSA_KV_GROUPS):
            kr = kr_ref[2 * t + g]
            lo = (kr + pos_ref[t, :, :half]).astype(BF16)
            hi = (kr + pos_ref[t, :, half:]).astype(BF16)
            a = _dot(lo, w1_ref[t, :half, :])
            bm = _dot(hi, w1_ref[t, half:, :])
            bm = pltpu.roll(bm, R - 1, 0)
            hid = _gelu_tanh(a + bm + b1_ref[t])
            outs.append(_dot(hid.astype(BF16), w2_ref[t]) + b2_ref[t])
    kc_ref[...] = jnp.concatenate(outs[0:2], axis=1).astype(BF16)
    vct_ref[...] = jnp.concatenate(outs[2:4], axis=1).T.astype(BF16)


def _compress_call(kr, pos, w1, b1, w2, b2, B, R):
    return pl.pallas_call(
        _compress_kernel,
        out_shape=(jax.ShapeDtypeStruct((B, R, 128), BF16), jax.ShapeDtypeStruct((B, 128, R), BF16)),
        grid=(B,),
        in_specs=[
            pl.BlockSpec((None, 4, R, CMP_STRIDE * HEAD_DIM), lambda b: (b, 0, 0, 0)),
            pl.BlockSpec((2, 1, CMP_BLOCK * HEAD_DIM), lambda b: (0, 0, 0)),
            pl.BlockSpec((2, CMP_BLOCK * HEAD_DIM, CMP_HIDDEN), lambda b: (0, 0, 0)),
            pl.BlockSpec((2, 1, CMP_HIDDEN), lambda b: (0, 0, 0)),
            pl.BlockSpec((2, CMP_HIDDEN, HEAD_DIM), lambda b: (0, 0, 0)),
            pl.BlockSpec((2, 1, HEAD_DIM), lambda b: (0, 0, 0)),
        ],
        out_specs=(pl.BlockSpec((None, R, 128), lambda b: (b, 0, 0)),
                   pl.BlockSpec((None, 128, R), lambda b: (b, 0, 0))),
        compiler_params=pltpu.CompilerParams(
            dimension_semantics=("arbitrary",), vmem_limit_bytes=VMEM_LIMIT),
        name="nsa_compress",
    )(kr, pos, w1, b1, w2, b2)


def _softmax_block(s):
    m = jnp.max(s, axis=0, keepdims=True)
    m_use = jnp.where(m < 0.5 * NEG_INF, 0.0, m)
    p = jnp.exp2(s - m_use)
    return p, jnp.sum(p, axis=0, keepdims=True)


def _flash_update(sc, rows, block, vt, m, acc_ref):
    n, L = sc.shape[0] // block, sc.shape[1]
    parts = [sc[block * j:block * (j + 1), :] for j in range(n)]
    tops = [jnp.max(parts[j].reshape(block // 8, 8, L), axis=0) + rows[j:j + 1, :] for j in range(n)]
    top = tops[0]
    for t in tops[1:]:
        top = jnp.maximum(top, t)
    m_new = jnp.maximum(m, jnp.max(top, axis=0, keepdims=True))
    m_use = jnp.where(m_new < 0.5 * NEG_INF, 0.0, m_new)
    alpha = jnp.exp2(m - m_use)
    p = jnp.concatenate([jnp.exp2(parts[j] + (rows[j:j + 1, :] - m_use)) for j in range(n)], axis=0)
    acc_ref[...] = alpha * acc_ref[...] + _dot(vt, p.astype(BF16))
    return m_new


def _topk_mask(scores, rows_f, k):
    work = scores
    sel = jnp.zeros(scores.shape, F32)
    for _ in range(k):
        cm = jnp.max(work, axis=0, keepdims=True)
        idx = jnp.min(jnp.where(work == cm, rows_f, 1e9), axis=0, keepdims=True)
        hit = rows_f == idx
        sel = jnp.where(hit, 1.0, sel)
        work = jnp.where(hit, REMOVED, work)
    return sel


def _add_block_bias(sc, rows, block):
    n = sc.shape[0] // block
    return jnp.concatenate([sc[block * j:block * (j + 1), :] + rows[j:j + 1, :] for j in range(n)], axis=0)


def _nsa_kernel(qt_ref, gt_ref, kc_ref, vct_ref, ksel_ref, vselt_ref, kwin_ref, vwint_ref, out_ref,
                qt_s, pg_s, bias_s, acc_s):
    qi = pl.program_id(1)
    q0 = qi * Q_TILE
    G = NSA_KV_GROUPS
    LG = NSA_HPG * Q_TILE
    n_cmp_rows = kc_ref.shape[0]
    n_sel = bias_s.shape[1]

    qt_s[...] = jnp.zeros_like(qt_s)
    for h in range(NSA_HEADS):
        g, hh = divmod(h, NSA_HPG)
        qt_s[g, HEAD_DIM * g:HEAD_DIM * (g + 1), Q_TILE * hh:Q_TILE * (hh + 1)] = (
            qt_ref[HEAD_DIM * h:HEAD_DIM * (h + 1), :])
    qts = [qt_s[g] for g in range(G)]
    tq = q0 + (lax.broadcasted_iota(jnp.int32, (1, LG), 1) & (Q_TILE - 1))
    tq1 = q0 + lax.broadcasted_iota(jnp.int32, (1, Q_TILE), 1)

    kc = kc_ref[...]
    n_idx = lax.broadcasted_iota(jnp.int32, (n_cmp_rows, 1), 0)
    cmp_ok = n_idx * CMP_STRIDE + (CMP_BLOCK - 1) <= tq
    nw = (WINDOW + Q_TILE) // 128
    wc0 = jnp.maximum(qi - WINDOW // Q_TILE, 0)
    wstart = wc0 * 128
    kw = kwin_ref[pl.ds(pl.multiple_of(wstart, 128), WINDOW + Q_TILE), :]
    kposw = wstart + lax.broadcasted_iota(jnp.int32, (WINDOW + Q_TILE, 1), 0)
    win_ok = (kposw <= tq) & (kposw > tq - WINDOW)
    vtw = jnp.concatenate([vwint_ref[wc0 + j] for j in range(nw)], axis=1)

    blk_i = lax.broadcasted_iota(jnp.int32, (n_sel, 1), 0)
    blk_f = jnp.broadcast_to(blk_i.astype(F32), (n_sel, Q_TILE))
    cur = tq1 // SEL_BLOCK
    forced = (blk_i == 0) | (blk_i == cur) | (blk_i == cur - 1)
    started = blk_i <= cur
    n_top = min(SEL_COUNT, n_sel)

    def sub_dots(kmat, g):
        n = kmat.shape[0]
        return jnp.concatenate([_dot(kmat[r:min(r + KV_SUB, n), :], qts[g]) for r in range(0, n, KV_SUB)], axis=0)

    o_cmp, o_win = [], []
    for g in range(G):
        rows = slice(HEAD_DIM * g, HEAD_DIM * (g + 1))
        s = jnp.where(cmp_ok, sub_dots(kc, g), NEG_INF)
        p, l = _softmax_block(s)
        den = jnp.maximum(l, 1e-30)
        o_cmp.append(_dot(vct_ref[rows, :], p.astype(BF16)) / den)
        pn = p / den
        pg = pn[:, 0:Q_TILE]
        for h in range(1, NSA_HPG):
            pg = pg + pn[:, h * Q_TILE:(h + 1) * Q_TILE]
        pg_s[g, 0:8, :] = jnp.zeros((8, Q_TILE), F32)
        pg_s[g, 8:8 + n_cmp_rows, :] = pg

        s = jnp.where(win_ok, sub_dots(kw, g), NEG_INF)
        p, l = _softmax_block(s)
        o_win.append(_dot(vtw[rows, :], p.astype(BF16)) / jnp.maximum(l, 1e-30))

    for g in range(G):
        def tap(w):
            return pg_s[g, pl.ds(8 + w, n_sel, stride=SEL_BLOCK // CMP_STRIDE), :]

        t = tap(-1) + 2.0 * tap(0)
        t = t + 2.0 * tap(1)
        t = t + 2.0 * tap(2)
        t = t + tap(3)
        s_slc = jnp.where(forced, FORCE, jnp.where(started, t, NEG_INF))
        sel = _topk_mask(s_slc, blk_f, n_top)
        bias = jnp.where(sel > 0.5, 0.0, NEG_INF)
        bias_s[g] = jnp.concatenate([bias] * NSA_HPG, axis=1)

    bpc = KV_CHUNK // SEL_BLOCK
    c_last = q0 // KV_CHUNK

    bps = KV_SUB // SEL_BLOCK

    steps = [(sub, g) for sub in range(KV_CHUNK // KV_SUB) for g in range(G)]

    ones = jnp.ones((ONES_ROWS, KV_SUB), BF16)

    def scores(c, sub, g, diagonal):
        base = pl.multiple_of(c * KV_CHUNK, KV_CHUNK) + sub * KV_SUB
        k = ksel_ref[pl.ds(pl.multiple_of(base, KV_SUB), KV_SUB), :]
        s = _dot(k, qts[g])
        if diagonal:
            kpos = base + lax.broadcasted_iota(jnp.int32, (KV_SUB, 1), 0)
            s = jnp.where(kpos <= tq, s, NEG_INF)
        return s

    def chunk_steps(c, ms, pending, diagonal):
        ms, pending = list(ms), list(pending)
        for t, (sub, g) in enumerate(steps):
            s = pending.pop(0)
            u = t + LOOKAHEAD
            if u < len(steps):
                pending.append(scores(c, *steps[u], diagonal))
            elif not diagonal:
                pending.append(scores(c + 1, *steps[u - len(steps)], False))
            rows = bias_s[g, pl.ds(pl.multiple_of(c * bpc, bpc), bpc), :][bps * sub:bps * (sub + 1), :]
            vt = vselt_ref[c, HEAD_DIM * g:HEAD_DIM * (g + 1), KV_SUB * sub:KV_SUB * (sub + 1)]
            ms[g] = _flash_update(s, rows, SEL_BLOCK, jnp.concatenate([vt, ones], axis=0), ms[g], acc_s.at[g])
        return tuple(ms), tuple(pending)

    acc_s[...] = jnp.zeros_like(acc_s)
    init = (jnp.full((1, LG), NEG_INF, F32),) * G

    def body(c, carry):
        ms, pending = chunk_steps(c, carry[:G], carry[G:], False)
        return ms + pending

    carry = lax.fori_loop(0, c_last, body, init + tuple(scores(0, *steps[i], False) for i in range(LOOKAHEAD)))
    chunk_steps(c_last, carry[:G], [scores(c_last, *steps[i], True) for i in range(LOOKAHEAD)], True)

    gates = jax.nn.sigmoid(gt_ref[0:3 * NSA_HEADS, :])
    outs = []
    for h in range(NSA_HEADS):
        g, hh = divmod(h, NSA_HPG)
        cols = slice(Q_TILE * hh, Q_TILE * (hh + 1))
        o_sel = acc_s[g, 0:HEAD_DIM, cols] / jnp.maximum(acc_s[g, HEAD_DIM:HEAD_DIM + 1, cols], 1e-30)
        o = gates[3 * h:3 * h + 1, :] * o_cmp[g][:, cols]
        o = o + gates[3 * h + 1:3 * h + 2, :] * o_sel
        o = o + gates[3 * h + 2:3 * h + 3, :] * o_win[g][:, cols]
        outs.append(o)
    out_ref[...] = jnp.concatenate(outs, axis=0).T.astype(BF16)


def _nsa_call(qtn, gt, kc, vct, ksel, vselt, kwin, vwint, B, S):
    R = S // CMP_STRIDE
    n_sel = S // SEL_BLOCK
    LG = NSA_HPG * Q_TILE
    return pl.pallas_call(
        _nsa_kernel,
        out_shape=jax.ShapeDtypeStruct((B * S, NSA_DIM), BF16),
        grid=(B, S // Q_TILE),
        in_specs=[
            pl.BlockSpec((None, NSA_DIM, Q_TILE), lambda b, q: (b, 0, q)),
            pl.BlockSpec((None, 128, Q_TILE), lambda b, q: (b, 0, q)),
            pl.BlockSpec((None, R, 128), lambda b, q: (b, 0, 0)),
            pl.BlockSpec((None, 128, R), lambda b, q: (b, 0, 0)),
            pl.BlockSpec((S, 128), lambda b, q: (b, 0)),
            pl.BlockSpec((None, S // KV_CHUNK, 128, KV_CHUNK), lambda b, q: (b, 0, 0, 0)),
            pl.BlockSpec((S, 128), lambda b, q: (b, 0)),
            pl.BlockSpec((None, S // 128, 128, 128), lambda b, q: (b, 0, 0, 0)),
        ],
        out_specs=pl.BlockSpec((Q_TILE, NSA_DIM), lambda b, q: (b * (S // Q_TILE) + q, 0)),
        scratch_shapes=[
            pltpu.VMEM((NSA_KV_GROUPS, 128, LG), BF16),
            pltpu.VMEM((NSA_KV_GROUPS, 8 + R, Q_TILE), F32),
            pltpu.VMEM((NSA_KV_GROUPS, n_sel, LG), F32),
            pltpu.VMEM((NSA_KV_GROUPS, HEAD_DIM + ONES_ROWS, LG), F32),
        ],
        compiler_params=pltpu.CompilerParams(
            dimension_semantics=("arbitrary", "arbitrary"), vmem_limit_bytes=VMEM_LIMIT),
        name="nsa_attn",
    )(qtn, gt, kc, vct, ksel, vselt, kwin, vwint)


def _moba_kernel(qt_ref, kmean_ref, k_ref, vt_ref, out_ref, qt_s, bias_s, acc_s):
    qi = pl.program_id(1)
    q0 = qi * MOBA_Q_TILE
    L = MOBA_Q_TILE
    H = MOBA_HEADS
    n_blk = kmean_ref.shape[0]
    n_top = max(1, min(MOBA_TOPK, n_blk - 1))

    qt_s[...] = jnp.zeros_like(qt_s)
    for h in range(H):
        rows = slice(HEAD_DIM * h, HEAD_DIM * (h + 1))
        qt_s[h, rows, :] = qt_ref[rows, :]
    qts = [qt_s[h] for h in range(H)]
    tq = q0 + lax.broadcasted_iota(jnp.int32, (1, L), 1)

    own = tq // MOBA_BLOCK
    blk_i = lax.broadcasted_iota(jnp.int32, (n_blk, 1), 0)
    blk_f = jnp.broadcast_to(blk_i.astype(F32), (n_blk, L))
    kmean = kmean_ref[...].astype(BF16)
    for h in range(H):
        sg = jnp.where(blk_i < own, _dot(kmean, qts[h]), NEG_INF)
        sel = _topk_mask(sg, blk_f, n_top)
        allow = ((sel > 0.5) & (blk_i < own)) | (blk_i == own)
        bias = jnp.where(allow, 0.0, NEG_INF)
        for j in range(n_blk):
            bias_s[h, j] = jnp.broadcast_to(bias[j:j + 1, :], (8, L))

    bpc = KV_CHUNK // MOBA_BLOCK

    steps = [(j, h) for j in range(bpc) for h in range(H)]

    ones = jnp.ones((ONES_ROWS, MOBA_BLOCK), BF16)

    def scores(c, j, h, diagonal):
        base = pl.multiple_of(c * KV_CHUNK, KV_CHUNK) + j * MOBA_BLOCK
        k = k_ref[pl.ds(pl.multiple_of(base, MOBA_BLOCK), MOBA_BLOCK), :]
        s = _dot(k, qts[h])
        if diagonal:
            kpos = base + lax.broadcasted_iota(jnp.int32, (MOBA_BLOCK, 1), 0)
            s = jnp.where(kpos <= tq, s, NEG_INF)
        return s

    def chunk_steps(c, ms, pending, diagonal):
        ms, pending = list(ms), list(pending)
        for t, (j, h) in enumerate(steps):
            s = pending.pop(0)
            u = t + LOOKAHEAD
            if u < len(steps):
                pending.append(scores(c, *steps[u], diagonal))
            elif not diagonal:
                pending.append(scores(c + 1, *steps[u - len(steps)], False))
            vt = vt_ref[c, HEAD_DIM * h:HEAD_DIM * (h + 1), MOBA_BLOCK * j:MOBA_BLOCK * (j + 1)]
            ms[h] = _flash_update(s, bias_s[h, c * bpc + j], MOBA_BLOCK, jnp.concatenate([vt, ones], axis=0),
                                  ms[h], acc_s.at[h])
        return tuple(ms), tuple(pending)

    acc_s[...] = jnp.zeros_like(acc_s)
    c_last = q0 // KV_CHUNK
    init = (jnp.full((1, L), NEG_INF, F32),) * H

    def body(c, carry):
        ms, pending = chunk_steps(c, carry[:H], carry[H:], False)
        return ms + pending

    carry = lax.fori_loop(0, c_last, body, init + tuple(scores(0, *steps[i], False) for i in range(LOOKAHEAD)))
    chunk_steps(c_last, carry[:H], [scores(c_last, *steps[i], True) for i in range(LOOKAHEAD)], True)
    outs = [acc_s[h, 0:HEAD_DIM, :] / jnp.maximum(acc_s[h, HEAD_DIM:HEAD_DIM + 1, :], 1e-30) for h in range(H)]
    out_ref[...] = jnp.concatenate(outs, axis=0).T.astype(BF16)


def _moba_call(qtm, kmean, km, vtm, B, S):
    n_blk = S // MOBA_BLOCK
    L = MOBA_Q_TILE
    return pl.pallas_call(
        _moba_kernel,
        out_shape=jax.ShapeDtypeStruct((B * S, MOBA_DIM), BF16),
        grid=(B, S // L),
        in_specs=[
            pl.BlockSpec((None, MOBA_DIM, L), lambda b, q: (b, 0, q)),
            pl.BlockSpec((None, n_blk, MOBA_DIM), lambda b, q: (b, 0, 0)),
            pl.BlockSpec((S, MOBA_DIM), lambda b, q: (b, 0)),
            pl.BlockSpec((None, S // KV_CHUNK, MOBA_DIM, KV_CHUNK), lambda b, q: (b, 0, 0, 0)),
        ],
        out_specs=pl.BlockSpec((L, MOBA_DIM), lambda b, q: (b * (S // L) + q, 0)),
        scratch_shapes=[
            pltpu.VMEM((MOBA_HEADS, MOBA_DIM, L), BF16),
            pltpu.VMEM((MOBA_HEADS, n_blk, 8, L), F32),
            pltpu.VMEM((MOBA_HEADS, HEAD_DIM + ONES_ROWS, L), F32),
        ],
        compiler_params=pltpu.CompilerParams(
            dimension_semantics=("arbitrary", "arbitrary"), vmem_limit_bytes=VMEM_LIMIT),
        name="moba_attn",
    )(qtm, kmean, km, vtm)


def _merge_kernel(u_ref, halo_ref, b_ref, c_ref, gbr_ref, x_ref, pw_ref, ps_ref, wa_ref, wb_ref, wc_ref,
                  wo_ref, gam_ref, x1_ref, h2_ref, ext_s, *, tiles_per_seq):
    tm = u_ref.shape[0]
    i = pl.program_id(0)
    first = (i % tiles_per_seq) == 0
    u = u_ref[...]
    ext_s[0:POOL_MAXW, :] = jnp.where(first, 0.0, halo_ref[...])
    ext_s[POOL_MAXW:POOL_MAXW + tm, :] = u

    def tail_sum(col, k0, k1):
        tot = None
        for k in range(k0, k1):
            v = ext_s[pl.ds(POOL_MAXW - k, tm), 128 * col:128 * col + 128]
            tot = v if tot is None else tot + v
        return tot

    t_glob = (i % tiles_per_seq) * tm + lax.broadcasted_iota(jnp.int32, (tm, 1), 0)
    lane = lax.broadcasted_iota(jnp.int32, (1, 128), 1)
    low = lane < POOL_GROUP_DIM
    pooled = []
    for col in range(2):
        wa_, wb_ = POOL_WINDOWS[2 * col], POOL_WINDOWS[2 * col + 1]
        sa = tail_sum(col, 0, wa_)
        sb = sa + tail_sum(col, wa_, wb_)
        ca = jnp.minimum(t_glob + 1, wa_).astype(F32)
        cb = jnp.minimum(t_glob + 1, wb_).astype(F32)
        pooled.append(jnp.where(low, sa, sb) / jnp.where(low, ca, cb))
    d = jnp.concatenate(pooled, axis=1) - u
    a = _dot(d.astype(BF16), pw_ref[...]) * ps_ref[...]

    av = _dot(a.astype(BF16), wa_ref[...])
    bv = _dot(b_ref[...], wb_ref[...])
    cv = _dot(c_ref[...], wc_ref[...])
    merged = jax.nn.sigmoid(gbr_ref[:, 0:D_MODEL]) * av
    merged = merged + jax.nn.sigmoid(gbr_ref[:, D_MODEL:2 * D_MODEL]) * bv
    merged = merged + jax.nn.sigmoid(gbr_ref[:, 2 * D_MODEL:3 * D_MODEL]) * cv
    x1 = x_ref[...] + _dot(merged.astype(BF16), wo_ref[...])
    x1_ref[...] = x1
    h2_ref[...] = _rms(x1, gam_ref[...]).astype(BF16)


def _merge_call(upool, bn, cm, gbr, x2, pw, ps, wa, wb, wc, wo, gamma, B, S):
    N = B * S
    tm = ROW_TILE
    nt = S // tm
    hb = tm // POOL_MAXW
    row = lambda w_: pl.BlockSpec((tm, w_), lambda i: (i, 0))
    const = lambda shape: pl.BlockSpec(shape, lambda i: (0,) * len(shape))
    return pl.pallas_call(
        functools.partial(_merge_kernel, tiles_per_seq=nt),
        out_shape=(jax.ShapeDtypeStruct((N, D_MODEL), F32), jax.ShapeDtypeStruct((N, D_MODEL), BF16)),
        grid=(N // tm,),
        in_specs=[
            row(POOL_DIM),
            pl.BlockSpec((POOL_MAXW, POOL_DIM), lambda i: (jnp.maximum(i * hb - 1, 0), 0)),
            row(NSA_DIM),
            row(MOBA_DIM),
            row(3 * D_MODEL),
            row(D_MODEL),
            const((POOL_DIM, POOL_DIM)),
            const((1, POOL_DIM)),
            const((POOL_DIM, D_MODEL)),
            const((NSA_DIM, D_MODEL)),
            const((MOBA_DIM, D_MODEL)),
            const((D_MODEL, D_MODEL)),
            const((1, D_MODEL)),
        ],
        out_specs=(row(D_MODEL), row(D_MODEL)),
        scratch_shapes=[pltpu.VMEM((POOL_MAXW + tm, POOL_DIM), F32)],
        compiler_params=pltpu.CompilerParams(
            dimension_semantics=("arbitrary",), vmem_limit_bytes=VMEM_LIMIT),
        name="merge",
    )(upool, upool, bn, cm, gbr, x2, pw, ps, wa, wb, wc, wo, gamma)


def _ffn_kernel(h_ref, x_ref, wg_ref, wu_ref, wd_ref, gam_ref, out_ref, acc_s, *, final_norm):
    f = pl.program_id(1)

    @pl.when(f == 0)
    def _():
        acc_s[...] = jnp.zeros_like(acc_s)

    h = h_ref[...]
    g = _dot(h, wg_ref[...])
    u = _dot(h, wu_ref[...])
    act = (g * jax.nn.sigmoid(g)) * u
    acc_s[...] += _dot(act.astype(BF16), wd_ref[...])

    @pl.when(f == pl.num_programs(1) - 1)
    def _():
        y = x_ref[...] + acc_s[...]
        if final_norm:
            y = _rms(y, gam_ref[...])
        out_ref[...] = y


def _ffn_call(h2, x1, wg, wu, wd, gamma, final_norm):
    N = x1.shape[0]
    F = wg.shape[1]
    tm, tf = FFN_ROW_TILE, FFN_COL_TILE
    return pl.pallas_call(
        functools.partial(_ffn_kernel, final_norm=final_norm),
        out_shape=jax.ShapeDtypeStruct((N, D_MODEL), F32),
        grid=(N // tm, F // tf),
        in_specs=[
            pl.BlockSpec((tm, D_MODEL), lambda i, f: (i, 0)),
            pl.BlockSpec((tm, D_MODEL), lambda i, f: (i, 0)),
            pl.BlockSpec((D_MODEL, tf), lambda i, f: (0, f)),
            pl.BlockSpec((D_MODEL, tf), lambda i, f: (0, f)),
            pl.BlockSpec((tf, D_MODEL), lambda i, f: (f, 0)),
            pl.BlockSpec((1, D_MODEL), lambda i, f: (0, 0)),
        ],
        out_specs=pl.BlockSpec((tm, D_MODEL), lambda i, f: (i, 0)),
        scratch_shapes=[pltpu.VMEM((tm, D_MODEL), F32)],
        compiler_params=pltpu.CompilerParams(
            dimension_semantics=("arbitrary", "arbitrary"), vmem_limit_bytes=VMEM_LIMIT),
        name="ffn",
    )(h2, x1, wg, wu, wd, gamma)


def _reorder_in_proj(w, b):
    o_qn = POOL_DIM
    o_kv = o_qn + NSA_DIM
    o_gn = o_kv + 6 * 128
    o_mo = o_gn + 3 * NSA_HEADS
    o_gb = o_mo + 3 * MOBA_DIM
    pad = 128 - 3 * NSA_HEADS

    def pick(a):
        parts = [a[..., 0:o_qn], a[..., o_kv:o_kv + 256], a[..., o_qn:o_kv], a[..., o_kv + 256:o_gn],
                 a[..., o_gn:o_mo], jnp.zeros(a.shape[:-1] + (pad,), a.dtype), a[..., o_mo:o_gb], a[..., o_gb:]]
        return jnp.concatenate(parts, axis=-1)

    return pick(w), pick(b)


def _rope_tables(S):
    pos = jnp.arange(S, dtype=F32)
    inv_freq = ROPE_THETA ** (-jnp.arange(0, HEAD_DIM, 2, dtype=F32) / HEAD_DIM)
    ang = pos[:, None] * inv_freq[None, :]
    cos, sin = jnp.cos(ang), jnp.sin(ang)
    cos_t = jnp.tile(cos, (1, 4))
    sin_t = jnp.tile(jnp.concatenate([-sin, sin], axis=1), (1, 2))
    return cos_t, sin_t


def kernel(x, attn_norm, w_in, b_in, pool_w, pool_scale, cmp_pos, cmp_w1, cmp_b1, cmp_w2, cmp_b2,
           w_br_pool, w_br_nsa, w_br_moba, w_out, ffn_norm, w_gate, w_up, w_down, final_norm):
    B, S, D = x.shape
    depth = w_in.shape[0]
    assert D == D_MODEL and S % KV_CHUNK == 0 and S % ROW_TILE == 0 and S >= WINDOW + Q_TILE
    assert (B * S) % FFN_ROW_TILE == 0 and w_gate.shape[2] % FFN_COL_TILE == 0
    N = B * S
    R = S // CMP_STRIDE
    cos_t, sin_t = _rope_tables(S)
    x2 = x.reshape(N, D)
    for l in range(depth):
        w_all, b_all = _reorder_in_proj(w_in[l], b_in[l])
        (upool, cmpin, qtn, ksel, vselt, kwin, vwint, gt, qtm, km, vtm, kmean, gbr) = _proj_call(
            x2, attn_norm[l][None, :], w_all.astype(BF16), b_all[None, :], cos_t, sin_t, B, S)

        kr = cmpin.reshape(B, S, 4, HEAD_DIM).transpose(0, 2, 1, 3).reshape(B, 4, R, CMP_STRIDE * HEAD_DIM)
        kc, vct = _compress_call(
            kr, cmp_pos[l].reshape(2, 1, CMP_BLOCK * HEAD_DIM), cmp_w1[l].astype(BF16),
            cmp_b1[l][:, None, :], cmp_w2[l].astype(BF16), cmp_b2[l][:, None, :], B, R)

        bn = _nsa_call(qtn, gt, kc, vct, ksel, vselt, kwin, vwint, B, S)
        cm = _moba_call(qtm, kmean.reshape(B, S // MOBA_BLOCK, MOBA_DIM), km, vtm, B, S)

        pw_bd = jax.scipy.linalg.block_diag(*[pool_w[l, g] for g in range(len(POOL_WINDOWS))])
        x1, h2 = _merge_call(
            upool, bn, cm, gbr, x2, pw_bd.astype(BF16), pool_scale[l][None, :],
            w_br_pool[l].astype(BF16), w_br_nsa[l].astype(BF16), w_br_moba[l].astype(BF16),
            w_out[l].astype(BF16), ffn_norm[l][None, :], B, S)

        x2 = _ffn_call(h2, x1, w_gate[l].astype(BF16), w_up[l].astype(BF16), w_down[l].astype(BF16),
                       final_norm[None, :], final_norm=(l == depth - 1))
    return x2.reshape(B, S, D)
```

```python
import functools

import numpy as np
import jax
import jax.numpy as jnp
from jax import lax
from jax.experimental import pallas as pl
from jax.experimental.pallas import tpu as pltpu

F32 = jnp.float32
BF16 = jnp.bfloat16

D_MODEL = 1024
HEAD_DIM = 64
ROPE_THETA = 10000.0
RMS_EPS = 1e-6
NEG_INF = -1e30
FORCE = 1e30
LOG2E = 1.4426950408889634
LOOKAHEAD = 2
ONES_ROWS = 16
REMOVED = -(2.0 ** 127)

POOL_WINDOWS = (2, 4, 8, 16)
POOL_GROUP_DIM = 64
POOL_DIM = 256
POOL_MAXW = 16

NSA_HEADS = 8
NSA_KV_GROUPS = 2
NSA_HPG = 4
NSA_DIM = 512
CMP_BLOCK = 32
CMP_STRIDE = 16
CMP_HIDDEN = 256
SEL_BLOCK = 64
SEL_COUNT = 16
N_FORCED = 3
WINDOW = 512

MOBA_HEADS = 4
MOBA_DIM = 256
MOBA_BLOCK = 256
MOBA_TOPK = 3

Q_TILE = 128
MOBA_Q_TILE = 512
KV_CHUNK = 512
KV_SUB = 256
ROW_TILE = 512
FFN_ROW_TILE = 1024
FFN_COL_TILE = 256
VMEM_LIMIT = 56 * 1024 * 1024

C_POOL = 0
C_CMP = 256
C_QN = 512
C_KS = 1024
C_VS = 1152
C_KW = 1280
C_VW = 1408
C_GN = 1536
C_QM = 1664
C_KM = 1920
C_VM = 2176
C_GBR = 2432
C_TOTAL = 5504


def _dot(a, b):
    return jnp.dot(a, b, preferred_element_type=F32)


def _rms(x, gamma):
    return x * lax.rsqrt(jnp.mean(x * x, axis=-1, keepdims=True) + RMS_EPS) * gamma


def _proj_kernel(x_ref, gam_ref, w_ref, b_ref, cos_ref, sin_ref,
                 upool_ref, cmpin_ref, qtn_ref, ksel_ref, vselt_ref, kwin_ref, vwint_ref, gt_ref,
                 qtm_ref, km_ref, vtm_ref, kmean_ref, gbr_ref):
    tm = x_ref.shape[0]
    h = _rms(x_ref[...], gam_ref[...]).astype(BF16)
    cos = cos_ref[...]
    sin = sin_ref[...]
    lane = lax.broadcasted_iota(jnp.int32, (tm, 128), 1)
    first_half = (lane & 32) == 0
    scale = HEAD_DIM ** -0.5 * LOG2E

    def seg(a, b):
        return _dot(h, w_ref[:, a:b]) + b_ref[:, a:b]

    def rope(y):
        swap = jnp.where(first_half, pltpu.roll(y, 96, 1), pltpu.roll(y, 32, 1))
        return y * cos + swap * sin

    y = seg(C_POOL, C_POOL + 512)
    upool_ref[...] = y[:, :256]
    cmpin_ref[...] = jnp.concatenate([rope(y[:, 256:384]), y[:, 384:512]], axis=1)

    y = seg(C_QN, C_QN + 512)
    for j in range(4):
        q = rope(y[:, 128 * j:128 * j + 128]) * scale
        qtn_ref[128 * j:128 * j + 128, :] = q.T.astype(BF16)

    y = seg(C_KS, C_KS + 512)
    ksel_ref[...] = rope(y[:, 0:128]).astype(BF16)
    vselt_ref[0] = y[:, 128:256].T.astype(BF16)
    kwin_ref[...] = rope(y[:, 256:384]).astype(BF16)
    vwt = y[:, 384:512].T.astype(BF16)
    for j in range(tm // 128):
        vwint_ref[j] = vwt[:, 128 * j:128 * j + 128]

    y = seg(C_GN, C_GN + 128)
    gt_ref[...] = y.T

    y = seg(C_QM, C_QM + 256)
    for j in range(2):
        q = rope(y[:, 128 * j:128 * j + 128]) * scale
        qtm_ref[128 * j:128 * j + 128, :] = q.T.astype(BF16)

    y = seg(C_KM, C_KM + 256)
    km = jnp.concatenate([rope(y[:, 0:128]), rope(y[:, 128:256])], axis=1)
    km_ref[...] = km.astype(BF16)
    nblk = tm // MOBA_BLOCK
    means = [jnp.sum(km[MOBA_BLOCK * j:MOBA_BLOCK * (j + 1), :], axis=0, keepdims=True) * (1.0 / MOBA_BLOCK)
             for j in range(nblk)]
    kmean_ref[...] = jnp.concatenate(means, axis=0)

    y = seg(C_VM, C_VM + 256)
    vtm_ref[0] = y.T.astype(BF16)

    for j in range(6):
        a = C_GBR + 512 * j
        gbr_ref[:, 512 * j:512 * j + 512] = seg(a, a + 512)


def _proj_call(x2, gamma, w, bias, cos_t, sin_t, B, S):
    N = B * S
    tm = ROW_TILE
    nt = S // tm
    f = lambda shape, dt: jax.ShapeDtypeStruct(shape, dt)
    out_shape = (
        f((N, 256), F32),
        f((N, 256), F32),
        f((B, NSA_DIM, S), BF16),
        f((N, 128), BF16),
        f((B, S // KV_CHUNK, 128, KV_CHUNK), BF16),
        f((N, 128), BF16),
        f((B, S // 128, 128, 128), BF16),
        f((B, 128, S), F32),
        f((B, MOBA_DIM, S), BF16),
        f((N, MOBA_DIM), BF16),
        f((B, S // KV_CHUNK, MOBA_DIM, KV_CHUNK), BF16),
        f((B, nt, tm // MOBA_BLOCK, MOBA_DIM), F32),
        f((N, 3 * D_MODEL), F32),
    )
    row = lambda w_: pl.BlockSpec((tm, w_), lambda i: (i, 0))
    const = lambda shape: pl.BlockSpec(shape, lambda i: (0,) * len(shape))
    in_specs = [
        row(D_MODEL),
        const((1, D_MODEL)),
        const((D_MODEL, C_TOTAL)),
        const((1, C_TOTAL)),
        pl.BlockSpec((tm, 128), lambda i: (i % nt, 0)),
        pl.BlockSpec((tm, 128), lambda i: (i % nt, 0)),
    ]
    out_specs = (
        row(256),
        row(256),
        pl.BlockSpec((None, NSA_DIM, tm), lambda i: (i // nt, 0, i % nt)),
        row(128),
        pl.BlockSpec((None, tm // KV_CHUNK, 128, KV_CHUNK), lambda i: (i // nt, i % nt, 0, 0)),
        row(128),
        pl.BlockSpec((None, tm // 128, 128, 128), lambda i: (i // nt, i % nt, 0, 0)),
        pl.BlockSpec((None, 128, tm), lambda i: (i // nt, 0, i % nt)),
        pl.BlockSpec((None, MOBA_DIM, tm), lambda i: (i // nt, 0, i % nt)),
        row(MOBA_DIM),
        pl.BlockSpec((None, tm // KV_CHUNK, MOBA_DIM, KV_CHUNK), lambda i: (i // nt, i % nt, 0, 0)),
        pl.BlockSpec((None, None, tm // MOBA_BLOCK, MOBA_DIM), lambda i: (i // nt, i % nt, 0, 0)),
        row(3 * D_MODEL),
    )
    return pl.pallas_call(
        _proj_kernel,
        out_shape=out_shape,
        grid=(N // tm,),
        in_specs=in_specs,
        out_specs=out_specs,
        compiler_params=pltpu.CompilerParams(
            dimension_semantics=("arbitrary",), vmem_limit_bytes=VMEM_LIMIT),
        name="proj",
    )(x2, gamma, w, bias, cos_t, sin_t)


def _gelu_tanh(x):
    return x * (0.5 * (1.0 + jnp.tanh(np.sqrt(2.0 / np.pi).astype(np.float32) * (x + 0.044715 * (x * x * x)))))


def _compress_kernel(kr_ref, pos_ref, w1_ref, b1_ref, w2_ref, b2_ref, kc_ref, vct_ref):
    R = kr_ref.shape[1]
    half = CMP_STRIDE * HEAD_DIM
    outs = []
    for t in range(2):
        for g in range(NSA_KV_GROUPS):
            kr = kr_ref[2 * t + g]
            lo = (kr + pos_ref[t, :, :half]).astype(BF16)
            hi = (kr + pos_ref[t, :, half:]).astype(BF16)
            a = _dot(lo, w1_ref[t, :half, :])
            bm = _dot(hi, w1_ref[t, half:, :])
            bm = pltpu.roll(bm, R - 1, 0)
            hid = _gelu_tanh(a + bm + b1_ref[t])
            outs.append(_dot(hid.astype(BF16), w2_ref[t]) + b2_ref[t])
    kc_ref[...] = jnp.concatenate(outs[0:2], axis=1).astype(BF16)
    vct_ref[...] = jnp.concatenate(outs[2:4], axis=1).T.astype(BF16)


def _compress_call(kr, pos, w1, b1, w2, b2, B, R):
    return pl.pallas_call(
        _compress_kernel,
        out_shape=(jax.ShapeDtypeStruct((B, R, 128), BF16), jax.ShapeDtypeStruct((B, 128, R), BF16)),
        grid=(B,),
        in_specs=[
            pl.BlockSpec((None, 4, R, CMP_STRIDE * HEAD_DIM), lambda b: (b, 0, 0, 0)),
            pl.BlockSpec((2, 1, CMP_BLOCK * HEAD_DIM), lambda b: (0, 0, 0)),
            pl.BlockSpec((2, CMP_BLOCK * HEAD_DIM, CMP_HIDDEN), lambda b: (0, 0, 0)),
            pl.BlockSpec((2, 1, CMP_HIDDEN), lambda b: (0, 0, 0)),
            pl.BlockSpec((2, CMP_HIDDEN, HEAD_DIM), lambda b: (0, 0, 0)),
            pl.BlockSpec((2, 1, HEAD_DIM), lambda b: (0, 0, 0)),
        ],
        out_specs=(pl.BlockSpec((None, R, 128), lambda b: (b, 0, 0)),
                   pl.BlockSpec((None, 128, R), lambda b: (b, 0, 0))),
        compiler_params=pltpu.CompilerParams(
            dimension_semantics=("arbitrary",), vmem_limit_bytes=VMEM_LIMIT),
        name="nsa_compress",
    )(kr, pos, w1, b1, w2, b2)


def _softmax_block(s):
    m = jnp.max(s, axis=0, keepdims=True)
    m_use = jnp.where(m < 0.5 * NEG_INF, 0.0, m)
    p = jnp.exp2(s - m_use)
    return p, jnp.sum(p, axis=0, keepdims=True)


def _flash_update(sc, rows, block, vt, m, acc_ref):
    n, L = sc.shape[0] // block, sc.shape[1]
    parts = [sc[block * j:block * (j + 1), :] for j in range(n)]
    tops = [jnp.max(parts[j].reshape(block // 8, 8, L), axis=0) for j in range(n)]
    if rows is not None:
        tops = [tops[j] + rows[j:j + 1, :] for j in range(n)]
    top = tops[0]
    for t in tops[1:]:
        top = jnp.maximum(top, t)
    m_new = jnp.maximum(m, jnp.max(top, axis=0, keepdims=True))
    m_use = jnp.where(m_new < 0.5 * NEG_INF, 0.0, m_new)
    alpha = jnp.exp2(m - m_use)
    if rows is None:
        p = jnp.exp2(sc - m_use)
    else:
        p = jnp.concatenate([jnp.exp2(parts[j] + (rows[j:j + 1, :] - m_use)) for j in range(n)], axis=0)
    acc_ref[...] = alpha * acc_ref[...] + _dot(vt, p.astype(BF16))
    return m_new


def _run_steps(steps, pending, tail):
    pending = list(pending)
    fns = [s for s, _ in steps] + list(tail)
    for t, (_, consume) in enumerate(steps):
        s = pending.pop(0)
        if t + LOOKAHEAD < len(fns):
            pending.append(fns[t + LOOKAHEAD]())
        consume(s)
    return tuple(pending)


def _mask_bias(ok, reps):
    return jnp.concatenate([jnp.where(ok, 0.0, NEG_INF)] * reps, axis=1)


def _topk_mask(scores, rows_f, k):
    work = scores
    for _ in range(k):
        cm = jnp.max(work, axis=0, keepdims=True)
        idx = jnp.min(jnp.where(work == cm, rows_f, 1e9), axis=0, keepdims=True)
        work = jnp.where(rows_f == idx, REMOVED, work)
    return jnp.where(work == REMOVED, 1.0, 0.0)


def _add_block_bias(sc, rows, block):
    n = sc.shape[0] // block
    return jnp.concatenate([sc[block * j:block * (j + 1), :] + rows[j:j + 1, :] for j in range(n)], axis=0)


def _nsa_kernel(qt_ref, gt_ref, kc_ref, vct_ref, ksel_ref, vselt_ref, kwin_ref, vwint_ref, out_ref,
                qt_s, pg_s, bias_s, acc_s, accw_s):
    qi = pl.program_id(1)
    q0 = qi * Q_TILE
    G = NSA_KV_GROUPS
    LG = NSA_HPG * Q_TILE
    n_cmp_rows = kc_ref.shape[0]
    n_sel = bias_s.shape[1]

    qt_s[...] = jnp.zeros_like(qt_s)
    for h in range(NSA_HEADS):
        g, hh = divmod(h, NSA_HPG)
        qt_s[g, HEAD_DIM * g:HEAD_DIM * (g + 1), Q_TILE * hh:Q_TILE * (hh + 1)] = (
            qt_ref[HEAD_DIM * h:HEAD_DIM * (h + 1), :])
    qts = [qt_s[g] for g in range(G)]
    tq1 = q0 + lax.broadcasted_iota(jnp.int32, (1, Q_TILE), 1)
    rows_of = lambda g: slice(HEAD_DIM * g, HEAD_DIM * (g + 1))

    def ones_under(vt):
        return jnp.concatenate([vt, jnp.ones((ONES_ROWS, vt.shape[1]), BF16)], axis=0)

    kc = kc_ref[...]
    n_idx = lax.broadcasted_iota(jnp.int32, (n_cmp_rows, 1), 0)
    cbias = _mask_bias(n_idx * CMP_STRIDE + (CMP_BLOCK - 1) <= tq1, NSA_HPG)
    o_cmp = []
    for g in range(G):
        s = jnp.concatenate([_dot(kc[r:r + KV_SUB, :], qts[g]) for r in range(0, n_cmp_rows, KV_SUB)], axis=0) + cbias
        p, l = _softmax_block(s)
        den = jnp.maximum(l, 1e-30)
        o_cmp.append(_dot(vct_ref[rows_of(g), :], p.astype(BF16)) / den)
        pn = p / den
        pg = pn[:, 0:Q_TILE]
        for h in range(1, NSA_HPG):
            pg = pg + pn[:, h * Q_TILE:(h + 1) * Q_TILE]
        pg_s[g, 0:8, :] = jnp.zeros((8, Q_TILE), F32)
        pg_s[g, 8:8 + n_cmp_rows, :] = pg

    W = WINDOW + Q_TILE
    wc0 = jnp.maximum(qi - WINDOW // Q_TILE, 0)
    wstart = wc0 * 128
    kposw = wstart + lax.broadcasted_iota(jnp.int32, (W, 1), 0)
    wbias = _mask_bias((kposw <= tq1) & (kposw > tq1 - WINDOW), NSA_HPG)
    vtw = jnp.concatenate([vwint_ref[wc0 + j] for j in range(W // 128)], axis=1)
    accw_s[...] = jnp.zeros_like(accw_s)
    mw = [jnp.full((1, LG), NEG_INF, F32)] * G
    win_steps = []
    for r0 in range(0, W, KV_SUB):
        r1 = min(r0 + KV_SUB, W)
        for g in range(G):
            def score(r0=r0, r1=r1, g=g):
                k = kwin_ref[pl.ds(pl.multiple_of(wstart + r0, 128), r1 - r0), :]
                return _dot(k, qts[g]) + wbias[r0:r1, :]

            def consume(s, r0=r0, r1=r1, g=g):
                mw[g] = _flash_update(s, None, r1 - r0, ones_under(vtw[rows_of(g), r0:r1]), mw[g], accw_s.at[g])

            win_steps.append((score, consume))
    _run_steps(win_steps, [fn() for fn, _ in win_steps[:LOOKAHEAD]], [])

    blk_i = lax.broadcasted_iota(jnp.int32, (n_sel, 1), 0)
    cur = tq1 // SEL_BLOCK
    forced = (blk_i == 0) | (blk_i == cur) | (blk_i == cur - 1)
    started = blk_i <= cur
    n_top = min(SEL_COUNT, n_sel)
    slc = []
    for g in range(G):
        def tap(w):
            return pg_s[g, pl.ds(8 + w, n_sel, stride=SEL_BLOCK // CMP_STRIDE), :]

        t = tap(-1) + 2.0 * tap(0)
        t = t + 2.0 * tap(1)
        t = t + 2.0 * tap(2)
        t = t + tap(3)
        slc.append(jnp.where(forced, NEG_INF, jnp.where(started, t, NEG_INF)))
    free = _topk_mask(jnp.concatenate(slc, axis=1), jnp.broadcast_to(blk_i.astype(F32), (n_sel, G * Q_TILE)),
                      n_top - N_FORCED)
    for g in range(G):
        bias = jnp.where(forced, 0.0, jnp.where(free[:, Q_TILE * g:Q_TILE * (g + 1)] > 0.5, 0.0, NEG_INF))
        bias_s[g] = jnp.concatenate([bias] * NSA_HPG, axis=1)

    bpc = KV_CHUNK // SEL_BLOCK
    bps = KV_SUB // SEL_BLOCK
    c_last = q0 // KV_CHUNK

    def chunk_steps(c, ms, causal):
        steps = []
        for sub in range(KV_CHUNK // KV_SUB):
            for g in range(G):
                def score(sub=sub, g=g):
                    base = pl.multiple_of(c * KV_CHUNK, KV_CHUNK) + sub * KV_SUB
                    s = _dot(ksel_ref[pl.ds(pl.multiple_of(base, KV_SUB), KV_SUB), :], qts[g])
                    return s if causal is None else s + causal[sub]

                def consume(s, sub=sub, g=g):
                    rows = bias_s[g, pl.ds(pl.multiple_of(c * bpc, bpc), bpc), :][bps * sub:bps * (sub + 1), :]
                    vt = vselt_ref[c, rows_of(g), KV_SUB * sub:KV_SUB * (sub + 1)]
                    ms[g] = _flash_update(s, rows, SEL_BLOCK, ones_under(vt), ms[g], acc_s.at[g])

                steps.append((score, consume))
        return steps

    acc_s[...] = jnp.zeros_like(acc_s)

    def body(c, carry):
        ms = list(carry[:G])
        nxt = [fn for fn, _ in chunk_steps(c + 1, ms, None)[:LOOKAHEAD]]
        pending = _run_steps(chunk_steps(c, ms, None), carry[G:], nxt)
        return tuple(ms) + pending

    first = [fn() for fn, _ in chunk_steps(0, None, None)[:LOOKAHEAD]]
    carry = lax.fori_loop(0, c_last, body, (jnp.full((1, LG), NEG_INF, F32),) * G + tuple(first))
    kpos = c_last * KV_CHUNK + lax.broadcasted_iota(jnp.int32, (KV_CHUNK, 1), 0)
    causal = _mask_bias(kpos <= tq1, NSA_HPG)
    diag = chunk_steps(c_last, list(carry[:G]), [causal[KV_SUB * sub:KV_SUB * (sub + 1), :]
                                               for sub in range(KV_CHUNK // KV_SUB)])
    _run_steps(diag, [fn() for fn, _ in diag[:LOOKAHEAD]], [])

    gates = jax.nn.sigmoid(gt_ref[0:3 * NSA_HEADS, :])
    outs = []
    for h in range(NSA_HEADS):
        g, hh = divmod(h, NSA_HPG)
        cols = slice(Q_TILE * hh, Q_TILE * (hh + 1))
        o_sel = acc_s[g, 0:HEAD_DIM, cols] / jnp.maximum(acc_s[g, HEAD_DIM:HEAD_DIM + 1, cols], 1e-30)
        o_win = accw_s[g, 0:HEAD_DIM, cols] / jnp.maximum(accw_s[g, HEAD_DIM:HEAD_DIM + 1, cols], 1e-30)
        o = gates[3 * h:3 * h + 1, :] * o_cmp[g][:, cols]
        o = o + gates[3 * h + 1:3 * h + 2, :] * o_sel
        o = o + gates[3 * h + 2:3 * h + 3, :] * o_win
        outs.append(o)
    out_ref[...] = jnp.concatenate(outs, axis=0).T.astype(BF16)


def _nsa_call(qtn, gt, kc, vct, ksel, vselt, kwin, vwint, B, S):
    R = S // CMP_STRIDE
    n_sel = S // SEL_BLOCK
    LG = NSA_HPG * Q_TILE
    return pl.pallas_call(
        _nsa_kernel,
        out_shape=jax.ShapeDtypeStruct((B * S, NSA_DIM), BF16),
        grid=(B, S // Q_TILE),
        in_specs=[
            pl.BlockSpec((None, NSA_DIM, Q_TILE), lambda b, q: (b, 0, q)),
            pl.BlockSpec((None, 128, Q_TILE), lambda b, q: (b, 0, q)),
            pl.BlockSpec((None, R, 128), lambda b, q: (b, 0, 0)),
            pl.BlockSpec((None, 128, R), lambda b, q: (b, 0, 0)),
            pl.BlockSpec((S, 128), lambda b, q: (b, 0)),
            pl.BlockSpec((None, S // KV_CHUNK, 128, KV_CHUNK), lambda b, q: (b, 0, 0, 0)),
            pl.BlockSpec((S, 128), lambda b, q: (b, 0)),
            pl.BlockSpec((None, S // 128, 128, 128), lambda b, q: (b, 0, 0, 0)),
        ],
        out_specs=pl.BlockSpec((Q_TILE, NSA_DIM), lambda b, q: (b * (S // Q_TILE) + q, 0)),
        scratch_shapes=[
            pltpu.VMEM((NSA_KV_GROUPS, 128, LG), BF16),
            pltpu.VMEM((NSA_KV_GROUPS, 8 + R, Q_TILE), F32),
            pltpu.VMEM((NSA_KV_GROUPS, n_sel, LG), F32),
            pltpu.VMEM((NSA_KV_GROUPS, HEAD_DIM + ONES_ROWS, LG), F32),
            pltpu.VMEM((NSA_KV_GROUPS, HEAD_DIM + ONES_ROWS, LG), F32),
        ],
        compiler_params=pltpu.CompilerParams(
            dimension_semantics=("arbitrary", "arbitrary"), vmem_limit_bytes=VMEM_LIMIT),
        name="nsa_attn",
    )(qtn, gt, kc, vct, ksel, vselt, kwin, vwint)


def _moba_kernel(qt_ref, kmean_ref, k_ref, vt_ref, out_ref, qt_s, bias_s, acc_s):
    qi = pl.program_id(1)
    q0 = qi * MOBA_Q_TILE
    L = MOBA_Q_TILE
    H = MOBA_HEADS
    n_blk = kmean_ref.shape[0]
    n_top = max(1, min(MOBA_TOPK, n_blk - 1))

    qt_s[...] = jnp.zeros_like(qt_s)
    for h in range(H):
        rows = slice(HEAD_DIM * h, HEAD_DIM * (h + 1))
        qt_s[h, rows, :] = qt_ref[rows, :]
    qts = [qt_s[h] for h in range(H)]
    tq = q0 + lax.broadcasted_iota(jnp.int32, (1, L), 1)

    own = tq // MOBA_BLOCK
    blk_i = lax.broadcasted_iota(jnp.int32, (n_blk, 1), 0)
    blk_f = jnp.broadcast_to(blk_i.astype(F32), (n_blk, L))
    kmean = kmean_ref[...].astype(BF16)
    for h in range(H):
        sg = jnp.where(blk_i < own, _dot(kmean, qts[h]), NEG_INF)
        sel = _topk_mask(sg, blk_f, n_top)
        allow = ((sel > 0.5) & (blk_i < own)) | (blk_i == own)
        bias = jnp.where(allow, 0.0, NEG_INF)
        for j in range(n_blk):
            bias_s[h, j] = jnp.broadcast_to(bias[j:j + 1, :], (8, L))

    bpc = KV_CHUNK // MOBA_BLOCK

    steps = [(j, h) for j in range(bpc) for h in range(H)]

    ones = jnp.ones((ONES_ROWS, MOBA_BLOCK), BF16)

    def scores(c, j, h, diagonal):
        base = pl.multiple_of(c * KV_CHUNK, KV_CHUNK) + j * MOBA_BLOCK
        k = k_ref[pl.ds(pl.multiple_of(base, MOBA_BLOCK), MOBA_BLOCK), :]
        s = _dot(k, qts[h])
        if diagonal:
            kpos = base + lax.broadcasted_iota(jnp.int32, (MOBA_BLOCK, 1), 0)
            s = jnp.where(kpos <= tq, s, NEG_INF)
        return s

    def chunk_steps(c, ms, pending, diagonal):
        ms, pending = list(ms), list(pending)
        for t, (j, h) in enumerate(steps):
            s = pending.pop(0)
            u = t + LOOKAHEAD
            if u < len(steps):
                pending.append(scores(c, *steps[u], diagonal))
            elif not diagonal:
                pending.append(scores(c + 1, *steps[u - len(steps)], False))
            vt = vt_ref[c, HEAD_DIM * h:HEAD_DIM * (h + 1), MOBA_BLOCK * j:MOBA_BLOCK * (j + 1)]
            ms[h] = _flash_update(s, bias_s[h, c * bpc + j], MOBA_BLOCK, jnp.concatenate([vt, ones], axis=0),
                                  ms[h], acc_s.at[h])
        return tuple(ms), tuple(pending)

    acc_s[...] = jnp.zeros_like(acc_s)
    c_last = q0 // KV_CHUNK
    init = (jnp.full((1, L), NEG_INF, F32),) * H

    def body(c, carry):
        ms, pending = chunk_steps(c, carry[:H], carry[H:], False)
        return ms + pending

    carry = lax.fori_loop(0, c_last, body, init + tuple(scores(0, *steps[i], False) for i in range(LOOKAHEAD)))
    chunk_steps(c_last, carry[:H], [scores(c_last, *steps[i], True) for i in range(LOOKAHEAD)], True)
    outs = [acc_s[h, 0:HEAD_DIM, :] / jnp.maximum(acc_s[h, HEAD_DIM:HEAD_DIM + 1, :], 1e-30) for h in range(H)]
    out_ref[...] = jnp.concatenate(outs, axis=0).T.astype(BF16)


def _moba_call(qtm, kmean, km, vtm, B, S):
    n_blk = S // MOBA_BLOCK
    L = MOBA_Q_TILE
    return pl.pallas_call(
        _moba_kernel,
        out_shape=jax.ShapeDtypeStruct((B * S, MOBA_DIM), BF16),
        grid=(B, S // L),
        in_specs=[
            pl.BlockSpec((None, MOBA_DIM, L), lambda b, q: (b, 0, q)),
            pl.BlockSpec((None, n_blk, MOBA_DIM), lambda b, q: (b, 0, 0)),
            pl.BlockSpec((S, MOBA_DIM), lambda b, q: (b, 0)),
            pl.BlockSpec((None, S // KV_CHUNK, MOBA_DIM, KV_CHUNK), lambda b, q: (b, 0, 0, 0)),
        ],
        out_specs=pl.BlockSpec((L, MOBA_DIM), lambda b, q: (b * (S // L) + q, 0)),
        scratch_shapes=[
            pltpu.VMEM((MOBA_HEADS, MOBA_DIM, L), BF16),
            pltpu.VMEM((MOBA_HEADS, n_blk, 8, L), F32),
            pltpu.VMEM((MOBA_HEADS, HEAD_DIM + ONES_ROWS, L), F32),
        ],
        compiler_params=pltpu.CompilerParams(
            dimension_semantics=("arbitrary", "arbitrary"), vmem_limit_bytes=VMEM_LIMIT),
        name="moba_attn",
    )(qtm, kmean, km, vtm)


def _merge_kernel(u_ref, halo_ref, b_ref, c_ref, gbr_ref, x_ref, pw_ref, ps_ref, wa_ref, wb_ref, wc_ref,
                  wo_ref, gam_ref, x1_ref, h2_ref, ext_s, *, tiles_per_seq):
    tm = u_ref.shape[0]
    i = pl.program_id(0)
    first = (i % tiles_per_seq) == 0
    u = u_ref[...]
    ext_s[0:POOL_MAXW, :] = jnp.where(first, 0.0, halo_ref[...])
    ext_s[POOL_MAXW:POOL_MAXW + tm, :] = u

    def tail_sum(col, k0, k1):
        tot = None
        for k in range(k0, k1):
            v = ext_s[pl.ds(POOL_MAXW - k, tm), 128 * col:128 * col + 128]
            tot = v if tot is None else tot + v
        return tot

    t_glob = (i % tiles_per_seq) * tm + lax.broadcasted_iota(jnp.int32, (tm, 1), 0)
    lane = lax.broadcasted_iota(jnp.int32, (1, 128), 1)
    low = lane < POOL_GROUP_DIM
    pooled = []
    for col in range(2):
        wa_, wb_ = POOL_WINDOWS[2 * col], POOL_WINDOWS[2 * col + 1]
        sa = tail_sum(col, 0, wa_)
        sb = sa + tail_sum(col, wa_, wb_)
        ca = jnp.minimum(t_glob + 1, wa_).astype(F32)
        cb = jnp.minimum(t_glob + 1, wb_).astype(F32)
        pooled.append(jnp.where(low, sa, sb) / jnp.where(low, ca, cb))
    d = jnp.concatenate(pooled, axis=1) - u
    a = _dot(d.astype(BF16), pw_ref[...]) * ps_ref[...]

    av = _dot(a.astype(BF16), wa_ref[...])
    bv = _dot(b_ref[...], wb_ref[...])
    cv = _dot(c_ref[...], wc_ref[...])
    merged = jax.nn.sigmoid(gbr_ref[:, 0:D_MODEL]) * av
    merged = merged + jax.nn.sigmoid(gbr_ref[:, D_MODEL:2 * D_MODEL]) * bv
    merged = merged + jax.nn.sigmoid(gbr_ref[:, 2 * D_MODEL:3 * D_MODEL]) * cv
    x1 = x_ref[...] + _dot(merged.astype(BF16), wo_ref[...])
    x1_ref[...] = x1
    h2_ref[...] = _rms(x1, gam_ref[...]).astype(BF16)


def _merge_call(upool, bn, cm, gbr, x2, pw, ps, wa, wb, wc, wo, gamma, B, S):
    N = B * S
    tm = ROW_TILE
    nt = S // tm
    hb = tm // POOL_MAXW
    row = lambda w_: pl.BlockSpec((tm, w_), lambda i: (i, 0))
    const = lambda shape: pl.BlockSpec(shape, lambda i: (0,) * len(shape))
    return pl.pallas_call(
        functools.partial(_merge_kernel, tiles_per_seq=nt),
        out_shape=(jax.ShapeDtypeStruct((N, D_MODEL), F32), jax.ShapeDtypeStruct((N, D_MODEL), BF16)),
        grid=(N // tm,),
        in_specs=[
            row(POOL_DIM),
            pl.BlockSpec((POOL_MAXW, POOL_DIM), lambda i: (jnp.maximum(i * hb - 1, 0), 0)),
            row(NSA_DIM),
            row(MOBA_DIM),
            row(3 * D_MODEL),
            row(D_MODEL),
            const((POOL_DIM, POOL_DIM)),
            const((1, POOL_DIM)),
            const((POOL_DIM, D_MODEL)),
            const((NSA_DIM, D_MODEL)),
            const((MOBA_DIM, D_MODEL)),
            const((D_MODEL, D_MODEL)),
            const((1, D_MODEL)),
        ],
        out_specs=(row(D_MODEL), row(D_MODEL)),
        scratch_shapes=[pltpu.VMEM((POOL_MAXW + tm, POOL_DIM), F32)],
        compiler_params=pltpu.CompilerParams(
            dimension_semantics=("arbitrary",), vmem_limit_bytes=VMEM_LIMIT),
        name="merge",
    )(upool, upool, bn, cm, gbr, x2, pw, ps, wa, wb, wc, wo, gamma)


def _ffn_kernel(h_ref, x_ref, wg_ref, wu_ref, wd_ref, gam_ref, out_ref, acc_s, *, final_norm):
    f = pl.program_id(1)

    @pl.when(f == 0)
    def _():
        acc_s[...] = jnp.zeros_like(acc_s)

    h = h_ref[...]
    g = _dot(h, wg_ref[...])
    u = _dot(h, wu_ref[...])
    act = (g * jax.nn.sigmoid(g)) * u
    acc_s[...] += _dot(act.astype(BF16), wd_ref[...])

    @pl.when(f == pl.num_programs(1) - 1)
    def _():
        y = x_ref[...] + acc_s[...]
        if final_norm:
            y = _rms(y, gam_ref[...])
        out_ref[...] = y


def _ffn_call(h2, x1, wg, wu, wd, gamma, final_norm):
    N = x1.shape[0]
    F = wg.shape[1]
    tm, tf = FFN_ROW_TILE, FFN_COL_TILE
    return pl.pallas_call(
        functools.partial(_ffn_kernel, final_norm=final_norm),
        out_shape=jax.ShapeDtypeStruct((N, D_MODEL), F32),
        grid=(N // tm, F // tf),
        in_specs=[
            pl.BlockSpec((tm, D_MODEL), lambda i, f: (i, 0)),
            pl.BlockSpec((tm, D_MODEL), lambda i, f: (i, 0)),
            pl.BlockSpec((D_MODEL, tf), lambda i, f: (0, f)),
            pl.BlockSpec((D_MODEL, tf), lambda i, f: (0, f)),
            pl.BlockSpec((tf, D_MODEL), lambda i, f: (f, 0)),
            pl.BlockSpec((1, D_MODEL), lambda i, f: (0, 0)),
        ],
        out_specs=pl.BlockSpec((tm, D_MODEL), lambda i, f: (i, 0)),
        scratch_shapes=[pltpu.VMEM((tm, D_MODEL), F32)],
        compiler_params=pltpu.CompilerParams(
            dimension_semantics=("arbitrary", "arbitrary"), vmem_limit_bytes=VMEM_LIMIT),
        name="ffn",
    )(h2, x1, wg, wu, wd, gamma)


def _reorder_in_proj(w, b):
    o_qn = POOL_DIM
    o_kv = o_qn + NSA_DIM
    o_gn = o_kv + 6 * 128
    o_mo = o_gn + 3 * NSA_HEADS
    o_gb = o_mo + 3 * MOBA_DIM
    pad = 128 - 3 * NSA_HEADS

    def pick(a):
        parts = [a[..., 0:o_qn], a[..., o_kv:o_kv + 256], a[..., o_qn:o_kv], a[..., o_kv + 256:o_gn],
                 a[..., o_gn:o_mo], jnp.zeros(a.shape[:-1] + (pad,), a.dtype), a[..., o_mo:o_gb], a[..., o_gb:]]
        return jnp.concatenate(parts, axis=-1)

    return pick(w), pick(b)


def _rope_tables(S):
    pos = jnp.arange(S, dtype=F32)
    inv_freq = ROPE_THETA ** (-jnp.arange(0, HEAD_DIM, 2, dtype=F32) / HEAD_DIM)
    ang = pos[:, None] * inv_freq[None, :]
    cos, sin = jnp.cos(ang), jnp.sin(ang)
    cos_t = jnp.tile(cos, (1, 4))
    sin_t = jnp.tile(jnp.concatenate([-sin, sin], axis=1), (1, 2))
    return cos_t, sin_t


def kernel(x, attn_norm, w_in, b_in, pool_w, pool_scale, cmp_pos, cmp_w1, cmp_b1, cmp_w2, cmp_b2,
           w_br_pool, w_br_nsa, w_br_moba, w_out, ffn_norm, w_gate, w_up, w_down, final_norm):
    B, S, D = x.shape
    depth = w_in.shape[0]
    assert D == D_MODEL and S % KV_CHUNK == 0 and S % ROW_TILE == 0 and S >= WINDOW + Q_TILE
    assert (B * S) % FFN_ROW_TILE == 0 and w_gate.shape[2] % FFN_COL_TILE == 0
    N = B * S
    R = S // CMP_STRIDE
    cos_t, sin_t = _rope_tables(S)
    x2 = x.reshape(N, D)
    for l in range(depth):
        w_all, b_all = _reorder_in_proj(w_in[l], b_in[l])
        (upool, cmpin, qtn, ksel, vselt, kwin, vwint, gt, qtm, km, vtm, kmean, gbr) = _proj_call(
            x2, attn_norm[l][None, :], w_all.astype(BF16), b_all[None, :], cos_t, sin_t, B, S)

        kr = cmpin.reshape(B, S, 4, HEAD_DIM).transpose(0, 2, 1, 3).reshape(B, 4, R, CMP_STRIDE * HEAD_DIM)
        kc, vct = _compress_call(
            kr, cmp_pos[l].reshape(2, 1, CMP_BLOCK * HEAD_DIM), cmp_w1[l].astype(BF16),
            cmp_b1[l][:, None, :], cmp_w2[l].astype(BF16), cmp_b2[l][:, None, :], B, R)

        bn = _nsa_call(qtn, gt, kc, vct, ksel, vselt, kwin, vwint, B, S)
        cm = _moba_call(qtm, kmean.reshape(B, S // MOBA_BLOCK, MOBA_DIM), km, vtm, B, S)

        pw_bd = jax.scipy.linalg.block_diag(*[pool_w[l, g] for g in range(len(POOL_WINDOWS))])
        x1, h2 = _merge_call(
            upool, bn, cm, gbr, x2, pw_bd.astype(BF16), pool_scale[l][None, :],
            w_br_pool[l].astype(BF16), w_br_nsa[l].astype(BF16), w_br_moba[l].astype(BF16),
            w_out[l].astype(BF16), ffn_norm[l][None, :], B, S)

        x2 = _ffn_call(h2, x1, w_gate[l].astype(BF16), w_up[l].astype(BF16), w_down[l].astype(BF16),
                       final_norm[None, :], final_norm=(l == depth - 1))
    return x2.reshape(B, S, D)
```

```python
import functools

import numpy as np
import jax
import jax.numpy as jnp
from jax import lax
from jax.experimental import pallas as pl
from jax.experimental.pallas import tpu as pltpu

F32 = jnp.float32
BF16 = jnp.bfloat16

D_MODEL = 1024
HEAD_DIM = 64
ROPE_THETA = 10000.0
RMS_EPS = 1e-6
NEG_INF = -1e30
FORCE = 1e30
LOG2E = 1.4426950408889634
LOOKAHEAD = 2
TRIP_UNROLL = 4
ONES_ROWS = 16
REMOVED = -(2.0 ** 127)

POOL_WINDOWS = (2, 4, 8, 16)
POOL_GROUP_DIM = 64
POOL_DIM = 256
POOL_MAXW = 16

NSA_HEADS = 8
NSA_KV_GROUPS = 2
NSA_HPG = 4
NSA_DIM = 512
CMP_BLOCK = 32
CMP_STRIDE = 16
CMP_HIDDEN = 256
SEL_BLOCK = 64
SEL_COUNT = 16
N_FORCED = 3
WINDOW = 512

MOBA_HEADS = 4
MOBA_DIM = 256
MOBA_BLOCK = 256
MOBA_TOPK = 3

Q_TILE = 128
MOBA_Q_TILE = 512
KV_CHUNK = 512
KV_SUB = 256
ROW_TILE = 512
FFN_ROW_TILE = 1024
FFN_COL_TILE = 1408
VMEM_LIMIT = 56 * 1024 * 1024

C_POOL = 0
C_CMP = 256
C_QN = 512
C_KS = 1024
C_VS = 1152
C_KW = 1280
C_VW = 1408
C_GN = 1536
C_QM = 1664
C_KM = 1920
C_VM = 2176
C_GBR = 2432
C_TOTAL = 5504


def _dot(a, b):
    return jnp.dot(a, b, preferred_element_type=F32)


def _rms(x, gamma):
    return x * lax.rsqrt(jnp.mean(x * x, axis=-1, keepdims=True) + RMS_EPS) * gamma


def _proj_kernel(x_ref, gam_ref, w_ref, b_ref, cos_ref, sin_ref,
                 upool_ref, cmpk_ref, cmpv_ref, qtn_ref, ksel_ref, vselt_ref, kwin_ref, vwint_ref, gt_ref,
                 qtm_ref, km_ref, vtm_ref, kmean_ref, gbr_ref):
    tm = x_ref.shape[0]
    h = _rms(x_ref[...], gam_ref[...]).astype(BF16)
    cos = cos_ref[...]
    sin = sin_ref[...]
    lane = lax.broadcasted_iota(jnp.int32, (tm, 128), 1)
    first_half = (lane & 32) == 0
    scale = HEAD_DIM ** -0.5 * LOG2E

    def seg(a, b):
        return _dot(h, w_ref[:, a:b]) + b_ref[:, a:b]

    def rope(y):
        swap = jnp.where(first_half, pltpu.roll(y, 96, 1), pltpu.roll(y, 32, 1))
        return y * cos + swap * sin

    y = seg(C_POOL, C_POOL + 512)
    upool_ref[...] = y[:, :256]
    cmpk_ref[...] = rope(y[:, 256:384])
    cmpv_ref[...] = y[:, 384:512]

    y = seg(C_QN, C_QN + 512)
    for j in range(4):
        q = rope(y[:, 128 * j:128 * j + 128]) * scale
        qtn_ref[128 * j:128 * j + 128, :] = q.T.astype(BF16)

    y = seg(C_KS, C_KS + 512)
    ksel_ref[...] = rope(y[:, 0:128]).astype(BF16)
    vselt_ref[0] = y[:, 128:256].T.astype(BF16)
    kwin_ref[...] = rope(y[:, 256:384]).astype(BF16)
    vwt = y[:, 384:512].T.astype(BF16)
    for j in range(tm // 128):
        vwint_ref[j] = vwt[:, 128 * j:128 * j + 128]

    y = seg(C_GN, C_GN + 128)
    gt_ref[...] = y.T

    y = seg(C_QM, C_QM + 256)
    for j in range(2):
        q = rope(y[:, 128 * j:128 * j + 128]) * scale
        qtm_ref[128 * j:128 * j + 128, :] = q.T.astype(BF16)

    y = seg(C_KM, C_KM + 256)
    km = jnp.concatenate([rope(y[:, 0:128]), rope(y[:, 128:256])], axis=1)
    km_ref[...] = km.astype(BF16)
    nblk = tm // MOBA_BLOCK
    means = [jnp.sum(km[MOBA_BLOCK * j:MOBA_BLOCK * (j + 1), :], axis=0, keepdims=True) * (1.0 / MOBA_BLOCK)
             for j in range(nblk)]
    kmean_ref[...] = jnp.concatenate(means, axis=0)

    y = seg(C_VM, C_VM + 256)
    vtm_ref[0] = y.T.astype(BF16)

    for j in range(6):
        a = C_GBR + 512 * j
        gbr_ref[:, 512 * j:512 * j + 512] = jax.nn.sigmoid(seg(a, a + 512)).astype(BF16)


def _proj_call(x2, gamma, w, bias, cos_t, sin_t, B, S):
    N = B * S
    tm = ROW_TILE
    nt = S // tm
    f = lambda shape, dt: jax.ShapeDtypeStruct(shape, dt)
    out_shape = (
        f((N, 256), F32),
        f((N, 128), F32),
        f((N, 128), F32),
        f((B, NSA_DIM, S), BF16),
        f((N, 128), BF16),
        f((B, S // KV_CHUNK, 128, KV_CHUNK), BF16),
        f((N, 128), BF16),
        f((B, S // 128, 128, 128), BF16),
        f((B, 128, S), F32),
        f((B, MOBA_DIM, S), BF16),
        f((N, MOBA_DIM), BF16),
        f((B, S // KV_CHUNK, MOBA_DIM, KV_CHUNK), BF16),
        f((B, nt, tm // MOBA_BLOCK, MOBA_DIM), F32),
        f((N, 3 * D_MODEL), BF16),
    )
    row = lambda w_: pl.BlockSpec((tm, w_), lambda i: (i, 0))
    const = lambda shape: pl.BlockSpec(shape, lambda i: (0,) * len(shape))
    in_specs = [
        row(D_MODEL),
        const((1, D_MODEL)),
        const((D_MODEL, C_TOTAL)),
        const((1, C_TOTAL)),
        pl.BlockSpec((tm, 128), lambda i: (i % nt, 0)),
        pl.BlockSpec((tm, 128), lambda i: (i % nt, 0)),
    ]
    out_specs = (
        row(256),
        row(128),
        row(128),
        pl.BlockSpec((None, NSA_DIM, tm), lambda i: (i // nt, 0, i % nt)),
        row(128),
        pl.BlockSpec((None, tm // KV_CHUNK, 128, KV_CHUNK), lambda i: (i // nt, i % nt, 0, 0)),
        row(128),
        pl.BlockSpec((None, tm // 128, 128, 128), lambda i: (i // nt, i % nt, 0, 0)),
        pl.BlockSpec((None, 128, tm), lambda i: (i // nt, 0, i % nt)),
        pl.BlockSpec((None, MOBA_DIM, tm), lambda i: (i // nt, 0, i % nt)),
        row(MOBA_DIM),
        pl.BlockSpec((None, tm // KV_CHUNK, MOBA_DIM, KV_CHUNK), lambda i: (i // nt, i % nt, 0, 0)),
        pl.BlockSpec((None, None, tm // MOBA_BLOCK, MOBA_DIM), lambda i: (i // nt, i % nt, 0, 0)),
        row(3 * D_MODEL),
    )
    return pl.pallas_call(
        _proj_kernel,
        out_shape=out_shape,
        grid=(N // tm,),
        in_specs=in_specs,
        out_specs=out_specs,
        compiler_params=pltpu.CompilerParams(
            dimension_semantics=("arbitrary",), vmem_limit_bytes=VMEM_LIMIT),
        name="proj",
    )(x2, gamma, w, bias, cos_t, sin_t)


def _gelu_tanh(x):
    return x * (0.5 * (1.0 + jnp.tanh(np.sqrt(2.0 / np.pi).astype(np.float32) * (x + 0.044715 * (x * x * x)))))


def _compress_kernel(xk_ref, xv_ref, pos_ref, w1_ref, b1_ref, w2_ref, b2_ref, kc_ref, vct_ref):
    R = xk_ref.shape[0] // CMP_STRIDE
    top = [None] * 4
    bot = [None] * 4
    for l in range(CMP_STRIDE):
        xl = [r[pl.ds(l, R, stride=CMP_STRIDE), :] for r in (xk_ref, xv_ref)]
        for j in range(4):
            t, g = divmod(j, NSA_KV_GROUPS)
            piece = xl[t][:, HEAD_DIM * g:HEAD_DIM * (g + 1)]
            lo = (piece + pos_ref[t, l:l + 1, :]).astype(BF16)
            hi = (piece + pos_ref[t, CMP_STRIDE + l:CMP_STRIDE + l + 1, :]).astype(BF16)
            a = _dot(lo, w1_ref[t, HEAD_DIM * l:HEAD_DIM * (l + 1), :])
            b = _dot(hi, w1_ref[t, HEAD_DIM * (CMP_STRIDE + l):HEAD_DIM * (CMP_STRIDE + l + 1), :])
            top[j] = a if top[j] is None else top[j] + a
            bot[j] = b if bot[j] is None else bot[j] + b
    outs = []
    for j in range(4):
        t = j // NSA_KV_GROUPS
        hid = _gelu_tanh(top[j] + pltpu.roll(bot[j], R - 1, 0) + b1_ref[t])
        outs.append(_dot(hid.astype(BF16), w2_ref[t]) + b2_ref[t])
    kc_ref[...] = jnp.concatenate(outs[0:2], axis=1).astype(BF16)
    vct_ref[...] = jnp.concatenate(outs[2:4], axis=1).T.astype(BF16)


def _compress_call(xk, xv, pos, w1, b1, w2, b2, B, R):
    S = R * CMP_STRIDE
    return pl.pallas_call(
        _compress_kernel,
        out_shape=(jax.ShapeDtypeStruct((B, R, 128), BF16), jax.ShapeDtypeStruct((B, 128, R), BF16)),
        grid=(B,),
        in_specs=[
            pl.BlockSpec((S, 128), lambda b: (b, 0)),
            pl.BlockSpec((S, 128), lambda b: (b, 0)),
            pl.BlockSpec((2, CMP_BLOCK, HEAD_DIM), lambda b: (0, 0, 0)),
            pl.BlockSpec((2, CMP_BLOCK * HEAD_DIM, CMP_HIDDEN), lambda b: (0, 0, 0)),
            pl.BlockSpec((2, 1, CMP_HIDDEN), lambda b: (0, 0, 0)),
            pl.BlockSpec((2, CMP_HIDDEN, HEAD_DIM), lambda b: (0, 0, 0)),
            pl.BlockSpec((2, 1, HEAD_DIM), lambda b: (0, 0, 0)),
        ],
        out_specs=(pl.BlockSpec((None, R, 128), lambda b: (b, 0, 0)),
                   pl.BlockSpec((None, 128, R), lambda b: (b, 0, 0))),
        compiler_params=pltpu.CompilerParams(
            dimension_semantics=("arbitrary",), vmem_limit_bytes=VMEM_LIMIT),
        name="nsa_compress",
    )(xk, xv, pos, w1, b1, w2, b2)


def _softmax_block(s):
    m = jnp.max(s, axis=0, keepdims=True)
    m_use = jnp.where(m < 0.5 * NEG_INF, 0.0, m)
    p = jnp.exp2(s - m_use)
    return p, jnp.sum(p, axis=0, keepdims=True)


def _flash_update(sc, rows, block, vt, m, acc_ref):
    n, L = sc.shape[0] // block, sc.shape[1]
    parts = [sc[block * j:block * (j + 1), :] for j in range(n)]
    tops = [jnp.max(parts[j].reshape(block // 8, 8, L), axis=0) for j in range(n)]
    if rows is not None:
        tops = [tops[j] + rows[j:j + 1, :] for j in range(n)]
    top = tops[0]
    for t in tops[1:]:
        top = jnp.maximum(top, t)
    m_new = jnp.maximum(m, jnp.max(top, axis=0, keepdims=True))
    m_use = jnp.where(m_new < 0.5 * NEG_INF, 0.0, m_new)
    alpha = jnp.exp2(m - m_use)
    if rows is None:
        p = jnp.exp2(sc - m_use)
    else:
        p = jnp.concatenate([jnp.exp2(parts[j] + (rows[j:j + 1, :] - m_use)) for j in range(n)], axis=0)
    acc_ref[...] = alpha * acc_ref[...] + _dot(vt, p.astype(BF16))
    return m_new


def _run_steps(steps, pending, tail):
    pending = list(pending)
    fns = [s for s, _ in steps] + list(tail)
    for t, (_, consume) in enumerate(steps):
        s = pending.pop(0)
        if t + LOOKAHEAD < len(fns):
            pending.append(fns[t + LOOKAHEAD]())
        consume(s)
    return tuple(pending)


def _chunk_loop(body, n, carry):
    def multi(i, cr):
        for j in range(TRIP_UNROLL):
            cr = body(TRIP_UNROLL * i + j, cr)
        return cr

    carry = lax.fori_loop(0, n // TRIP_UNROLL, multi, carry)
    return lax.fori_loop(TRIP_UNROLL * (n // TRIP_UNROLL), n, body, carry)


def _mask_bias(ok, reps):
    return jnp.concatenate([jnp.where(ok, 0.0, NEG_INF)] * reps, axis=1)


def _topk_mask(scores, rows_f, k):
    work = scores
    for _ in range(k):
        cm = jnp.max(work, axis=0, keepdims=True)
        idx = jnp.min(jnp.where(work == cm, rows_f, 1e9), axis=0, keepdims=True)
        work = jnp.where(rows_f == idx, REMOVED, work)
    return jnp.where(work == REMOVED, 1.0, 0.0)


def _add_block_bias(sc, rows, block):
    n = sc.shape[0] // block
    return jnp.concatenate([sc[block * j:block * (j + 1), :] + rows[j:j + 1, :] for j in range(n)], axis=0)


def _nsa_kernel(qt_ref, gt_ref, kc_ref, vct_ref, ksel_ref, vselt_ref, kwin_ref, vwint_ref, out_ref,
                qt_s, pg_s, bias_s, acc_s, accw_s):
    qi = pl.program_id(1)
    q0 = qi * Q_TILE
    G = NSA_KV_GROUPS
    LG = NSA_HPG * Q_TILE
    n_cmp_rows = kc_ref.shape[0]
    n_sel = bias_s.shape[1]

    qt_s[...] = jnp.zeros_like(qt_s)
    for h in range(NSA_HEADS):
        g, hh = divmod(h, NSA_HPG)
        qt_s[g, HEAD_DIM * g:HEAD_DIM * (g + 1), Q_TILE * hh:Q_TILE * (hh + 1)] = (
            qt_ref[HEAD_DIM * h:HEAD_DIM * (h + 1), :])
    qts = [qt_s[g] for g in range(G)]
    tq1 = q0 + lax.broadcasted_iota(jnp.int32, (1, Q_TILE), 1)
    rows_of = lambda g: slice(HEAD_DIM * g, HEAD_DIM * (g + 1))

    def ones_under(vt):
        return jnp.concatenate([vt, jnp.ones((ONES_ROWS, vt.shape[1]), BF16)], axis=0)

    kc = kc_ref[...]
    n_idx = lax.broadcasted_iota(jnp.int32, (n_cmp_rows, 1), 0)
    cbias = _mask_bias(n_idx * CMP_STRIDE + (CMP_BLOCK - 1) <= tq1, NSA_HPG)
    o_cmp = []
    for g in range(G):
        s = jnp.concatenate([_dot(kc[r:r + KV_SUB, :], qts[g]) for r in range(0, n_cmp_rows, KV_SUB)], axis=0) + cbias
        p, l = _softmax_block(s)
        den = jnp.maximum(l, 1e-30)
        o_cmp.append(_dot(vct_ref[rows_of(g), :], p.astype(BF16)) / den)
        pn = p / den
        pg = pn[:, 0:Q_TILE]
        for h in range(1, NSA_HPG):
            pg = pg + pn[:, h * Q_TILE:(h + 1) * Q_TILE]
        pg_s[g, 0:8, :] = jnp.zeros((8, Q_TILE), F32)
        pg_s[g, 8:8 + n_cmp_rows, :] = pg

    W = WINDOW + Q_TILE
    wc0 = jnp.maximum(qi - WINDOW // Q_TILE, 0)
    wstart = wc0 * 128
    kposw = wstart + lax.broadcasted_iota(jnp.int32, (W, 1), 0)
    wbias = _mask_bias((kposw <= tq1) & (kposw > tq1 - WINDOW), NSA_HPG)
    vtw = jnp.concatenate([vwint_ref[wc0 + j] for j in range(W // 128)], axis=1)
    accw_s[...] = jnp.zeros_like(accw_s)
    mw = [jnp.full((1, LG), NEG_INF, F32)] * G
    win_steps = []
    for r0 in range(0, W, KV_SUB):
        r1 = min(r0 + KV_SUB, W)
        for g in range(G):
            def score(r0=r0, r1=r1, g=g):
                k = kwin_ref[pl.ds(pl.multiple_of(wstart + r0, 128), r1 - r0), :]
                return _dot(k, qts[g]) + wbias[r0:r1, :]

            def consume(s, r0=r0, r1=r1, g=g):
                mw[g] = _flash_update(s, None, r1 - r0, ones_under(vtw[rows_of(g), r0:r1]), mw[g], accw_s.at[g])

            win_steps.append((score, consume))
    _run_steps(win_steps, [fn() for fn, _ in win_steps[:LOOKAHEAD]], [])

    blk_i = lax.broadcasted_iota(jnp.int32, (n_sel, 1), 0)
    cur = tq1 // SEL_BLOCK
    forced = (blk_i == 0) | (blk_i == cur) | (blk_i == cur - 1)
    started = blk_i <= cur
    n_top = min(SEL_COUNT, n_sel)
    slc = []
    for g in range(G):
        def tap(w):
            return pg_s[g, pl.ds(8 + w, n_sel, stride=SEL_BLOCK // CMP_STRIDE), :]

        t = tap(-1) + 2.0 * tap(0)
        t = t + 2.0 * tap(1)
        t = t + 2.0 * tap(2)
        t = t + tap(3)
        slc.append(jnp.where(forced, NEG_INF, jnp.where(started, t, NEG_INF)))
    free = _topk_mask(jnp.concatenate(slc, axis=1), jnp.broadcast_to(blk_i.astype(F32), (n_sel, G * Q_TILE)),
                      n_top - N_FORCED)
    for g in range(G):
        bias = jnp.where(forced, 0.0, jnp.where(free[:, Q_TILE * g:Q_TILE * (g + 1)] > 0.5, 0.0, NEG_INF))
        bias_s[g] = jnp.concatenate([bias] * NSA_HPG, axis=1)

    bpc = KV_CHUNK // SEL_BLOCK
    bps = KV_SUB // SEL_BLOCK
    c_last = q0 // KV_CHUNK

    def chunk_steps(c, ms, causal):
        steps = []
        for sub in range(KV_CHUNK // KV_SUB):
            for g in range(G):
                def score(sub=sub, g=g):
                    base = pl.multiple_of(c * KV_CHUNK, KV_CHUNK) + sub * KV_SUB
                    s = _dot(ksel_ref[pl.ds(pl.multiple_of(base, KV_SUB), KV_SUB), :], qts[g])
                    return s if causal is None else s + causal[sub]

                def consume(s, sub=sub, g=g):
                    rows = bias_s[g, pl.ds(pl.multiple_of(c * bpc, bpc), bpc), :][bps * sub:bps * (sub + 1), :]
                    vt = vselt_ref[c, rows_of(g), KV_SUB * sub:KV_SUB * (sub + 1)]
                    ms[g] = _flash_update(s, rows, SEL_BLOCK, ones_under(vt), ms[g], acc_s.at[g])

                steps.append((score, consume))
        return steps

    acc_s[...] = jnp.zeros_like(acc_s)

    def body(c, carry):
        ms = list(carry[:G])
        nxt = [fn for fn, _ in chunk_steps(c + 1, ms, None)[:LOOKAHEAD]]
        pending = _run_steps(chunk_steps(c, ms, None), carry[G:], nxt)
        return tuple(ms) + pending

    first = [fn() for fn, _ in chunk_steps(0, None, None)[:LOOKAHEAD]]
    carry = (jnp.full((1, LG), NEG_INF, F32),) * G + tuple(first)
    carry = _chunk_loop(body, c_last, carry)
    kpos = c_last * KV_CHUNK + lax.broadcasted_iota(jnp.int32, (KV_CHUNK, 1), 0)
    causal = _mask_bias(kpos <= tq1, NSA_HPG)
    diag = chunk_steps(c_last, list(carry[:G]), [causal[KV_SUB * sub:KV_SUB * (sub + 1), :]
                                               for sub in range(KV_CHUNK // KV_SUB)])
    _run_steps(diag, [fn() for fn, _ in diag[:LOOKAHEAD]], [])

    gates = jax.nn.sigmoid(gt_ref[0:3 * NSA_HEADS, :])
    outs = []
    for h in range(NSA_HEADS):
        g, hh = divmod(h, NSA_HPG)
        cols = slice(Q_TILE * hh, Q_TILE * (hh + 1))
        o_sel = acc_s[g, 0:HEAD_DIM, cols] / jnp.maximum(acc_s[g, HEAD_DIM:HEAD_DIM + 1, cols], 1e-30)
        o_win = accw_s[g, 0:HEAD_DIM, cols] / jnp.maximum(accw_s[g, HEAD_DIM:HEAD_DIM + 1, cols], 1e-30)
        o = gates[3 * h:3 * h + 1, :] * o_cmp[g][:, cols]
        o = o + gates[3 * h + 1:3 * h + 2, :] * o_sel
        o = o + gates[3 * h + 2:3 * h + 3, :] * o_win
        outs.append(o)
    out_ref[...] = jnp.concatenate(outs, axis=0).T.astype(BF16)


def _nsa_call(qtn, gt, kc, vct, ksel, vselt, kwin, vwint, B, S):
    R = S // CMP_STRIDE
    n_sel = S // SEL_BLOCK
    LG = NSA_HPG * Q_TILE
    return pl.pallas_call(
        _nsa_kernel,
        out_shape=jax.ShapeDtypeStruct((B * S, NSA_DIM), BF16),
        grid=(B, S // Q_TILE),
        in_specs=[
            pl.BlockSpec((None, NSA_DIM, Q_TILE), lambda b, q: (b, 0, q)),
            pl.BlockSpec((None, 128, Q_TILE), lambda b, q: (b, 0, q)),
            pl.BlockSpec((None, R, 128), lambda b, q: (b, 0, 0)),
            pl.BlockSpec((None, 128, R), lambda b, q: (b, 0, 0)),
            pl.BlockSpec((S, 128), lambda b, q: (b, 0)),
            pl.BlockSpec((None, S // KV_CHUNK, 128, KV_CHUNK), lambda b, q: (b, 0, 0, 0)),
            pl.BlockSpec((S, 128), lambda b, q: (b, 0)),
            pl.BlockSpec((None, S // 128, 128, 128), lambda b, q: (b, 0, 0, 0)),
        ],
        out_specs=pl.BlockSpec((Q_TILE, NSA_DIM), lambda b, q: (b * (S // Q_TILE) + q, 0)),
        scratch_shapes=[
            pltpu.VMEM((NSA_KV_GROUPS, 128, LG), BF16),
            pltpu.VMEM((NSA_KV_GROUPS, 8 + R, Q_TILE), F32),
            pltpu.VMEM((NSA_KV_GROUPS, n_sel, LG), F32),
            pltpu.VMEM((NSA_KV_GROUPS, HEAD_DIM + ONES_ROWS, LG), F32),
            pltpu.VMEM((NSA_KV_GROUPS, HEAD_DIM + ONES_ROWS, LG), F32),
        ],
        compiler_params=pltpu.CompilerParams(
            dimension_semantics=("arbitrary", "arbitrary"), vmem_limit_bytes=VMEM_LIMIT),
        name="nsa_attn",
    )(qtn, gt, kc, vct, ksel, vselt, kwin, vwint)


def _moba_kernel(qt_ref, kmean_ref, k_ref, vt_ref, out_ref, qt_s, bias_s, acc_s):
    qi = pl.program_id(1)
    q0 = qi * MOBA_Q_TILE
    L = MOBA_Q_TILE
    H = MOBA_HEADS
    n_blk = kmean_ref.shape[0]
    n_top = max(1, min(MOBA_TOPK, n_blk - 1))

    qt_s[...] = jnp.zeros_like(qt_s)
    for h in range(H):
        rows = slice(HEAD_DIM * h, HEAD_DIM * (h + 1))
        qt_s[h, rows, :] = qt_ref[rows, :]
    qts = [qt_s[h] for h in range(H)]
    tq = q0 + lax.broadcasted_iota(jnp.int32, (1, L), 1)

    own = tq // MOBA_BLOCK
    blk_i = lax.broadcasted_iota(jnp.int32, (n_blk, 1), 0)
    blk_f = jnp.broadcast_to(blk_i.astype(F32), (n_blk, L))
    kmean = kmean_ref[...].astype(BF16)
    for h in range(H):
        sg = jnp.where(blk_i < own, _dot(kmean, qts[h]), NEG_INF)
        sel = _topk_mask(sg, blk_f, n_top)
        allow = ((sel > 0.5) & (blk_i < own)) | (blk_i == own)
        bias = jnp.where(allow, 0.0, NEG_INF)
        for j in range(n_blk):
            bias_s[h, j] = jnp.broadcast_to(bias[j:j + 1, :], (8, L))

    bpc = KV_CHUNK // MOBA_BLOCK

    steps = [(j, h) for j in range(bpc) for h in range(H)]

    ones = jnp.ones((ONES_ROWS, MOBA_BLOCK), BF16)

    def scores(c, j, h, diagonal):
        base = pl.multiple_of(c * KV_CHUNK, KV_CHUNK) + j * MOBA_BLOCK
        k = k_ref[pl.ds(pl.multiple_of(base, MOBA_BLOCK), MOBA_BLOCK), :]
        s = _dot(k, qts[h])
        if diagonal:
            kpos = base + lax.broadcasted_iota(jnp.int32, (MOBA_BLOCK, 1), 0)
            s = jnp.where(kpos <= tq, s, NEG_INF)
        return s

    def chunk_steps(c, ms, pending, diagonal):
        ms, pending = list(ms), list(pending)
        for t, (j, h) in enumerate(steps):
            s = pending.pop(0)
            u = t + LOOKAHEAD
            if u < len(steps):
                pending.append(scores(c, *steps[u], diagonal))
            elif not diagonal:
                pending.append(scores(c + 1, *steps[u - len(steps)], False))
            vt = vt_ref[c, HEAD_DIM * h:HEAD_DIM * (h + 1), MOBA_BLOCK * j:MOBA_BLOCK * (j + 1)]
            ms[h] = _flash_update(s, bias_s[h, c * bpc + j], MOBA_BLOCK, jnp.concatenate([vt, ones], axis=0),
                                  ms[h], acc_s.at[h])
        return tuple(ms), tuple(pending)

    acc_s[...] = jnp.zeros_like(acc_s)
    c_last = q0 // KV_CHUNK
    init = (jnp.full((1, L), NEG_INF, F32),) * H

    def body(c, carry):
        ms, pending = chunk_steps(c, carry[:H], carry[H:], False)
        return ms + pending

    carry = init + tuple(scores(0, *steps[i], False) for i in range(LOOKAHEAD))
    carry = _chunk_loop(body, c_last, carry)
    chunk_steps(c_last, carry[:H], [scores(c_last, *steps[i], True) for i in range(LOOKAHEAD)], True)
    outs = [acc_s[h, 0:HEAD_DIM, :] / jnp.maximum(acc_s[h, HEAD_DIM:HEAD_DIM + 1, :], 1e-30) for h in range(H)]
    out_ref[...] = jnp.concatenate(outs, axis=0).T.astype(BF16)


def _moba_call(qtm, kmean, km, vtm, B, S):
    n_blk = S // MOBA_BLOCK
    L = MOBA_Q_TILE
    return pl.pallas_call(
        _moba_kernel,
        out_shape=jax.ShapeDtypeStruct((B * S, MOBA_DIM), BF16),
        grid=(B, S // L),
        in_specs=[
            pl.BlockSpec((None, MOBA_DIM, L), lambda b, q: (b, 0, q)),
            pl.BlockSpec((None, n_blk, MOBA_DIM), lambda b, q: (b, 0, 0)),
            pl.BlockSpec((S, MOBA_DIM), lambda b, q: (b, 0)),
            pl.BlockSpec((None, S // KV_CHUNK, MOBA_DIM, KV_CHUNK), lambda b, q: (b, 0, 0, 0)),
        ],
        out_specs=pl.BlockSpec((L, MOBA_DIM), lambda b, q: (b * (S // L) + q, 0)),
        scratch_shapes=[
            pltpu.VMEM((MOBA_HEADS, MOBA_DIM, L), BF16),
            pltpu.VMEM((MOBA_HEADS, n_blk, 8, L), F32),
            pltpu.VMEM((MOBA_HEADS, HEAD_DIM + ONES_ROWS, L), F32),
        ],
        compiler_params=pltpu.CompilerParams(
            dimension_semantics=("arbitrary", "arbitrary"), vmem_limit_bytes=VMEM_LIMIT),
        name="moba_attn",
    )(qtm, kmean, km, vtm)


def _merge_kernel(u_ref, halo_ref, b_ref, c_ref, gbr_ref, x_ref, pw_ref, ps_ref, wa_ref, wb_ref, wc_ref,
                  wo_ref, gam_ref, x1_ref, h2_ref, ext_s, *, tiles_per_seq):
    tm = u_ref.shape[0]
    i = pl.program_id(0)
    first = (i % tiles_per_seq) == 0
    u = u_ref[...]
    ext_s[0:POOL_MAXW, :] = jnp.where(first, 0.0, halo_ref[...])
    ext_s[POOL_MAXW:POOL_MAXW + tm, :] = u

    def tail_sum(col, k0, k1):
        tot = None
        for k in range(k0, k1):
            v = ext_s[pl.ds(POOL_MAXW - k, tm), 128 * col:128 * col + 128]
            tot = v if tot is None else tot + v
        return tot

    t_glob = (i % tiles_per_seq) * tm + lax.broadcasted_iota(jnp.int32, (tm, 1), 0)
    lane = lax.broadcasted_iota(jnp.int32, (1, 128), 1)
    low = lane < POOL_GROUP_DIM
    pooled = []
    for col in range(2):
        wa_, wb_ = POOL_WINDOWS[2 * col], POOL_WINDOWS[2 * col + 1]
        sa = tail_sum(col, 0, wa_)
        sb = sa + tail_sum(col, wa_, wb_)
        ca = jnp.minimum(t_glob + 1, wa_).astype(F32)
        cb = jnp.minimum(t_glob + 1, wb_).astype(F32)
        pooled.append(jnp.where(low, sa, sb) / jnp.where(low, ca, cb))
    d = jnp.concatenate(pooled, axis=1) - u
    a = _dot(d.astype(BF16), pw_ref[...]) * ps_ref[...]

    av = _dot(a.astype(BF16), wa_ref[...])
    bv = _dot(b_ref[...], wb_ref[...])
    cv = _dot(c_ref[...], wc_ref[...])
    merged = gbr_ref[:, 0:D_MODEL].astype(F32) * av
    merged = merged + gbr_ref[:, D_MODEL:2 * D_MODEL].astype(F32) * bv
    merged = merged + gbr_ref[:, 2 * D_MODEL:3 * D_MODEL].astype(F32) * cv
    x1 = x_ref[...] + _dot(merged.astype(BF16), wo_ref[...])
    x1_ref[...] = x1
    h2_ref[...] = _rms(x1, gam_ref[...]).astype(BF16)


def _merge_call(upool, bn, cm, gbr, x2, pw, ps, wa, wb, wc, wo, gamma, B, S):
    N = B * S
    tm = ROW_TILE
    nt = S // tm
    hb = tm // POOL_MAXW
    row = lambda w_: pl.BlockSpec((tm, w_), lambda i: (i, 0))
    const = lambda shape: pl.BlockSpec(shape, lambda i: (0,) * len(shape))
    return pl.pallas_call(
        functools.partial(_merge_kernel, tiles_per_seq=nt),
        out_shape=(jax.ShapeDtypeStruct((N, D_MODEL), F32), jax.ShapeDtypeStruct((N, D_MODEL), BF16)),
        grid=(N // tm,),
        in_specs=[
            row(POOL_DIM),
            pl.BlockSpec((POOL_MAXW, POOL_DIM), lambda i: (jnp.maximum(i * hb - 1, 0), 0)),
            row(NSA_DIM),
            row(MOBA_DIM),
            row(3 * D_MODEL),
            row(D_MODEL),
            const((POOL_DIM, POOL_DIM)),
            const((1, POOL_DIM)),
            const((POOL_DIM, D_MODEL)),
            const((NSA_DIM, D_MODEL)),
            const((MOBA_DIM, D_MODEL)),
            const((D_MODEL, D_MODEL)),
            const((1, D_MODEL)),
        ],
        out_specs=(row(D_MODEL), row(D_MODEL)),
        scratch_shapes=[pltpu.VMEM((POOL_MAXW + tm, POOL_DIM), F32)],
        compiler_params=pltpu.CompilerParams(
            dimension_semantics=("arbitrary",), vmem_limit_bytes=VMEM_LIMIT),
        name="merge",
    )(upool, upool, bn, cm, gbr, x2, pw, ps, wa, wb, wc, wo, gamma)


def _ffn_kernel(h_ref, x_ref, wg_ref, wu_ref, wd_ref, gam_ref, out_ref, acc_s, *, final_norm):
    f = pl.program_id(1)

    @pl.when(f == 0)
    def _():
        acc_s[...] = jnp.zeros_like(acc_s)

    h = h_ref[...]
    g = _dot(h, wg_ref[...])
    u = _dot(h, wu_ref[...])
    act = (g * jax.nn.sigmoid(g)) * u
    acc_s[...] += _dot(act.astype(BF16), wd_ref[...])

    @pl.when(f == pl.num_programs(1) - 1)
    def _():
        y = x_ref[...] + acc_s[...]
        if final_norm:
            y = _rms(y, gam_ref[...])
        out_ref[...] = y


def _ffn_call(h2, x1, wg, wu, wd, gamma, final_norm):
    N = x1.shape[0]
    F = wg.shape[1]
    tm, tf = FFN_ROW_TILE, FFN_COL_TILE
    return pl.pallas_call(
        functools.partial(_ffn_kernel, final_norm=final_norm),
        out_shape=jax.ShapeDtypeStruct((N, D_MODEL), F32),
        grid=(N // tm, F // tf),
        in_specs=[
            pl.BlockSpec((tm, D_MODEL), lambda i, f: (i, 0)),
            pl.BlockSpec((tm, D_MODEL), lambda i, f: (i, 0)),
            pl.BlockSpec((D_MODEL, tf), lambda i, f: (0, f)),
            pl.BlockSpec((D_MODEL, tf), lambda i, f: (0, f)),
            pl.BlockSpec((tf, D_MODEL), lambda i, f: (f, 0)),
            pl.BlockSpec((1, D_MODEL), lambda i, f: (0, 0)),
        ],
        out_specs=pl.BlockSpec((tm, D_MODEL), lambda i, f: (i, 0)),
        scratch_shapes=[pltpu.VMEM((tm, D_MODEL), F32)],
        compiler_params=pltpu.CompilerParams(
            dimension_semantics=("arbitrary", "arbitrary"), vmem_limit_bytes=VMEM_LIMIT),
        name="ffn",
    )(h2, x1, wg, wu, wd, gamma)


def _reorder_in_proj(w, b):
    o_qn = POOL_DIM
    o_kv = o_qn + NSA_DIM
    o_gn = o_kv + 6 * 128
    o_mo = o_gn + 3 * NSA_HEADS
    o_gb = o_mo + 3 * MOBA_DIM
    pad = 128 - 3 * NSA_HEADS

    def pick(a):
        parts = [a[..., 0:o_qn], a[..., o_kv:o_kv + 256], a[..., o_qn:o_kv], a[..., o_kv + 256:o_gn],
                 a[..., o_gn:o_mo], jnp.zeros(a.shape[:-1] + (pad,), a.dtype), a[..., o_mo:o_gb], a[..., o_gb:]]
        return jnp.concatenate(parts, axis=-1)

    return pick(w), pick(b)


def _rope_tables(S):
    pos = jnp.arange(S, dtype=F32)
    inv_freq = ROPE_THETA ** (-jnp.arange(0, HEAD_DIM, 2, dtype=F32) / HEAD_DIM)
    ang = pos[:, None] * inv_freq[None, :]
    cos, sin = jnp.cos(ang), jnp.sin(ang)
    cos_t = jnp.tile(cos, (1, 4))
    sin_t = jnp.tile(jnp.concatenate([-sin, sin], axis=1), (1, 2))
    return cos_t, sin_t


def kernel(x, attn_norm, w_in, b_in, pool_w, pool_scale, cmp_pos, cmp_w1, cmp_b1, cmp_w2, cmp_b2,
           w_br_pool, w_br_nsa, w_br_moba, w_out, ffn_norm, w_gate, w_up, w_down, final_norm):
    B, S, D = x.shape
    depth = w_in.shape[0]
    assert D == D_MODEL and S % KV_CHUNK == 0 and S % ROW_TILE == 0 and S >= WINDOW + Q_TILE
    assert (B * S) % FFN_ROW_TILE == 0 and w_gate.shape[2] % FFN_COL_TILE == 0
    N = B * S
    R = S // CMP_STRIDE
    cos_t, sin_t = _rope_tables(S)
    x2 = x.reshape(N, D)
    for l in range(depth):
        w_all, b_all = _reorder_in_proj(w_in[l], b_in[l])
        (upool, cmpk, cmpv, qtn, ksel, vselt, kwin, vwint, gt, qtm, km, vtm, kmean, gbr) = _proj_call(
            x2, attn_norm[l][None, :], w_all.astype(BF16), b_all[None, :], cos_t, sin_t, B, S)

        kc, vct = _compress_call(
            cmpk, cmpv, cmp_pos[l], cmp_w1[l].astype(BF16),
            cmp_b1[l][:, None, :], cmp_w2[l].astype(BF16), cmp_b2[l][:, None, :], B, R)

        bn = _nsa_call(qtn, gt, kc, vct, ksel, vselt, kwin, vwint, B, S)
        cm = _moba_call(qtm, kmean.reshape(B, S // MOBA_BLOCK, MOBA_DIM), km, vtm, B, S)

        pw_bd = jax.scipy.linalg.block_diag(*[pool_w[l, g] for g in range(len(POOL_WINDOWS))])
        x1, h2 = _merge_call(
            upool, bn, cm, gbr, x2, pw_bd.astype(BF16), pool_scale[l][None, :],
            w_br_pool[l].astype(BF16), w_br_nsa[l].astype(BF16), w_br_moba[l].astype(BF16),
            w_out[l].astype(BF16), ffn_norm[l][None, :], B, S)

        x2 = _ffn_call(h2, x1, w_gate[l].astype(BF16), w_up[l].astype(BF16), w_down[l].astype(BF16),
                       final_norm[None, :], final_norm=(l == depth - 1))
    return x2.reshape(B, S, D)
```

```python
import functools

import numpy as np
import jax
import jax.numpy as jnp
from jax import lax
from jax.experimental import pallas as pl
from jax.experimental.pallas import tpu as pltpu

F32 = jnp.float32
BF16 = jnp.bfloat16

D_MODEL = 1024
HEAD_DIM = 64
ROPE_THETA = 10000.0
RMS_EPS = 1e-6
NEG_INF = -1e30
FORCE = 1e30
LOG2E = 1.4426950408889634
LOOKAHEAD = 2
TRIP_UNROLL = 4
ONES_ROWS = 16
REMOVED = -(2.0 ** 127)

POOL_WINDOWS = (2, 4, 8, 16)
POOL_GROUP_DIM = 64
POOL_DIM = 256
POOL_MAXW = 16

NSA_HEADS = 8
NSA_KV_GROUPS = 2
NSA_HPG = 4
NSA_DIM = 512
CMP_BLOCK = 32
CMP_STRIDE = 16
CMP_HIDDEN = 256
SEL_BLOCK = 64
SEL_COUNT = 16
N_FORCED = 3
WINDOW = 512

MOBA_HEADS = 4
MOBA_DIM = 256
MOBA_BLOCK = 256
MOBA_TOPK = 3

Q_TILE = 256
LANES = 512
MOBA_Q_TILE = 512
KV_CHUNK = 512
KV_SUB = 256
ROW_TILE = 512
FFN_ROW_TILE = 1024
FFN_COL_TILE = 1408
VMEM_LIMIT = 56 * 1024 * 1024

C_POOL = 0
C_CMP = 256
C_QN = 512
C_KS = 1024
C_VS = 1152
C_KW = 1280
C_VW = 1408
C_GN = 1536
C_QM = 1664
C_KM = 1920
C_VM = 2176
C_GBR = 2432
C_TOTAL = 5504


def _dot(a, b):
    return jnp.dot(a, b, preferred_element_type=F32)


def _rms(x, gamma):
    return x * lax.rsqrt(jnp.mean(x * x, axis=-1, keepdims=True) + RMS_EPS) * gamma


def _proj_kernel(x_ref, gam_ref, w_ref, b_ref, cos_ref, sin_ref,
                 upool_ref, cmpk_ref, cmpv_ref, qtn_ref, ksel_ref, vselt_ref, kwin_ref, vwint_ref, gt_ref,
                 qtm_ref, km_ref, vtm_ref, kmean_ref, gbr_ref):
    tm = x_ref.shape[0]
    h = _rms(x_ref[...], gam_ref[...]).astype(BF16)
    cos = cos_ref[...]
    sin = sin_ref[...]
    lane = lax.broadcasted_iota(jnp.int32, (tm, 128), 1)
    first_half = (lane & 32) == 0
    scale = HEAD_DIM ** -0.5 * LOG2E

    def seg(a, b):
        return _dot(h, w_ref[:, a:b]) + b_ref[:, a:b]

    def rope(y):
        swap = jnp.where(first_half, pltpu.roll(y, 96, 1), pltpu.roll(y, 32, 1))
        return y * cos + swap * sin

    y = seg(C_POOL, C_POOL + 512)
    upool_ref[...] = y[:, :256]
    cmpk_ref[...] = rope(y[:, 256:384])
    cmpv_ref[...] = y[:, 384:512]

    y = seg(C_QN, C_QN + 512)
    for j in range(4):
        q = rope(y[:, 128 * j:128 * j + 128]) * scale
        qtn_ref[128 * j:128 * j + 128, :] = q.T.astype(BF16)

    y = seg(C_KS, C_KS + 512)
    ksel_ref[...] = rope(y[:, 0:128]).astype(BF16)
    vselt_ref[0] = y[:, 128:256].T.astype(BF16)
    kwin_ref[...] = rope(y[:, 256:384]).astype(BF16)
    vwt = y[:, 384:512].T.astype(BF16)
    for j in range(tm // 128):
        vwint_ref[j] = vwt[:, 128 * j:128 * j + 128]

    y = seg(C_GN, C_GN + 128)
    gt_ref[...] = y.T

    y = seg(C_QM, C_QM + 256)
    for j in range(2):
        q = rope(y[:, 128 * j:128 * j + 128]) * scale
        qtm_ref[128 * j:128 * j + 128, :] = q.T.astype(BF16)

    y = seg(C_KM, C_KM + 256)
    km = jnp.concatenate([rope(y[:, 0:128]), rope(y[:, 128:256])], axis=1)
    km_ref[...] = km.astype(BF16)
    nblk = tm // MOBA_BLOCK
    means = [jnp.sum(km[MOBA_BLOCK * j:MOBA_BLOCK * (j + 1), :], axis=0, keepdims=True) * (1.0 / MOBA_BLOCK)
             for j in range(nblk)]
    kmean_ref[...] = jnp.concatenate(means, axis=0)

    y = seg(C_VM, C_VM + 256)
    vtm_ref[0] = y.T.astype(BF16)

    for j in range(6):
        a = C_GBR + 512 * j
        gbr_ref[:, 512 * j:512 * j + 512] = seg(a, a + 512)


def _proj_call(x2, gamma, w, bias, cos_t, sin_t, B, S):
    N = B * S
    tm = ROW_TILE
    nt = S // tm
    f = lambda shape, dt: jax.ShapeDtypeStruct(shape, dt)
    out_shape = (
        f((N, 256), F32),
        f((N, 128), F32),
        f((N, 128), F32),
        f((B, NSA_DIM, S), BF16),
        f((N, 128), BF16),
        f((B, S // KV_CHUNK, 128, KV_CHUNK), BF16),
        f((N, 128), BF16),
        f((B, S // 128, 128, 128), BF16),
        f((B, 128, S), F32),
        f((B, MOBA_DIM, S), BF16),
        f((N, MOBA_DIM), BF16),
        f((B, S // KV_CHUNK, MOBA_DIM, KV_CHUNK), BF16),
        f((B, nt, tm // MOBA_BLOCK, MOBA_DIM), F32),
        f((N, 3 * D_MODEL), F32),
    )
    row = lambda w_: pl.BlockSpec((tm, w_), lambda i: (i, 0))
    const = lambda shape: pl.BlockSpec(shape, lambda i: (0,) * len(shape))
    in_specs = [
        row(D_MODEL),
        const((1, D_MODEL)),
        const((D_MODEL, C_TOTAL)),
        const((1, C_TOTAL)),
        pl.BlockSpec((tm, 128), lambda i: (i % nt, 0)),
        pl.BlockSpec((tm, 128), lambda i: (i % nt, 0)),
    ]
    out_specs = (
        row(256),
        row(128),
        row(128),
        pl.BlockSpec((None, NSA_DIM, tm), lambda i: (i // nt, 0, i % nt)),
        row(128),
        pl.BlockSpec((None, tm // KV_CHUNK, 128, KV_CHUNK), lambda i: (i // nt, i % nt, 0, 0)),
        row(128),
        pl.BlockSpec((None, tm // 128, 128, 128), lambda i: (i // nt, i % nt, 0, 0)),
        pl.BlockSpec((None, 128, tm), lambda i: (i // nt, 0, i % nt)),
        pl.BlockSpec((None, MOBA_DIM, tm), lambda i: (i // nt, 0, i % nt)),
        row(MOBA_DIM),
        pl.BlockSpec((None, tm // KV_CHUNK, MOBA_DIM, KV_CHUNK), lambda i: (i // nt, i % nt, 0, 0)),
        pl.BlockSpec((None, None, tm // MOBA_BLOCK, MOBA_DIM), lambda i: (i // nt, i % nt, 0, 0)),
        row(3 * D_MODEL),
    )
    return pl.pallas_call(
        _proj_kernel,
        out_shape=out_shape,
        grid=(N // tm,),
        in_specs=in_specs,
        out_specs=out_specs,
        compiler_params=pltpu.CompilerParams(
            dimension_semantics=("arbitrary",), vmem_limit_bytes=VMEM_LIMIT),
        name="proj",
    )(x2, gamma, w, bias, cos_t, sin_t)


def _gelu_tanh(x):
    return x * (0.5 * (1.0 + jnp.tanh(np.sqrt(2.0 / np.pi).astype(np.float32) * (x + 0.044715 * (x * x * x)))))


def _compress_kernel(xk_ref, xv_ref, pos_ref, w1_ref, b1_ref, w2_ref, b2_ref, kc_ref, vct_ref):
    R = xk_ref.shape[0] // CMP_STRIDE
    top = [None] * 4
    bot = [None] * 4
    for l in range(CMP_STRIDE):
        xl = [r[pl.ds(l, R, stride=CMP_STRIDE), :] for r in (xk_ref, xv_ref)]
        for j in range(4):
            t, g = divmod(j, NSA_KV_GROUPS)
            piece = xl[t][:, HEAD_DIM * g:HEAD_DIM * (g + 1)]
            lo = (piece + pos_ref[t, l:l + 1, :]).astype(BF16)
            hi = (piece + pos_ref[t, CMP_STRIDE + l:CMP_STRIDE + l + 1, :]).astype(BF16)
            a = _dot(lo, w1_ref[t, HEAD_DIM * l:HEAD_DIM * (l + 1), :])
            b = _dot(hi, w1_ref[t, HEAD_DIM * (CMP_STRIDE + l):HEAD_DIM * (CMP_STRIDE + l + 1), :])
            top[j] = a if top[j] is None else top[j] + a
            bot[j] = b if bot[j] is None else bot[j] + b
    outs = []
    for j in range(4):
        t = j // NSA_KV_GROUPS
        hid = _gelu_tanh(top[j] + pltpu.roll(bot[j], R - 1, 0) + b1_ref[t])
        outs.append(_dot(hid.astype(BF16), w2_ref[t]) + b2_ref[t])
    kc_ref[...] = jnp.concatenate(outs[0:2], axis=1).astype(BF16)
    vct_ref[...] = jnp.concatenate(outs[2:4], axis=1).T.astype(BF16)


def _compress_call(xk, xv, pos, w1, b1, w2, b2, B, R):
    S = R * CMP_STRIDE
    return pl.pallas_call(
        _compress_kernel,
        out_shape=(jax.ShapeDtypeStruct((B, R, 128), BF16), jax.ShapeDtypeStruct((B, 128, R), BF16)),
        grid=(B,),
        in_specs=[
            pl.BlockSpec((S, 128), lambda b: (b, 0)),
            pl.BlockSpec((S, 128), lambda b: (b, 0)),
            pl.BlockSpec((2, CMP_BLOCK, HEAD_DIM), lambda b: (0, 0, 0)),
            pl.BlockSpec((2, CMP_BLOCK * HEAD_DIM, CMP_HIDDEN), lambda b: (0, 0, 0)),
            pl.BlockSpec((2, 1, CMP_HIDDEN), lambda b: (0, 0, 0)),
            pl.BlockSpec((2, CMP_HIDDEN, HEAD_DIM), lambda b: (0, 0, 0)),
            pl.BlockSpec((2, 1, HEAD_DIM), lambda b: (0, 0, 0)),
        ],
        out_specs=(pl.BlockSpec((None, R, 128), lambda b: (b, 0, 0)),
                   pl.BlockSpec((None, 128, R), lambda b: (b, 0, 0))),
        compiler_params=pltpu.CompilerParams(
            dimension_semantics=("arbitrary",), vmem_limit_bytes=VMEM_LIMIT),
        name="nsa_compress",
    )(xk, xv, pos, w1, b1, w2, b2)


def _softmax_block(s):
    m = jnp.max(s, axis=0, keepdims=True)
    m_use = jnp.where(m < 0.5 * NEG_INF, 0.0, m)
    p = jnp.exp2(s - m_use)
    return p, jnp.sum(p, axis=0, keepdims=True)


def _flash_update(sc, rows, block, vt, m, acc_ref):
    n, L = sc.shape[0] // block, sc.shape[1]
    parts = [sc[block * j:block * (j + 1), :] for j in range(n)]
    tops = [jnp.max(parts[j].reshape(block // 8, 8, L), axis=0) for j in range(n)]
    if rows is not None:
        tops = [tops[j] + rows[j:j + 1, :] for j in range(n)]
    top = tops[0]
    for t in tops[1:]:
        top = jnp.maximum(top, t)
    m_new = jnp.maximum(m, jnp.max(top, axis=0, keepdims=True))
    m_use = jnp.where(m_new < 0.5 * NEG_INF, 0.0, m_new)
    alpha = jnp.exp2(m - m_use)
    if rows is None:
        p = jnp.exp2(sc - m_use)
    else:
        p = jnp.concatenate([jnp.exp2(parts[j] + (rows[j:j + 1, :] - m_use)) for j in range(n)], axis=0)
    acc_ref[...] = alpha * acc_ref[...] + _dot(vt, p.astype(BF16))
    return m_new


def _run_steps(steps, pending, tail):
    pending = list(pending)
    fns = [s for s, _ in steps] + list(tail)
    for t, (_, consume) in enumerate(steps):
        s = pending.pop(0)
        if t + LOOKAHEAD < len(fns):
            pending.append(fns[t + LOOKAHEAD]())
        consume(s)
    return tuple(pending)


def _chunk_loop(body, n, carry):
    def multi(i, cr):
        for j in range(TRIP_UNROLL):
            cr = body(TRIP_UNROLL * i + j, cr)
        return cr

    carry = lax.fori_loop(0, n // TRIP_UNROLL, multi, carry)
    return lax.fori_loop(TRIP_UNROLL * (n // TRIP_UNROLL), n, body, carry)


def _mask_bias(ok, reps):
    return jnp.concatenate([jnp.where(ok, 0.0, NEG_INF)] * reps, axis=1)


def _topk_mask(scores, rows_f, k):
    work = scores
    for _ in range(k):
        cm = jnp.max(work, axis=0, keepdims=True)
        idx = jnp.min(jnp.where(work == cm, rows_f, 1e9), axis=0, keepdims=True)
        work = jnp.where(rows_f == idx, REMOVED, work)
    return jnp.where(work == REMOVED, 1.0, 0.0)


def _add_block_bias(sc, rows, block):
    n = sc.shape[0] // block
    return jnp.concatenate([sc[block * j:block * (j + 1), :] + rows[j:j + 1, :] for j in range(n)], axis=0)


def _nsa_kernel(qt_ref, gt_ref, kc_ref, vct_ref, ksel_ref, vselt_ref, kwin_ref, vwint_ref, out_ref,
                qt_s, pg_s, bias_s, acc_s, accw_s):
    qi = pl.program_id(1)
    q0 = qi * Q_TILE
    G = NSA_KV_GROUPS
    HPL = LANES // Q_TILE
    PARTS = NSA_HPG // HPL
    NLG = G * PARTS
    n_cmp_rows = kc_ref.shape[0]
    n_sel = bias_s.shape[1]
    group_of = lambda lg: lg // PARTS
    rows_of = lambda g: slice(HEAD_DIM * g, HEAD_DIM * (g + 1))

    qt_s[...] = jnp.zeros_like(qt_s)
    for h in range(NSA_HEADS):
        g, hh = divmod(h, NSA_HPG)
        part, hl = divmod(hh, HPL)
        qt_s[g * PARTS + part, rows_of(g), Q_TILE * hl:Q_TILE * (hl + 1)] = qt_ref[HEAD_DIM * h:HEAD_DIM * (h + 1), :]
    qts = [qt_s[lg] for lg in range(NLG)]
    tq1 = q0 + lax.broadcasted_iota(jnp.int32, (1, Q_TILE), 1)

    def ones_under(vt):
        return jnp.concatenate([vt, jnp.ones((ONES_ROWS, vt.shape[1]), BF16)], axis=0)

    kc = kc_ref[...]
    n_idx = lax.broadcasted_iota(jnp.int32, (n_cmp_rows, 1), 0)
    cbias = _mask_bias(n_idx * CMP_STRIDE + (CMP_BLOCK - 1) <= tq1, HPL)
    o_cmp = []
    for g in range(G):
        pg = None
        for part in range(PARTS):
            q = qts[g * PARTS + part]
            s = jnp.concatenate([_dot(kc[r:r + KV_SUB, :], q) for r in range(0, n_cmp_rows, KV_SUB)], axis=0) + cbias
            p, l = _softmax_block(s)
            den = jnp.maximum(l, 1e-30)
            o_cmp.append(_dot(vct_ref[rows_of(g), :], p.astype(BF16)) / den)
            pn = p / den
            for hl in range(HPL):
                ph = pn[:, hl * Q_TILE:(hl + 1) * Q_TILE]
                pg = ph if pg is None else pg + ph
        for sl in range(Q_TILE // 128):
            pg_s[g * (Q_TILE // 128) + sl, 0:8, :] = jnp.zeros((8, 128), F32)
            pg_s[g * (Q_TILE // 128) + sl, 8:8 + n_cmp_rows, :] = pg[:, 128 * sl:128 * (sl + 1)]

    W = WINDOW + Q_TILE
    wstart = jnp.maximum(q0 - WINDOW, 0)
    wc0 = wstart // 128
    kposw = wstart + lax.broadcasted_iota(jnp.int32, (W, 1), 0)
    wbias = _mask_bias((kposw <= tq1) & (kposw > tq1 - WINDOW), HPL)
    vtw = jnp.concatenate([vwint_ref[wc0 + j] for j in range(W // 128)], axis=1)
    accw_s[...] = jnp.zeros_like(accw_s)
    mw = [jnp.full((1, LANES), NEG_INF, F32)] * NLG
    win_steps = []
    for r0 in range(0, W, KV_SUB):
        r1 = min(r0 + KV_SUB, W)
        for lg in range(NLG):
            def score(r0=r0, r1=r1, lg=lg):
                k = kwin_ref[pl.ds(pl.multiple_of(wstart + r0, 128), r1 - r0), :]
                return _dot(k, qts[lg]) + wbias[r0:r1, :]

            def consume(s, r0=r0, r1=r1, lg=lg):
                vt = ones_under(vtw[rows_of(group_of(lg)), r0:r1])
                mw[lg] = _flash_update(s, None, r1 - r0, vt, mw[lg], accw_s.at[lg])

            win_steps.append((score, consume))
    _run_steps(win_steps, [fn() for fn, _ in win_steps[:LOOKAHEAD]], [])

    blk_i = lax.broadcasted_iota(jnp.int32, (n_sel, 1), 0)
    cur = tq1 // SEL_BLOCK
    forced = (blk_i == 0) | (blk_i == cur) | (blk_i == cur - 1)
    started = blk_i <= cur
    n_top = min(SEL_COUNT, n_sel)
    slc = []
    for g in range(G):
        slabs = []
        for sl in range(Q_TILE // 128):
            def tap(w):
                return pg_s[g * (Q_TILE // 128) + sl, pl.ds(8 + w, n_sel, stride=SEL_BLOCK // CMP_STRIDE), :]

            t = tap(-1) + 2.0 * tap(0)
            t = t + 2.0 * tap(1)
            t = t + 2.0 * tap(2)
            t = t + tap(3)
            slabs.append(t)
        t = slabs[0] if len(slabs) == 1 else jnp.concatenate(slabs, axis=1)
        slc.append(jnp.where(forced, NEG_INF, jnp.where(started, t, NEG_INF)))
    free = _topk_mask(jnp.concatenate(slc, axis=1), jnp.broadcast_to(blk_i.astype(F32), (n_sel, G * Q_TILE)),
                      n_top - N_FORCED)
    for g in range(G):
        bias = jnp.where(forced, 0.0, jnp.where(free[:, Q_TILE * g:Q_TILE * (g + 1)] > 0.5, 0.0, NEG_INF))
        bias_s[g] = jnp.concatenate([bias] * HPL, axis=1)

    bpc = KV_CHUNK // SEL_BLOCK
    bps = KV_SUB // SEL_BLOCK
    c_last = q0 // KV_CHUNK

    def chunk_steps(c, ms, causal):
        steps = []
        for sub in range(KV_CHUNK // KV_SUB):
            for lg in range(NLG):
                def score(sub=sub, lg=lg):
                    base = pl.multiple_of(c * KV_CHUNK, KV_CHUNK) + sub * KV_SUB
                    s = _dot(ksel_ref[pl.ds(pl.multiple_of(base, KV_SUB), KV_SUB), :], qts[lg])
                    return s if causal is None else s + causal[sub]

                def consume(s, sub=sub, lg=lg):
                    g = group_of(lg)
                    rows = bias_s[g, pl.ds(pl.multiple_of(c * bpc, bpc), bpc), :][bps * sub:bps * (sub + 1), :]
                    vt = vselt_ref[c, rows_of(g), KV_SUB * sub:KV_SUB * (sub + 1)]
                    ms[lg] = _flash_update(s, rows, SEL_BLOCK, ones_under(vt), ms[lg], acc_s.at[lg])

                steps.append((score, consume))
        return steps

    acc_s[...] = jnp.zeros_like(acc_s)

    def body(c, carry):
        ms = list(carry[:NLG])
        nxt = [fn for fn, _ in chunk_steps(c + 1, ms, None)[:LOOKAHEAD]]
        pending = _run_steps(chunk_steps(c, ms, None), carry[NLG:], nxt)
        return tuple(ms) + pending

    first = [fn() for fn, _ in chunk_steps(0, None, None)[:LOOKAHEAD]]
    carry = (jnp.full((1, LANES), NEG_INF, F32),) * NLG + tuple(first)
    carry = _chunk_loop(body, c_last, carry)
    kpos = c_last * KV_CHUNK + lax.broadcasted_iota(jnp.int32, (KV_CHUNK, 1), 0)
    causal = _mask_bias(kpos <= tq1, HPL)
    diag = chunk_steps(c_last, list(carry[:NLG]), [causal[KV_SUB * sub:KV_SUB * (sub + 1), :]
                                                 for sub in range(KV_CHUNK // KV_SUB)])
    _run_steps(diag, [fn() for fn, _ in diag[:LOOKAHEAD]], [])

    gates = jax.nn.sigmoid(gt_ref[0:3 * NSA_HEADS, :])
    outs = []
    for h in range(NSA_HEADS):
        g, hh = divmod(h, NSA_HPG)
        part, hl = divmod(hh, HPL)
        lg = g * PARTS + part
        cols = slice(Q_TILE * hl, Q_TILE * (hl + 1))
        o_sel = acc_s[lg, 0:HEAD_DIM, cols] / jnp.maximum(acc_s[lg, HEAD_DIM:HEAD_DIM + 1, cols], 1e-30)
        o_win = accw_s[lg, 0:HEAD_DIM, cols] / jnp.maximum(accw_s[lg, HEAD_DIM:HEAD_DIM + 1, cols], 1e-30)
        o = gates[3 * h:3 * h + 1, :] * o_cmp[lg][:, cols]
        o = o + gates[3 * h + 1:3 * h + 2, :] * o_sel
        o = o + gates[3 * h + 2:3 * h + 3, :] * o_win
        outs.append(o)
    out_ref[...] = jnp.concatenate(outs, axis=0).T.astype(BF16)


def _nsa_call(qtn, gt, kc, vct, ksel, vselt, kwin, vwint, B, S):
    R = S // CMP_STRIDE
    n_sel = S // SEL_BLOCK
    nlg = NSA_HEADS * Q_TILE // LANES
    return pl.pallas_call(
        _nsa_kernel,
        out_shape=jax.ShapeDtypeStruct((B * S, NSA_DIM), BF16),
        grid=(B, S // Q_TILE),
        in_specs=[
            pl.BlockSpec((None, NSA_DIM, Q_TILE), lambda b, q: (b, 0, q)),
            pl.BlockSpec((None, 128, Q_TILE), lambda b, q: (b, 0, q)),
            pl.BlockSpec((None, R, 128), lambda b, q: (b, 0, 0)),
            pl.BlockSpec((None, 128, R), lambda b, q: (b, 0, 0)),
            pl.BlockSpec((S, 128), lambda b, q: (b, 0)),
            pl.BlockSpec((None, S // KV_CHUNK, 128, KV_CHUNK), lambda b, q: (b, 0, 0, 0)),
            pl.BlockSpec((S, 128), lambda b, q: (b, 0)),
            pl.BlockSpec((None, S // 128, 128, 128), lambda b, q: (b, 0, 0, 0)),
        ],
        out_specs=pl.BlockSpec((Q_TILE, NSA_DIM), lambda b, q: (b * (S // Q_TILE) + q, 0)),
        scratch_shapes=[
            pltpu.VMEM((nlg, 128, LANES), BF16),
            pltpu.VMEM((NSA_KV_GROUPS * (Q_TILE // 128), 8 + R, 128), F32),
            pltpu.VMEM((NSA_KV_GROUPS, n_sel, LANES), F32),
            pltpu.VMEM((nlg, HEAD_DIM + ONES_ROWS, LANES), F32),
            pltpu.VMEM((nlg, HEAD_DIM + ONES_ROWS, LANES), F32),
        ],
        compiler_params=pltpu.CompilerParams(
            dimension_semantics=("arbitrary", "arbitrary"), vmem_limit_bytes=VMEM_LIMIT),
        name="nsa_attn",
    )(qtn, gt, kc, vct, ksel, vselt, kwin, vwint)


def _moba_kernel(qt_ref, kmean_ref, k_ref, vt_ref, out_ref, qt_s, bias_s, acc_s):
    qi = pl.program_id(1)
    q0 = qi * MOBA_Q_TILE
    L = MOBA_Q_TILE
    H = MOBA_HEADS
    n_blk = kmean_ref.shape[0]
    n_top = max(1, min(MOBA_TOPK, n_blk - 1))

    qt_s[...] = jnp.zeros_like(qt_s)
    for h in range(H):
        rows = slice(HEAD_DIM * h, HEAD_DIM * (h + 1))
        qt_s[h, rows, :] = qt_ref[rows, :]
    qts = [qt_s[h] for h in range(H)]
    tq = q0 + lax.broadcasted_iota(jnp.int32, (1, L), 1)

    own = tq // MOBA_BLOCK
    blk_i = lax.broadcasted_iota(jnp.int32, (n_blk, 1), 0)
    blk_f = jnp.broadcast_to(blk_i.astype(F32), (n_blk, L))
    kmean = kmean_ref[...].astype(BF16)
    for h in range(H):
        sg = jnp.where(blk_i < own, _dot(kmean, qts[h]), NEG_INF)
        sel = _topk_mask(sg, blk_f, n_top)
        allow = ((sel > 0.5) & (blk_i < own)) | (blk_i == own)
        bias = jnp.where(allow, 0.0, NEG_INF)
        for j in range(n_blk):
            bias_s[h, j] = jnp.broadcast_to(bias[j:j + 1, :], (8, L))

    bpc = KV_CHUNK // MOBA_BLOCK

    steps = [(j, h) for j in range(bpc) for h in range(H)]

    ones = jnp.ones((ONES_ROWS, MOBA_BLOCK), BF16)

    def scores(c, j, h, diagonal):
        base = pl.multiple_of(c * KV_CHUNK, KV_CHUNK) + j * MOBA_BLOCK
        k = k_ref[pl.ds(pl.multiple_of(base, MOBA_BLOCK), MOBA_BLOCK), :]
        s = _dot(k, qts[h])
        if diagonal:
            kpos = base + lax.broadcasted_iota(jnp.int32, (MOBA_BLOCK, 1), 0)
            s = jnp.where(kpos <= tq, s, NEG_INF)
        return s

    def chunk_steps(c, ms, pending, diagonal):
        ms, pending = list(ms), list(pending)
        for t, (j, h) in enumerate(steps):
            s = pending.pop(0)
            u = t + LOOKAHEAD
            if u < len(steps):
                pending.append(scores(c, *steps[u], diagonal))
            elif not diagonal:
                pending.append(scores(c + 1, *steps[u - len(steps)], False))
            vt = vt_ref[c, HEAD_DIM * h:HEAD_DIM * (h + 1), MOBA_BLOCK * j:MOBA_BLOCK * (j + 1)]
            ms[h] = _flash_update(s, bias_s[h, c * bpc + j], MOBA_BLOCK, jnp.concatenate([vt, ones], axis=0),
                                  ms[h], acc_s.at[h])
        return tuple(ms), tuple(pending)

    acc_s[...] = jnp.zeros_like(acc_s)
    c_last = q0 // KV_CHUNK
    init = (jnp.full((1, L), NEG_INF, F32),) * H

    def body(c, carry):
        ms, pending = chunk_steps(c, carry[:H], carry[H:], False)
        return ms + pending

    carry = init + tuple(scores(0, *steps[i], False) for i in range(LOOKAHEAD))
    carry = _chunk_loop(body, c_last, carry)
    chunk_steps(c_last, carry[:H], [scores(c_last, *steps[i], True) for i in range(LOOKAHEAD)], True)
    outs = [acc_s[h, 0:HEAD_DIM, :] / jnp.maximum(acc_s[h, HEAD_DIM:HEAD_DIM + 1, :], 1e-30) for h in range(H)]
    out_ref[...] = jnp.concatenate(outs, axis=0).T.astype(BF16)


def _moba_call(qtm, kmean, km, vtm, B, S):
    n_blk = S // MOBA_BLOCK
    L = MOBA_Q_TILE
    return pl.pallas_call(
        _moba_kernel,
        out_shape=jax.ShapeDtypeStruct((B * S, MOBA_DIM), BF16),
        grid=(B, S // L),
        in_specs=[
            pl.BlockSpec((None, MOBA_DIM, L), lambda b, q: (b, 0, q)),
            pl.BlockSpec((None, n_blk, MOBA_DIM), lambda b, q: (b, 0, 0)),
            pl.BlockSpec((S, MOBA_DIM), lambda b, q: (b, 0)),
            pl.BlockSpec((None, S // KV_CHUNK, MOBA_DIM, KV_CHUNK), lambda b, q: (b, 0, 0, 0)),
        ],
        out_specs=pl.BlockSpec((L, MOBA_DIM), lambda b, q: (b * (S // L) + q, 0)),
        scratch_shapes=[
            pltpu.VMEM((MOBA_HEADS, MOBA_DIM, L), BF16),
            pltpu.VMEM((MOBA_HEADS, n_blk, 8, L), F32),
            pltpu.VMEM((MOBA_HEADS, HEAD_DIM + ONES_ROWS, L), F32),
        ],
        compiler_params=pltpu.CompilerParams(
            dimension_semantics=("arbitrary", "arbitrary"), vmem_limit_bytes=VMEM_LIMIT),
        name="moba_attn",
    )(qtm, kmean, km, vtm)


def _merge_kernel(u_ref, halo_ref, b_ref, c_ref, gbr_ref, x_ref, pw_ref, ps_ref, wa_ref, wb_ref, wc_ref,
                  wo_ref, gam_ref, x1_ref, h2_ref, ext_s, *, tiles_per_seq):
    tm = u_ref.shape[0]
    i = pl.program_id(0)
    first = (i % tiles_per_seq) == 0
    u = u_ref[...]
    ext_s[0:POOL_MAXW, :] = jnp.where(first, 0.0, halo_ref[...])
    ext_s[POOL_MAXW:POOL_MAXW + tm, :] = u

    def tail_sum(col, k0, k1):
        tot = None
        for k in range(k0, k1):
            v = ext_s[pl.ds(POOL_MAXW - k, tm), 128 * col:128 * col + 128]
            tot = v if tot is None else tot + v
        return tot

    t_glob = (i % tiles_per_seq) * tm + lax.broadcasted_iota(jnp.int32, (tm, 1), 0)
    lane = lax.broadcasted_iota(jnp.int32, (1, 128), 1)
    low = lane < POOL_GROUP_DIM
    pooled = []
    for col in range(2):
        wa_, wb_ = POOL_WINDOWS[2 * col], POOL_WINDOWS[2 * col + 1]
        sa = tail_sum(col, 0, wa_)
        sb = sa + tail_sum(col, wa_, wb_)
        ca = jnp.minimum(t_glob + 1, wa_).astype(F32)
        cb = jnp.minimum(t_glob + 1, wb_).astype(F32)
        pooled.append(jnp.where(low, sa, sb) / jnp.where(low, ca, cb))
    d = jnp.concatenate(pooled, axis=1) - u
    a = _dot(d.astype(BF16), pw_ref[...]) * ps_ref[...]

    av = _dot(a.astype(BF16), wa_ref[...])
    bv = _dot(b_ref[...], wb_ref[...])
    cv = _dot(c_ref[...], wc_ref[...])
    merged = jax.nn.sigmoid(gbr_ref[:, 0:D_MODEL]) * av
    merged = merged + jax.nn.sigmoid(gbr_ref[:, D_MODEL:2 * D_MODEL]) * bv
    merged = merged + jax.nn.sigmoid(gbr_ref[:, 2 * D_MODEL:3 * D_MODEL]) * cv
    x1 = x_ref[...] + _dot(merged.astype(BF16), wo_ref[...])
    x1_ref[...] = x1
    h2_ref[...] = _rms(x1, gam_ref[...]).astype(BF16)


def _merge_call(upool, bn, cm, gbr, x2, pw, ps, wa, wb, wc, wo, gamma, B, S):
    N = B * S
    tm = ROW_TILE
    nt = S // tm
    hb = tm // POOL_MAXW
    row = lambda w_: pl.BlockSpec((tm, w_), lambda i: (i, 0))
    const = lambda shape: pl.BlockSpec(shape, lambda i: (0,) * len(shape))
    return pl.pallas_call(
        functools.partial(_merge_kernel, tiles_per_seq=nt),
        out_shape=(jax.ShapeDtypeStruct((N, D_MODEL), F32), jax.ShapeDtypeStruct((N, D_MODEL), BF16)),
        grid=(N // tm,),
        in_specs=[
            row(POOL_DIM),
            pl.BlockSpec((POOL_MAXW, POOL_DIM), lambda i: (jnp.maximum(i * hb - 1, 0), 0)),
            row(NSA_DIM),
            row(MOBA_DIM),
            row(3 * D_MODEL),
            row(D_MODEL),
            const((POOL_DIM, POOL_DIM)),
            const((1, POOL_DIM)),
            const((POOL_DIM, D_MODEL)),
            const((NSA_DIM, D_MODEL)),
            const((MOBA_DIM, D_MODEL)),
            const((D_MODEL, D_MODEL)),
            const((1, D_MODEL)),
        ],
        out_specs=(row(D_MODEL), row(D_MODEL)),
        scratch_shapes=[pltpu.VMEM((POOL_MAXW + tm, POOL_DIM), F32)],
        compiler_params=pltpu.CompilerParams(
            dimension_semantics=("arbitrary",), vmem_limit_bytes=VMEM_LIMIT),
        name="merge",
    )(upool, upool, bn, cm, gbr, x2, pw, ps, wa, wb, wc, wo, gamma)


def _ffn_kernel(h_ref, x_ref, wg_ref, wu_ref, wd_ref, gam_ref, out_ref, acc_s, *, final_norm):
    f = pl.program_id(1)

    @pl.when(f == 0)
    def _():
        acc_s[...] = jnp.zeros_like(acc_s)

    h = h_ref[...]
    g = _dot(h, wg_ref[...])
    u = _dot(h, wu_ref[...])
    act = (g * jax.nn.sigmoid(g)) * u
    acc_s[...] += _dot(act.astype(BF16), wd_ref[...])

    @pl.when(f == pl.num_programs(1) - 1)
    def _():
        y = x_ref[...] + acc_s[...]
        if final_norm:
            y = _rms(y, gam_ref[...])
        out_ref[...] = y


def _ffn_call(h2, x1, wg, wu, wd, gamma, final_norm):
    N = x1.shape[0]
    F = wg.shape[1]
    tm, tf = FFN_ROW_TILE, FFN_COL_TILE
    return pl.pallas_call(
        functools.partial(_ffn_kernel, final_norm=final_norm),
        out_shape=jax.ShapeDtypeStruct((N, D_MODEL), F32),
        grid=(N // tm, F // tf),
        in_specs=[
            pl.BlockSpec((tm, D_MODEL), lambda i, f: (i, 0)),
            pl.BlockSpec((tm, D_MODEL), lambda i, f: (i, 0)),
            pl.BlockSpec((D_MODEL, tf), lambda i, f: (0, f)),
            pl.BlockSpec((D_MODEL, tf), lambda i, f: (0, f)),
            pl.BlockSpec((tf, D_MODEL), lambda i, f: (f, 0)),
            pl.BlockSpec((1, D_MODEL), lambda i, f: (0, 0)),
        ],
        out_specs=pl.BlockSpec((tm, D_MODEL), lambda i, f: (i, 0)),
        scratch_shapes=[pltpu.VMEM((tm, D_MODEL), F32)],
        compiler_params=pltpu.CompilerParams(
            dimension_semantics=("arbitrary", "arbitrary"), vmem_limit_bytes=VMEM_LIMIT),
        name="ffn",
    )(h2, x1, wg, wu, wd, gamma)


def _reorder_in_proj(w, b):
    o_qn = POOL_DIM
    o_kv = o_qn + NSA_DIM
    o_gn = o_kv + 6 * 128
    o_mo = o_gn + 3 * NSA_HEADS
    o_gb = o_mo + 3 * MOBA_DIM
    pad = 128 - 3 * NSA_HEADS

    def pick(a):
        parts = [a[..., 0:o_qn], a[..., o_kv:o_kv + 256], a[..., o_qn:o_kv], a[..., o_kv + 256:o_gn],
                 a[..., o_gn:o_mo], jnp.zeros(a.shape[:-1] + (pad,), a.dtype), a[..., o_mo:o_gb], a[..., o_gb:]]
        return jnp.concatenate(parts, axis=-1)

    return pick(w), pick(b)


def _rope_tables(S):
    pos = jnp.arange(S, dtype=F32)
    inv_freq = ROPE_THETA ** (-jnp.arange(0, HEAD_DIM, 2, dtype=F32) / HEAD_DIM)
    ang = pos[:, None] * inv_freq[None, :]
    cos, sin = jnp.cos(ang), jnp.sin(ang)
    cos_t = jnp.tile(cos, (1, 4))
    sin_t = jnp.tile(jnp.concatenate([-sin, sin], axis=1), (1, 2))
    return cos_t, sin_t


def kernel(x, attn_norm, w_in, b_in, pool_w, pool_scale, cmp_pos, cmp_w1, cmp_b1, cmp_w2, cmp_b2,
           w_br_pool, w_br_nsa, w_br_moba, w_out, ffn_norm, w_gate, w_up, w_down, final_norm):
    B, S, D = x.shape
    depth = w_in.shape[0]
    assert D == D_MODEL and S % KV_CHUNK == 0 and S % ROW_TILE == 0 and S >= WINDOW + Q_TILE
    assert (B * S) % FFN_ROW_TILE == 0 and w_gate.shape[2] % FFN_COL_TILE == 0
    N = B * S
    R = S // CMP_STRIDE
    cos_t, sin_t = _rope_tables(S)
    x2 = x.reshape(N, D)
    for l in range(depth):
        w_all, b_all = _reorder_in_proj(w_in[l], b_in[l])
        (upool, cmpk, cmpv, qtn, ksel, vselt, kwin, vwint, gt, qtm, km, vtm, kmean, gbr) = _proj_call(
            x2, attn_norm[l][None, :], w_all.astype(BF16), b_all[None, :], cos_t, sin_t, B, S)

        kc, vct = _compress_call(
            cmpk, cmpv, cmp_pos[l], cmp_w1[l].astype(BF16),
            cmp_b1[l][:, None, :], cmp_w2[l].astype(BF16), cmp_b2[l][:, None, :], B, R)

        bn = _nsa_call(qtn, gt, kc, vct, ksel, vselt, kwin, vwint, B, S)
        cm = _moba_call(qtm, kmean.reshape(B, S // MOBA_BLOCK, MOBA_DIM), km, vtm, B, S)

        pw_bd = jax.scipy.linalg.block_diag(*[pool_w[l, g] for g in range(len(POOL_WINDOWS))])
        x1, h2 = _merge_call(
            upool, bn, cm, gbr, x2, pw_bd.astype(BF16), pool_scale[l][None, :],
            w_br_pool[l].astype(BF16), w_br_nsa[l].astype(BF16), w_br_moba[l].astype(BF16),
            w_out[l].astype(BF16), ffn_norm[l][None, :], B, S)

        x2 = _ffn_call(h2, x1, w_gate[l].astype(BF16), w_up[l].astype(BF16), w_down[l].astype(BF16),
                       final_norm[None, :], final_norm=(l == depth - 1))
    return x2.reshape(B, S, D)
```

```python
import functools

import numpy as np
import jax
import jax.numpy as jnp
from jax import lax
from jax.experimental import pallas as pl
from jax.experimental.pallas import tpu as pltpu

F32 = jnp.float32
BF16 = jnp.bfloat16

D_MODEL = 1024
HEAD_DIM = 64
ROPE_THETA = 10000.0
RMS_EPS = 1e-6
NEG_INF = -1e30
FORCE = 1e30
LOG2E = 1.4426950408889634
LOOKAHEAD = 2
NSA_TRIP_UNROLL = 4
MOBA_TRIP_UNROLL = 4
ONES_ROWS = 16
REMOVED = -(2.0 ** 127)

POOL_WINDOWS = (2, 4, 8, 16)
POOL_GROUP_DIM = 64
POOL_DIM = 256
POOL_MAXW = 16

NSA_HEADS = 8
NSA_KV_GROUPS = 2
NSA_HPG = 4
NSA_DIM = 512
CMP_BLOCK = 32
CMP_STRIDE = 16
CMP_HIDDEN = 256
SEL_BLOCK = 64
SEL_COUNT = 16
N_FORCED = 3
WINDOW = 512

MOBA_HEADS = 4
MOBA_DIM = 256
MOBA_BLOCK = 256
MOBA_TOPK = 3

Q_TILE = 256
LANES = 512
MOBA_Q_TILE = 512
KV_CHUNK = 512
KV_SUB = 256
ROW_TILE = 512
FFN_ROW_TILE = 1024
FFN_COL_TILE = 1408
VMEM_LIMIT = 56 * 1024 * 1024

C_POOL = 0
C_CMP = 256
C_QN = 512
C_KS = 1024
C_VS = 1152
C_KW = 1280
C_VW = 1408
C_GN = 1536
C_QM = 1664
C_KM = 1920
C_VM = 2176
C_GBR = 2432
C_TOTAL = 5504


def _dot(a, b):
    return jnp.dot(a, b, preferred_element_type=F32)


def _rms(x, gamma):
    return x * lax.rsqrt(jnp.mean(x * x, axis=-1, keepdims=True) + RMS_EPS) * gamma


def _proj_kernel(x_ref, gam_ref, w_ref, b_ref, cos_ref, sin_ref,
                 upool_ref, cmpk_ref, cmpv_ref, qtn_ref, ksel_ref, vselt_ref, kwin_ref, vwint_ref, gt_ref,
                 qtm_ref, km_ref, vtm_ref, kmean_ref, gbr_ref):
    tm = x_ref.shape[0]
    h = _rms(x_ref[...], gam_ref[...]).astype(BF16)
    cos = cos_ref[...]
    sin = sin_ref[...]
    lane = lax.broadcasted_iota(jnp.int32, (tm, 128), 1)
    first_half = (lane & 32) == 0
    scale = HEAD_DIM ** -0.5 * LOG2E

    def seg(a, b):
        return _dot(h, w_ref[:, a:b]) + b_ref[:, a:b]

    def rope(y):
        swap = jnp.where(first_half, pltpu.roll(y, 96, 1), pltpu.roll(y, 32, 1))
        return y * cos + swap * sin

    y = seg(C_POOL, C_POOL + 512)
    upool_ref[...] = y[:, :256]
    cmpk_ref[...] = rope(y[:, 256:384])
    cmpv_ref[...] = y[:, 384:512]

    y = seg(C_QN, C_QN + 512)
    for j in range(4):
        q = rope(y[:, 128 * j:128 * j + 128]) * scale
        qtn_ref[128 * j:128 * j + 128, :] = q.T.astype(BF16)

    y = seg(C_KS, C_KS + 512)
    ksel_ref[...] = rope(y[:, 0:128]).astype(BF16)
    vselt_ref[0] = y[:, 128:256].T.astype(BF16)
    kwin_ref[...] = rope(y[:, 256:384]).astype(BF16)
    vwt = y[:, 384:512].T.astype(BF16)
    for j in range(tm // 128):
        vwint_ref[j] = vwt[:, 128 * j:128 * j + 128]

    y = seg(C_GN, C_GN + 128)
    gt_ref[...] = y.T

    y = seg(C_QM, C_QM + 256)
    for j in range(2):
        q = rope(y[:, 128 * j:128 * j + 128]) * scale
        qtm_ref[128 * j:128 * j + 128, :] = q.T.astype(BF16)

    y = seg(C_KM, C_KM + 256)
    km = jnp.concatenate([rope(y[:, 0:128]), rope(y[:, 128:256])], axis=1)
    km_ref[...] = km.astype(BF16)
    nblk = tm // MOBA_BLOCK
    means = [jnp.sum(km[MOBA_BLOCK * j:MOBA_BLOCK * (j + 1), :], axis=0, keepdims=True) * (1.0 / MOBA_BLOCK)
             for j in range(nblk)]
    kmean_ref[...] = jnp.concatenate(means, axis=0)

    y = seg(C_VM, C_VM + 256)
    vtm_ref[0] = y.T.astype(BF16)

    for j in range(6):
        a = C_GBR + 512 * j
        gbr_ref[:, 512 * j:512 * j + 512] = seg(a, a + 512)


def _proj_call(x2, gamma, w, bias, cos_t, sin_t, B, S):
    N = B * S
    tm = ROW_TILE
    nt = S // tm
    f = lambda shape, dt: jax.ShapeDtypeStruct(shape, dt)
    out_shape = (
        f((N, 256), F32),
        f((N, 128), F32),
        f((N, 128), F32),
        f((B, NSA_DIM, S), BF16),
        f((N, 128), BF16),
        f((B, S // KV_CHUNK, 128, KV_CHUNK), BF16),
        f((N, 128), BF16),
        f((B, S // 128, 128, 128), BF16),
        f((B, 128, S), F32),
        f((B, MOBA_DIM, S), BF16),
        f((N, MOBA_DIM), BF16),
        f((B, S // KV_CHUNK, MOBA_DIM, KV_CHUNK), BF16),
        f((B, nt, tm // MOBA_BLOCK, MOBA_DIM), F32),
        f((N, 3 * D_MODEL), F32),
    )
    row = lambda w_: pl.BlockSpec((tm, w_), lambda i: (i, 0))
    const = lambda shape: pl.BlockSpec(shape, lambda i: (0,) * len(shape))
    in_specs = [
        row(D_MODEL),
        const((1, D_MODEL)),
        const((D_MODEL, C_TOTAL)),
        const((1, C_TOTAL)),
        pl.BlockSpec((tm, 128), lambda i: (i % nt, 0)),
        pl.BlockSpec((tm, 128), lambda i: (i % nt, 0)),
    ]
    out_specs = (
        row(256),
        row(128),
        row(128),
        pl.BlockSpec((None, NSA_DIM, tm), lambda i: (i // nt, 0, i % nt)),
        row(128),
        pl.BlockSpec((None, tm // KV_CHUNK, 128, KV_CHUNK), lambda i: (i // nt, i % nt, 0, 0)),
        row(128),
        pl.BlockSpec((None, tm // 128, 128, 128), lambda i: (i // nt, i % nt, 0, 0)),
        pl.BlockSpec((None, 128, tm), lambda i: (i // nt, 0, i % nt)),
        pl.BlockSpec((None, MOBA_DIM, tm), lambda i: (i // nt, 0, i % nt)),
        row(MOBA_DIM),
        pl.BlockSpec((None, tm // KV_CHUNK, MOBA_DIM, KV_CHUNK), lambda i: (i // nt, i % nt, 0, 0)),
        pl.BlockSpec((None, None, tm // MOBA_BLOCK, MOBA_DIM), lambda i: (i // nt, i % nt, 0, 0)),
        row(3 * D_MODEL),
    )
    return pl.pallas_call(
        _proj_kernel,
        out_shape=out_shape,
        grid=(N // tm,),
        in_specs=in_specs,
        out_specs=out_specs,
        compiler_params=pltpu.CompilerParams(
            dimension_semantics=("arbitrary",), vmem_limit_bytes=VMEM_LIMIT),
        name="proj",
    )(x2, gamma, w, bias, cos_t, sin_t)


def _gelu_tanh(x):
    return x * (0.5 * (1.0 + jnp.tanh(np.sqrt(2.0 / np.pi).astype(np.float32) * (x + 0.044715 * (x * x * x)))))


def _compress_kernel(xk_ref, xv_ref, pos_ref, w1_ref, b1_ref, w2_ref, b2_ref, kc_ref, vct_ref):
    R = xk_ref.shape[0] // CMP_STRIDE
    top = [None] * 4
    bot = [None] * 4
    for l in range(CMP_STRIDE):
        xl = [r[pl.ds(l, R, stride=CMP_STRIDE), :] for r in (xk_ref, xv_ref)]
        for j in range(4):
            t, g = divmod(j, NSA_KV_GROUPS)
            piece = xl[t][:, HEAD_DIM * g:HEAD_DIM * (g + 1)]
            lo = (piece + pos_ref[t, l:l + 1, :]).astype(BF16)
            hi = (piece + pos_ref[t, CMP_STRIDE + l:CMP_STRIDE + l + 1, :]).astype(BF16)
            a = _dot(lo, w1_ref[t, HEAD_DIM * l:HEAD_DIM * (l + 1), :])
            b = _dot(hi, w1_ref[t, HEAD_DIM * (CMP_STRIDE + l):HEAD_DIM * (CMP_STRIDE + l + 1), :])
            top[j] = a if top[j] is None else top[j] + a
            bot[j] = b if bot[j] is None else bot[j] + b
    outs = []
    for j in range(4):
        t = j // NSA_KV_GROUPS
        hid = _gelu_tanh(top[j] + pltpu.roll(bot[j], R - 1, 0) + b1_ref[t])
        outs.append(_dot(hid.astype(BF16), w2_ref[t]) + b2_ref[t])
    kc_ref[...] = jnp.concatenate(outs[0:2], axis=1).astype(BF16)
    vct_ref[...] = jnp.concatenate(outs[2:4], axis=1).T.astype(BF16)


def _compress_call(xk, xv, pos, w1, b1, w2, b2, B, R):
    S = R * CMP_STRIDE
    return pl.pallas_call(
        _compress_kernel,
        out_shape=(jax.ShapeDtypeStruct((B, R, 128), BF16), jax.ShapeDtypeStruct((B, 128, R), BF16)),
        grid=(B,),
        in_specs=[
            pl.BlockSpec((S, 128), lambda b: (b, 0)),
            pl.BlockSpec((S, 128), lambda b: (b, 0)),
            pl.BlockSpec((2, CMP_BLOCK, HEAD_DIM), lambda b: (0, 0, 0)),
            pl.BlockSpec((2, CMP_BLOCK * HEAD_DIM, CMP_HIDDEN), lambda b: (0, 0, 0)),
            pl.BlockSpec((2, 1, CMP_HIDDEN), lambda b: (0, 0, 0)),
            pl.BlockSpec((2, CMP_HIDDEN, HEAD_DIM), lambda b: (0, 0, 0)),
            pl.BlockSpec((2, 1, HEAD_DIM), lambda b: (0, 0, 0)),
        ],
        out_specs=(pl.BlockSpec((None, R, 128), lambda b: (b, 0, 0)),
                   pl.BlockSpec((None, 128, R), lambda b: (b, 0, 0))),
        compiler_params=pltpu.CompilerParams(
            dimension_semantics=("arbitrary",), vmem_limit_bytes=VMEM_LIMIT),
        name="nsa_compress",
    )(xk, xv, pos, w1, b1, w2, b2)


def _softmax_block(s):
    m = jnp.max(s, axis=0, keepdims=True)
    m_use = jnp.where(m < 0.5 * NEG_INF, 0.0, m)
    p = jnp.exp2(s - m_use)
    return p, jnp.sum(p, axis=0, keepdims=True)


def _flash_update(sc, rows, block, vt, m, acc_ref):
    n, L = sc.shape[0] // block, sc.shape[1]
    parts = [sc[block * j:block * (j + 1), :] for j in range(n)]
    tops = [jnp.max(parts[j].reshape(block // 8, 8, L), axis=0) for j in range(n)]
    if rows is not None:
        tops = [tops[j] + rows[j:j + 1, :] for j in range(n)]
    top = tops[0]
    for t in tops[1:]:
        top = jnp.maximum(top, t)
    m_new = jnp.maximum(m, jnp.max(top, axis=0, keepdims=True))
    m_use = jnp.where(m_new < 0.5 * NEG_INF, 0.0, m_new)
    alpha = jnp.exp2(m - m_use)
    if rows is None:
        p = jnp.exp2(sc - m_use)
    else:
        p = jnp.concatenate([jnp.exp2(parts[j] + (rows[j:j + 1, :] - m_use)) for j in range(n)], axis=0)
    acc_ref[...] = alpha * acc_ref[...] + _dot(vt, p.astype(BF16))
    return m_new


def _run_steps(steps, pending, tail):
    pending = list(pending)
    fns = [s for s, _ in steps] + list(tail)
    for t, (_, consume) in enumerate(steps):
        s = pending.pop(0)
        if t + LOOKAHEAD < len(fns):
            pending.append(fns[t + LOOKAHEAD]())
        consume(s)
    return tuple(pending)


def _chunk_loop(body, n, carry, unroll):
    def multi(i, cr):
        for j in range(unroll):
            cr = body(unroll * i + j, cr)
        return cr

    carry = lax.fori_loop(0, n // unroll, multi, carry)
    return lax.fori_loop(unroll * (n // unroll), n, body, carry)


def _mask_bias(ok, reps):
    return jnp.concatenate([jnp.where(ok, 0.0, NEG_INF)] * reps, axis=1)


def _topk_mask(scores, rows_f, k):
    work = scores
    for _ in range(k):
        cm = jnp.max(work, axis=0, keepdims=True)
        idx = jnp.min(jnp.where(work == cm, rows_f, 1e9), axis=0, keepdims=True)
        work = jnp.where(rows_f == idx, REMOVED, work)
    return jnp.where(work == REMOVED, 1.0, 0.0)


def _add_block_bias(sc, rows, block):
    n = sc.shape[0] // block
    return jnp.concatenate([sc[block * j:block * (j + 1), :] + rows[j:j + 1, :] for j in range(n)], axis=0)


def _nsa_kernel(qt_ref, gt_ref, kc_ref, vct_ref, ksel_ref, vselt_ref, kwin_ref, vwint_ref, out_ref,
                qt_s, pg_s, bias_s, acc_s, accw_s, ocmp_s):
    qi = pl.program_id(1)
    q0 = qi * Q_TILE
    G = NSA_KV_GROUPS
    HPL = LANES // Q_TILE
    PARTS = NSA_HPG // HPL
    NLG = G * PARTS
    n_cmp_rows = kc_ref.shape[0]
    n_sel = bias_s.shape[1]
    group_of = lambda lg: lg // PARTS
    rows_of = lambda g: slice(HEAD_DIM * g, HEAD_DIM * (g + 1))

    qt_s[...] = jnp.zeros_like(qt_s)
    for h in range(NSA_HEADS):
        g, hh = divmod(h, NSA_HPG)
        part, hl = divmod(hh, HPL)
        qt_s[g * PARTS + part, rows_of(g), Q_TILE * hl:Q_TILE * (hl + 1)] = qt_ref[HEAD_DIM * h:HEAD_DIM * (h + 1), :]
    qts = [qt_s[lg] for lg in range(NLG)]
    tq1 = q0 + lax.broadcasted_iota(jnp.int32, (1, Q_TILE), 1)

    def ones_under(vt):
        return jnp.concatenate([vt, jnp.ones((ONES_ROWS, vt.shape[1]), BF16)], axis=0)

    def cmp_branch(nr):
        n_idx = lax.broadcasted_iota(jnp.int32, (nr, 1), 0)
        cbias = _mask_bias(n_idx * CMP_STRIDE + (CMP_BLOCK - 1) <= tq1, HPL)
        for g in range(G):
            pg = None
            for part in range(PARTS):
                lg = g * PARTS + part
                s = jnp.concatenate([_dot(kc_ref[r:min(r + KV_SUB, nr), :], qts[lg])
                                     for r in range(0, nr, KV_SUB)], axis=0) + cbias
                p, l = _softmax_block(s)
                den = jnp.maximum(l, 1e-30)
                ocmp_s[lg] = _dot(vct_ref[rows_of(g), 0:nr], p.astype(BF16)) / den
                pn = p / den
                for hl in range(HPL):
                    ph = pn[:, hl * Q_TILE:(hl + 1) * Q_TILE]
                    pg = ph if pg is None else pg + ph
            for sl in range(Q_TILE // 128):
                slab = g * (Q_TILE // 128) + sl
                pg_s[slab, 0:8, :] = jnp.zeros((8, 128), F32)
                pg_s[slab, 8:8 + nr, :] = pg[:, 128 * sl:128 * (sl + 1)]
                if nr < n_cmp_rows:
                    pg_s[slab, 8 + nr:8 + n_cmp_rows, :] = jnp.zeros((n_cmp_rows - nr, 128), F32)

    n_visible = (q0 + Q_TILE - CMP_BLOCK) // CMP_STRIDE + 1
    sizes = list(range(128, n_cmp_rows + 1, 128))
    for i, nr in enumerate(sizes):
        lo_ok = n_visible > sizes[i - 1] if i > 0 else True
        hi_ok = n_visible <= nr if i + 1 < len(sizes) else True
        pl.when(jnp.logical_and(lo_ok, hi_ok))(functools.partial(cmp_branch, nr))

    W = WINDOW + Q_TILE
    wstart = jnp.maximum(q0 - WINDOW, 0)
    wc0 = wstart // 128
    kposw = wstart + lax.broadcasted_iota(jnp.int32, (W, 1), 0)
    wbias = _mask_bias((kposw <= tq1) & (kposw > tq1 - WINDOW), HPL)
    vtw = jnp.concatenate([vwint_ref[wc0 + j] for j in range(W // 128)], axis=1)
    accw_s[...] = jnp.zeros_like(accw_s)
    mw = [jnp.full((1, LANES), NEG_INF, F32)] * NLG
    win_steps = []
    for r0 in range(0, W, KV_SUB):
        r1 = min(r0 + KV_SUB, W)
        for lg in range(NLG):
            def score(r0=r0, r1=r1, lg=lg):
                k = kwin_ref[pl.ds(pl.multiple_of(wstart + r0, 128), r1 - r0), :]
                return _dot(k, qts[lg]) + wbias[r0:r1, :]

            def consume(s, r0=r0, r1=r1, lg=lg):
                vt = ones_under(vtw[rows_of(group_of(lg)), r0:r1])
                mw[lg] = _flash_update(s, None, r1 - r0, vt, mw[lg], accw_s.at[lg])

            win_steps.append((score, consume))
    _run_steps(win_steps, [fn() for fn, _ in win_steps[:LOOKAHEAD]], [])

    blk_i = lax.broadcasted_iota(jnp.int32, (n_sel, 1), 0)
    cur = tq1 // SEL_BLOCK
    forced = (blk_i == 0) | (blk_i == cur) | (blk_i == cur - 1)
    started = blk_i <= cur
    n_top = min(SEL_COUNT, n_sel)
    slc = []
    for g in range(G):
        slabs = []
        for sl in range(Q_TILE // 128):
            def tap(w):
                return pg_s[g * (Q_TILE // 128) + sl, pl.ds(8 + w, n_sel, stride=SEL_BLOCK // CMP_STRIDE), :]

            t = tap(-1) + 2.0 * tap(0)
            t = t + 2.0 * tap(1)
            t = t + 2.0 * tap(2)
            t = t + tap(3)
            slabs.append(t)
        t = slabs[0] if len(slabs) == 1 else jnp.concatenate(slabs, axis=1)
        slc.append(jnp.where(forced, NEG_INF, jnp.where(started, t, NEG_INF)))
    free = _topk_mask(jnp.concatenate(slc, axis=1), jnp.broadcast_to(blk_i.astype(F32), (n_sel, G * Q_TILE)),
                      n_top - N_FORCED)
    for g in range(G):
        bias = jnp.where(forced, 0.0, jnp.where(free[:, Q_TILE * g:Q_TILE * (g + 1)] > 0.5, 0.0, NEG_INF))
        bias_s[g] = jnp.concatenate([bias] * HPL, axis=1)

    bpc = KV_CHUNK // SEL_BLOCK
    bps = KV_SUB // SEL_BLOCK
    c_last = q0 // KV_CHUNK

    def chunk_steps(c, ms, causal):
        steps = []
        for sub in range(KV_CHUNK // KV_SUB):
            for lg in range(NLG):
                def score(sub=sub, lg=lg):
                    base = pl.multiple_of(c * KV_CHUNK, KV_CHUNK) + sub * KV_SUB
                    s = _dot(ksel_ref[pl.ds(pl.multiple_of(base, KV_SUB), KV_SUB), :], qts[lg])
                    return s if causal is None else s + causal[sub]

                def consume(s, sub=sub, lg=lg):
                    g = group_of(lg)
                    rows = bias_s[g, pl.ds(pl.multiple_of(c * bpc, bpc), bpc), :][bps * sub:bps * (sub + 1), :]
                    vt = vselt_ref[c, rows_of(g), KV_SUB * sub:KV_SUB * (sub + 1)]
                    ms[lg] = _flash_update(s, rows, SEL_BLOCK, ones_under(vt), ms[lg], acc_s.at[lg])

                steps.append((score, consume))
        return steps

    acc_s[...] = jnp.zeros_like(acc_s)

    def body(c, carry):
        ms = list(carry[:NLG])
        nxt = [fn for fn, _ in chunk_steps(c + 1, ms, None)[:LOOKAHEAD]]
        pending = _run_steps(chunk_steps(c, ms, None), carry[NLG:], nxt)
        return tuple(ms) + pending

    first = [fn() for fn, _ in chunk_steps(0, None, None)[:LOOKAHEAD]]
    carry = (jnp.full((1, LANES), NEG_INF, F32),) * NLG + tuple(first)
    carry = _chunk_loop(body, c_last, carry, NSA_TRIP_UNROLL)
    kpos = c_last * KV_CHUNK + lax.broadcasted_iota(jnp.int32, (KV_CHUNK, 1), 0)
    causal = _mask_bias(kpos <= tq1, HPL)
    n_sub = KV_CHUNK // KV_SUB
    needed = ((q0 % KV_CHUNK) + Q_TILE + KV_SUB - 1) // KV_SUB

    def run_diag(n_pieces):
        steps = chunk_steps(c_last, list(carry[:NLG]), [causal[KV_SUB * sub:KV_SUB * (sub + 1), :]
                                                       for sub in range(n_sub)])[:n_pieces * NLG]
        _run_steps(steps, [fn() for fn, _ in steps[:LOOKAHEAD]], [])

    for n_pieces in range(1, n_sub + 1):
        cond = needed == n_pieces if n_pieces < n_sub else needed >= n_pieces
        pl.when(cond)(functools.partial(run_diag, n_pieces))

    gates = jax.nn.sigmoid(gt_ref[0:3 * NSA_HEADS, :])
    outs = []
    for h in range(NSA_HEADS):
        g, hh = divmod(h, NSA_HPG)
        part, hl = divmod(hh, HPL)
        lg = g * PARTS + part
        cols = slice(Q_TILE * hl, Q_TILE * (hl + 1))
        o_sel = acc_s[lg, 0:HEAD_DIM, cols] / jnp.maximum(acc_s[lg, HEAD_DIM:HEAD_DIM + 1, cols], 1e-30)
        o_win = accw_s[lg, 0:HEAD_DIM, cols] / jnp.maximum(accw_s[lg, HEAD_DIM:HEAD_DIM + 1, cols], 1e-30)
        o = gates[3 * h:3 * h + 1, :] * ocmp_s[lg, :, cols]
        o = o + gates[3 * h + 1:3 * h + 2, :] * o_sel
        o = o + gates[3 * h + 2:3 * h + 3, :] * o_win
        outs.append(o)
    out_ref[...] = jnp.concatenate(outs, axis=0).T.astype(BF16)


def _nsa_call(qtn, gt, kc, vct, ksel, vselt, kwin, vwint, B, S):
    R = S // CMP_STRIDE
    n_sel = S // SEL_BLOCK
    nlg = NSA_HEADS * Q_TILE // LANES
    return pl.pallas_call(
        _nsa_kernel,
        out_shape=jax.ShapeDtypeStruct((B * S, NSA_DIM), BF16),
        grid=(B, S // Q_TILE),
        in_specs=[
            pl.BlockSpec((None, NSA_DIM, Q_TILE), lambda b, q: (b, 0, q)),
            pl.BlockSpec((None, 128, Q_TILE), lambda b, q: (b, 0, q)),
            pl.BlockSpec((None, R, 128), lambda b, q: (b, 0, 0)),
            pl.BlockSpec((None, 128, R), lambda b, q: (b, 0, 0)),
            pl.BlockSpec((S, 128), lambda b, q: (b, 0)),
            pl.BlockSpec((None, S // KV_CHUNK, 128, KV_CHUNK), lambda b, q: (b, 0, 0, 0)),
            pl.BlockSpec((S, 128), lambda b, q: (b, 0)),
            pl.BlockSpec((None, S // 128, 128, 128), lambda b, q: (b, 0, 0, 0)),
        ],
        out_specs=pl.BlockSpec((Q_TILE, NSA_DIM), lambda b, q: (b * (S // Q_TILE) + q, 0)),
        scratch_shapes=[
            pltpu.VMEM((nlg, 128, LANES), BF16),
            pltpu.VMEM((NSA_KV_GROUPS * (Q_TILE // 128), 8 + R, 128), F32),
            pltpu.VMEM((NSA_KV_GROUPS, n_sel, LANES), F32),
            pltpu.VMEM((nlg, HEAD_DIM + ONES_ROWS, LANES), F32),
            pltpu.VMEM((nlg, HEAD_DIM + ONES_ROWS, LANES), F32),
            pltpu.VMEM((nlg, HEAD_DIM, LANES), F32),
        ],
        compiler_params=pltpu.CompilerParams(
            dimension_semantics=("arbitrary", "arbitrary"), vmem_limit_bytes=VMEM_LIMIT),
        name="nsa_attn",
    )(qtn, gt, kc, vct, ksel, vselt, kwin, vwint)


def _moba_kernel(qt_ref, kmean_ref, k_ref, vt_ref, out_ref, qt_s, bias_s, acc_s):
    qi = pl.program_id(1)
    q0 = qi * MOBA_Q_TILE
    L = MOBA_Q_TILE
    H = MOBA_HEADS
    n_blk = kmean_ref.shape[0]
    n_top = max(1, min(MOBA_TOPK, n_blk - 1))

    qt_s[...] = jnp.zeros_like(qt_s)
    for h in range(H):
        rows = slice(HEAD_DIM * h, HEAD_DIM * (h + 1))
        qt_s[h, rows, :] = qt_ref[rows, :]
    qts = [qt_s[h] for h in range(H)]
    tq = q0 + lax.broadcasted_iota(jnp.int32, (1, L), 1)

    own = tq // MOBA_BLOCK
    blk_i = lax.broadcasted_iota(jnp.int32, (n_blk, 1), 0)
    blk_f = jnp.broadcast_to(blk_i.astype(F32), (n_blk, L))
    kmean = kmean_ref[...].astype(BF16)
    for h in range(H):
        sg = jnp.where(blk_i < own, _dot(kmean, qts[h]), NEG_INF)
        sel = _topk_mask(sg, blk_f, n_top)
        allow = ((sel > 0.5) & (blk_i < own)) | (blk_i == own)
        bias = jnp.where(allow, 0.0, NEG_INF)
        for j in range(n_blk):
            bias_s[h, j] = jnp.broadcast_to(bias[j:j + 1, :], (8, L))

    bpc = KV_CHUNK // MOBA_BLOCK

    steps = [(j, h) for j in range(bpc) for h in range(H)]

    ones = jnp.ones((ONES_ROWS, MOBA_BLOCK), BF16)

    def scores(c, j, h, diagonal):
        base = pl.multiple_of(c * KV_CHUNK, KV_CHUNK) + j * MOBA_BLOCK
        k = k_ref[pl.ds(pl.multiple_of(base, MOBA_BLOCK), MOBA_BLOCK), :]
        s = _dot(k, qts[h])
        if diagonal:
            kpos = base + lax.broadcasted_iota(jnp.int32, (MOBA_BLOCK, 1), 0)
            s = jnp.where(kpos <= tq, s, NEG_INF)
        return s

    def chunk_steps(c, ms, pending, diagonal):
        ms, pending = list(ms), list(pending)
        for t, (j, h) in enumerate(steps):
            s = pending.pop(0)
            u = t + LOOKAHEAD
            if u < len(steps):
                pending.append(scores(c, *steps[u], diagonal))
            elif not diagonal:
                pending.append(scores(c + 1, *steps[u - len(steps)], False))
            vt = vt_ref[c, HEAD_DIM * h:HEAD_DIM * (h + 1), MOBA_BLOCK * j:MOBA_BLOCK * (j + 1)]
            ms[h] = _flash_update(s, bias_s[h, c * bpc + j], MOBA_BLOCK, jnp.concatenate([vt, ones], axis=0),
                                  ms[h], acc_s.at[h])
        return tuple(ms), tuple(pending)

    acc_s[...] = jnp.zeros_like(acc_s)
    c_last = q0 // KV_CHUNK
    init = (jnp.full((1, L), NEG_INF, F32),) * H

    def body(c, carry):
        ms, pending = chunk_steps(c, carry[:H], carry[H:], False)
        return ms + pending

    carry = init + tuple(scores(0, *steps[i], False) for i in range(LOOKAHEAD))
    carry = _chunk_loop(body, c_last, carry, MOBA_TRIP_UNROLL)
    chunk_steps(c_last, carry[:H], [scores(c_last, *steps[i], True) for i in range(LOOKAHEAD)], True)
    outs = [acc_s[h, 0:HEAD_DIM, :] / jnp.maximum(acc_s[h, HEAD_DIM:HEAD_DIM + 1, :], 1e-30) for h in range(H)]
    out_ref[...] = jnp.concatenate(outs, axis=0).T.astype(BF16)


def _moba_call(qtm, kmean, km, vtm, B, S):
    n_blk = S // MOBA_BLOCK
    L = MOBA_Q_TILE
    return pl.pallas_call(
        _moba_kernel,
        out_shape=jax.ShapeDtypeStruct((B * S, MOBA_DIM), BF16),
        grid=(B, S // L),
        in_specs=[
            pl.BlockSpec((None, MOBA_DIM, L), lambda b, q: (b, 0, q)),
            pl.BlockSpec((None, n_blk, MOBA_DIM), lambda b, q: (b, 0, 0)),
            pl.BlockSpec((S, MOBA_DIM), lambda b, q: (b, 0)),
            pl.BlockSpec((None, S // KV_CHUNK, MOBA_DIM, KV_CHUNK), lambda b, q: (b, 0, 0, 0)),
        ],
        out_specs=pl.BlockSpec((L, MOBA_DIM), lambda b, q: (b * (S // L) + q, 0)),
        scratch_shapes=[
            pltpu.VMEM((MOBA_HEADS, MOBA_DIM, L), BF16),
            pltpu.VMEM((MOBA_HEADS, n_blk, 8, L), F32),
            pltpu.VMEM((MOBA_HEADS, HEAD_DIM + ONES_ROWS, L), F32),
        ],
        compiler_params=pltpu.CompilerParams(
            dimension_semantics=("arbitrary", "arbitrary"), vmem_limit_bytes=VMEM_LIMIT),
        name="moba_attn",
    )(qtm, kmean, km, vtm)


def _merge_kernel(u_ref, halo_ref, b_ref, c_ref, gbr_ref, x_ref, pw_ref, ps_ref, wa_ref, wb_ref, wc_ref,
                  wo_ref, gam_ref, x1_ref, h2_ref, ext_s, *, tiles_per_seq):
    tm = u_ref.shape[0]
    i = pl.program_id(0)
    first = (i % tiles_per_seq) == 0
    u = u_ref[...]
    ext_s[0:POOL_MAXW, :] = jnp.where(first, 0.0, halo_ref[...])
    ext_s[POOL_MAXW:POOL_MAXW + tm, :] = u

    def tail_sum(col, k0, k1):
        tot = None
        for k in range(k0, k1):
            v = ext_s[pl.ds(POOL_MAXW - k, tm), 128 * col:128 * col + 128]
            tot = v if tot is None else tot + v
        return tot

    t_glob = (i % tiles_per_seq) * tm + lax.broadcasted_iota(jnp.int32, (tm, 1), 0)
    lane = lax.broadcasted_iota(jnp.int32, (1, 128), 1)
    low = lane < POOL_GROUP_DIM
    pooled = []
    for col in range(2):
        wa_, wb_ = POOL_WINDOWS[2 * col], POOL_WINDOWS[2 * col + 1]
        sa = tail_sum(col, 0, wa_)
        sb = sa + tail_sum(col, wa_, wb_)
        ca = jnp.minimum(t_glob + 1, wa_).astype(F32)
        cb = jnp.minimum(t_glob + 1, wb_).astype(F32)
        pooled.append(jnp.where(low, sa, sb) / jnp.where(low, ca, cb))
    d = jnp.concatenate(pooled, axis=1) - u
    a = _dot(d.astype(BF16), pw_ref[...]) * ps_ref[...]

    av = _dot(a.astype(BF16), wa_ref[...])
    bv = _dot(b_ref[...], wb_ref[...])
    cv = _dot(c_ref[...], wc_ref[...])
    merged = jax.nn.sigmoid(gbr_ref[:, 0:D_MODEL]) * av
    merged = merged + jax.nn.sigmoid(gbr_ref[:, D_MODEL:2 * D_MODEL]) * bv
    merged = merged + jax.nn.sigmoid(gbr_ref[:, 2 * D_MODEL:3 * D_MODEL]) * cv
    x1 = x_ref[...] + _dot(merged.astype(BF16), wo_ref[...])
    x1_ref[...] = x1
    h2_ref[...] = _rms(x1, gam_ref[...]).astype(BF16)


def _merge_call(upool, bn, cm, gbr, x2, pw, ps, wa, wb, wc, wo, gamma, B, S):
    N = B * S
    tm = ROW_TILE
    nt = S // tm
    hb = tm // POOL_MAXW
    row = lambda w_: pl.BlockSpec((tm, w_), lambda i: (i, 0))
    const = lambda shape: pl.BlockSpec(shape, lambda i: (0,) * len(shape))
    return pl.pallas_call(
        functools.partial(_merge_kernel, tiles_per_seq=nt),
        out_shape=(jax.ShapeDtypeStruct((N, D_MODEL), F32), jax.ShapeDtypeStruct((N, D_MODEL), BF16)),
        grid=(N // tm,),
        in_specs=[
            row(POOL_DIM),
            pl.BlockSpec((POOL_MAXW, POOL_DIM), lambda i: (jnp.maximum(i * hb - 1, 0), 0)),
            row(NSA_DIM),
            row(MOBA_DIM),
            row(3 * D_MODEL),
            row(D_MODEL),
            const((POOL_DIM, POOL_DIM)),
            const((1, POOL_DIM)),
            const((POOL_DIM, D_MODEL)),
            const((NSA_DIM, D_MODEL)),
            const((MOBA_DIM, D_MODEL)),
            const((D_MODEL, D_MODEL)),
            const((1, D_MODEL)),
        ],
        out_specs=(row(D_MODEL), row(D_MODEL)),
        scratch_shapes=[pltpu.VMEM((POOL_MAXW + tm, POOL_DIM), F32)],
        compiler_params=pltpu.CompilerParams(
            dimension_semantics=("arbitrary",), vmem_limit_bytes=VMEM_LIMIT),
        name="merge",
    )(upool, upool, bn, cm, gbr, x2, pw, ps, wa, wb, wc, wo, gamma)


def _ffn_kernel(h_ref, x_ref, wg_ref, wu_ref, wd_ref, gam_ref, out_ref, acc_s, *, final_norm):
    f = pl.program_id(1)

    @pl.when(f == 0)
    def _():
        acc_s[...] = jnp.zeros_like(acc_s)

    h = h_ref[...]
    g = _dot(h, wg_ref[...])
    u = _dot(h, wu_ref[...])
    act = (g * jax.nn.sigmoid(g)) * u
    acc_s[...] += _dot(act.astype(BF16), wd_ref[...])

    @pl.when(f == pl.num_programs(1) - 1)
    def _():
        y = x_ref[...] + acc_s[...]
        if final_norm:
            y = _rms(y, gam_ref[...])
        out_ref[...] = y


def _ffn_call(h2, x1, wg, wu, wd, gamma, final_norm):
    N = x1.shape[0]
    F = wg.shape[1]
    tm, tf = FFN_ROW_TILE, FFN_COL_TILE
    return pl.pallas_call(
        functools.partial(_ffn_kernel, final_norm=final_norm),
        out_shape=jax.ShapeDtypeStruct((N, D_MODEL), F32),
        grid=(N // tm, F // tf),
        in_specs=[
            pl.BlockSpec((tm, D_MODEL), lambda i, f: (i, 0)),
            pl.BlockSpec((tm, D_MODEL), lambda i, f: (i, 0)),
            pl.BlockSpec((D_MODEL, tf), lambda i, f: (0, f)),
            pl.BlockSpec((D_MODEL, tf), lambda i, f: (0, f)),
            pl.BlockSpec((tf, D_MODEL), lambda i, f: (f, 0)),
            pl.BlockSpec((1, D_MODEL), lambda i, f: (0, 0)),
        ],
        out_specs=pl.BlockSpec((tm, D_MODEL), lambda i, f: (i, 0)),
        scratch_shapes=[pltpu.VMEM((tm, D_MODEL), F32)],
        compiler_params=pltpu.CompilerParams(
            dimension_semantics=("arbitrary", "arbitrary"), vmem_limit_bytes=VMEM_LIMIT),
        name="ffn",
    )(h2, x1, wg, wu, wd, gamma)


def _reorder_in_proj(w, b):
    o_qn = POOL_DIM
    o_kv = o_qn + NSA_DIM
    o_gn = o_kv + 6 * 128
    o_mo = o_gn + 3 * NSA_HEADS
    o_gb = o_mo + 3 * MOBA_DIM
    pad = 128 - 3 * NSA_HEADS

    def pick(a):
        parts = [a[..., 0:o_qn], a[..., o_kv:o_kv + 256], a[..., o_qn:o_kv], a[..., o_kv + 256:o_gn],
                 a[..., o_gn:o_mo], jnp.zeros(a.shape[:-1] + (pad,), a.dtype), a[..., o_mo:o_gb], a[..., o_gb:]]
        return jnp.concatenate(parts, axis=-1)

    return pick(w), pick(b)


def _rope_tables(S):
    pos = jnp.arange(S, dtype=F32)
    inv_freq = ROPE_THETA ** (-jnp.arange(0, HEAD_DIM, 2, dtype=F32) / HEAD_DIM)
    ang = pos[:, None] * inv_freq[None, :]
    cos, sin = jnp.cos(ang), jnp.sin(ang)
    cos_t = jnp.tile(cos, (1, 4))
    sin_t = jnp.tile(jnp.concatenate([-sin, sin], axis=1), (1, 2))
    return cos_t, sin_t


def kernel(x, attn_norm, w_in, b_in, pool_w, pool_scale, cmp_pos, cmp_w1, cmp_b1, cmp_w2, cmp_b2,
           w_br_pool, w_br_nsa, w_br_moba, w_out, ffn_norm, w_gate, w_up, w_down, final_norm):
    B, S, D = x.shape
    depth = w_in.shape[0]
    assert D == D_MODEL and S % KV_CHUNK == 0 and S % ROW_TILE == 0 and S >= WINDOW + Q_TILE
    assert (B * S) % FFN_ROW_TILE == 0 and w_gate.shape[2] % FFN_COL_TILE == 0
    N = B * S
    R = S // CMP_STRIDE
    cos_t, sin_t = _rope_tables(S)
    x2 = x.reshape(N, D)
    for l in range(depth):
        w_all, b_all = _reorder_in_proj(w_in[l].astype(BF16), b_in[l])
        (upool, cmpk, cmpv, qtn, ksel, vselt, kwin, vwint, gt, qtm, km, vtm, kmean, gbr) = _proj_call(
            x2, attn_norm[l][None, :], w_all, b_all[None, :], cos_t, sin_t, B, S)

        kc, vct = _compress_call(
            cmpk, cmpv, cmp_pos[l], cmp_w1[l].astype(BF16),
            cmp_b1[l][:, None, :], cmp_w2[l].astype(BF16), cmp_b2[l][:, None, :], B, R)

        bn = _nsa_call(qtn, gt, kc, vct, ksel, vselt, kwin, vwint, B, S)
        cm = _moba_call(qtm, kmean.reshape(B, S // MOBA_BLOCK, MOBA_DIM), km, vtm, B, S)

        pw_bd = jax.scipy.linalg.block_diag(*[pool_w[l, g] for g in range(len(POOL_WINDOWS))])
        x1, h2 = _merge_call(
            upool, bn, cm, gbr, x2, pw_bd.astype(BF16), pool_scale[l][None, :],
            w_br_pool[l].astype(BF16), w_br_nsa[l].astype(BF16), w_br_moba[l].astype(BF16),
            w_out[l].astype(BF16), ffn_norm[l][None, :], B, S)

        x2 = _ffn_call(h2, x1, w_gate[l].astype(BF16), w_up[l].astype(BF16), w_down[l].astype(BF16),
                       final_norm[None, :], final_norm=(l == depth - 1))
    return x2.reshape(B, S, D)
```

```python
import functools

import numpy as np
import jax
import jax.numpy as jnp
from jax import lax
from jax.experimental import pallas as pl
from jax.experimental.pallas import tpu as pltpu

F32 = jnp.float32
BF16 = jnp.bfloat16

D_MODEL = 1024
HEAD_DIM = 64
ROPE_THETA = 10000.0
RMS_EPS = 1e-6
NEG_INF = -1e30
LOG2E = 1.4426950408889634
LOOKAHEAD = 2
NSA_TRIP_UNROLL = 4
MOBA_TRIP_UNROLL = 4
ONES_ROWS = 16
REMOVED = -(2.0 ** 127)

POOL_WINDOWS = (2, 4, 8, 16)
POOL_GROUP_DIM = 64
POOL_DIM = 256
POOL_MAXW = 16

NSA_HEADS = 8
NSA_KV_GROUPS = 2
NSA_HPG = 4
NSA_DIM = 512
CMP_BLOCK = 32
CMP_STRIDE = 16
CMP_HIDDEN = 256
SEL_BLOCK = 64
SEL_COUNT = 16
N_FORCED = 3
WINDOW = 512

MOBA_HEADS = 4
MOBA_DIM = 256
MOBA_BLOCK = 256
MOBA_TOPK = 3

Q_TILE = 256
LANES = 512
MOBA_Q_TILE = 512
KV_CHUNK = 512
KV_SUB = 256
ROW_TILE = 512
FFN_ROW_TILE = 1024
FFN_COL_TILE = 1408
VMEM_LIMIT = 56 * 1024 * 1024

C_POOL = 0
C_CMP = 256
C_QN = 512
C_KS = 1024
C_VS = 1152
C_KW = 1280
C_VW = 1408
C_GN = 1536
C_QM = 1664
C_KM = 1920
C_VM = 2176
C_GBR = 2432
C_TOTAL = 5504


def _dot(a, b):
    return jnp.dot(a, b, preferred_element_type=F32)


def _rms(x, gamma):
    return x * lax.rsqrt(jnp.mean(x * x, axis=-1, keepdims=True) + RMS_EPS) * gamma


def _proj_kernel(x_ref, gam_ref, w_ref, b_ref, cos_ref, sin_ref,
                 upool_ref, cmpk_ref, cmpv_ref, qtn_ref, ksel_ref, vselt_ref, kwin_ref, vwint_ref, gt_ref,
                 qtm_ref, km_ref, vtm_ref, kmean_ref, gbr_ref):
    tm = x_ref.shape[0]
    h = _rms(x_ref[...], gam_ref[...]).astype(BF16)
    cos = cos_ref[...]
    sin = sin_ref[...]
    lane = lax.broadcasted_iota(jnp.int32, (tm, 128), 1)
    first_half = (lane & 32) == 0
    scale = HEAD_DIM ** -0.5 * LOG2E

    def seg(a, b):
        return _dot(h, w_ref[:, a:b]) + b_ref[:, a:b]

    def rope(y):
        swap = jnp.where(first_half, pltpu.roll(y, 96, 1), pltpu.roll(y, 32, 1))
        return y * cos + swap * sin

    y = seg(C_POOL, C_POOL + 512)
    upool_ref[...] = y[:, :256]
    cmpk_ref[...] = rope(y[:, 256:384])
    cmpv_ref[...] = y[:, 384:512]

    y = seg(C_QN, C_QN + 512)
    for j in range(4):
        q = rope(y[:, 128 * j:128 * j + 128]) * scale
        qtn_ref[128 * j:128 * j + 128, :] = q.T.astype(BF16)

    y = seg(C_KS, C_KS + 512)
    ksel_ref[...] = rope(y[:, 0:128]).astype(BF16)
    vselt_ref[0] = y[:, 128:256].T.astype(BF16)
    kwin_ref[...] = rope(y[:, 256:384]).astype(BF16)
    vwt = y[:, 384:512].T.astype(BF16)
    for j in range(tm // 128):
        vwint_ref[j] = vwt[:, 128 * j:128 * j + 128]

    y = seg(C_GN, C_GN + 128)
    gt_ref[...] = y.T

    y = seg(C_QM, C_QM + 256)
    for j in range(2):
        q = rope(y[:, 128 * j:128 * j + 128]) * scale
        qtm_ref[128 * j:128 * j + 128, :] = q.T.astype(BF16)

    y = seg(C_KM, C_KM + 256)
    km = jnp.concatenate([rope(y[:, 0:128]), rope(y[:, 128:256])], axis=1)
    km_ref[...] = km.astype(BF16)
    nblk = tm // MOBA_BLOCK
    means = [jnp.sum(km[MOBA_BLOCK * j:MOBA_BLOCK * (j + 1), :], axis=0, keepdims=True) * (1.0 / MOBA_BLOCK)
             for j in range(nblk)]
    kmean_ref[...] = jnp.concatenate(means, axis=0)

    y = seg(C_VM, C_VM + 256)
    vtm_ref[0] = y.T.astype(BF16)

    for j in range(6):
        a = C_GBR + 512 * j
        gbr_ref[:, 512 * j:512 * j + 512] = seg(a, a + 512)


def _proj_call(x2, gamma, w, bias, cos_t, sin_t, B, S):
    N = B * S
    tm = ROW_TILE
    nt = S // tm
    f = lambda shape, dt: jax.ShapeDtypeStruct(shape, dt)
    out_shape = (
        f((N, 256), F32),
        f((N, 128), F32),
        f((N, 128), F32),
        f((B, NSA_DIM, S), BF16),
        f((N, 128), BF16),
        f((B, S // KV_CHUNK, 128, KV_CHUNK), BF16),
        f((N, 128), BF16),
        f((B, S // 128, 128, 128), BF16),
        f((B, 128, S), F32),
        f((B, MOBA_DIM, S), BF16),
        f((N, MOBA_DIM), BF16),
        f((B, S // KV_CHUNK, MOBA_DIM, KV_CHUNK), BF16),
        f((B, nt, tm // MOBA_BLOCK, MOBA_DIM), F32),
        f((N, 3 * D_MODEL), F32),
    )
    row = lambda w_: pl.BlockSpec((tm, w_), lambda i: (i, 0))
    const = lambda shape: pl.BlockSpec(shape, lambda i: (0,) * len(shape))
    in_specs = [
        row(D_MODEL),
        const((1, D_MODEL)),
        const((D_MODEL, C_TOTAL)),
        const((1, C_TOTAL)),
        pl.BlockSpec((tm, 128), lambda i: (i % nt, 0)),
        pl.BlockSpec((tm, 128), lambda i: (i % nt, 0)),
    ]
    out_specs = (
        row(256),
        row(128),
        row(128),
        pl.BlockSpec((None, NSA_DIM, tm), lambda i: (i // nt, 0, i % nt)),
        row(128),
        pl.BlockSpec((None, tm // KV_CHUNK, 128, KV_CHUNK), lambda i: (i // nt, i % nt, 0, 0)),
        row(128),
        pl.BlockSpec((None, tm // 128, 128, 128), lambda i: (i // nt, i % nt, 0, 0)),
        pl.BlockSpec((None, 128, tm), lambda i: (i // nt, 0, i % nt)),
        pl.BlockSpec((None, MOBA_DIM, tm), lambda i: (i // nt, 0, i % nt)),
        row(MOBA_DIM),
        pl.BlockSpec((None, tm // KV_CHUNK, MOBA_DIM, KV_CHUNK), lambda i: (i // nt, i % nt, 0, 0)),
        pl.BlockSpec((None, None, tm // MOBA_BLOCK, MOBA_DIM), lambda i: (i // nt, i % nt, 0, 0)),
        row(3 * D_MODEL),
    )
    return pl.pallas_call(
        _proj_kernel,
        out_shape=out_shape,
        grid=(N // tm,),
        in_specs=in_specs,
        out_specs=out_specs,
        compiler_params=pltpu.CompilerParams(
            dimension_semantics=("arbitrary",), vmem_limit_bytes=VMEM_LIMIT),
        name="proj",
    )(x2, gamma, w, bias, cos_t, sin_t)


def _gelu_tanh(x):
    return x * (0.5 * (1.0 + jnp.tanh(np.sqrt(2.0 / np.pi).astype(np.float32) * (x + 0.044715 * (x * x * x)))))


def _compress_kernel(xk_ref, xv_ref, pos_ref, w1_ref, b1_ref, w2_ref, b2_ref, kc_ref, vct_ref):
    R = xk_ref.shape[0] // CMP_STRIDE
    top = [None] * 4
    bot = [None] * 4
    for l in range(CMP_STRIDE):
        xl = [r[pl.ds(l, R, stride=CMP_STRIDE), :] for r in (xk_ref, xv_ref)]
        for j in range(4):
            t, g = divmod(j, NSA_KV_GROUPS)
            piece = xl[t][:, HEAD_DIM * g:HEAD_DIM * (g + 1)]
            lo = (piece + pos_ref[t, l:l + 1, :]).astype(BF16)
            hi = (piece + pos_ref[t, CMP_STRIDE + l:CMP_STRIDE + l + 1, :]).astype(BF16)
            a = _dot(lo, w1_ref[t, HEAD_DIM * l:HEAD_DIM * (l + 1), :])
            b = _dot(hi, w1_ref[t, HEAD_DIM * (CMP_STRIDE + l):HEAD_DIM * (CMP_STRIDE + l + 1), :])
            top[j] = a if top[j] is None else top[j] + a
            bot[j] = b if bot[j] is None else bot[j] + b
    outs = []
    for j in range(4):
        t = j // NSA_KV_GROUPS
        hid = _gelu_tanh(top[j] + pltpu.roll(bot[j], R - 1, 0) + b1_ref[t])
        outs.append(_dot(hid.astype(BF16), w2_ref[t]) + b2_ref[t])
    kc_ref[...] = jnp.concatenate(outs[0:2], axis=1).astype(BF16)
    vct_ref[...] = jnp.concatenate(outs[2:4], axis=1).T.astype(BF16)


def _compress_call(xk, xv, pos, w1, b1, w2, b2, B, R):
    S = R * CMP_STRIDE
    return pl.pallas_call(
        _compress_kernel,
        out_shape=(jax.ShapeDtypeStruct((B, R, 128), BF16), jax.ShapeDtypeStruct((B, 128, R), BF16)),
        grid=(B,),
        in_specs=[
            pl.BlockSpec((S, 128), lambda b: (b, 0)),
            pl.BlockSpec((S, 128), lambda b: (b, 0)),
            pl.BlockSpec((2, CMP_BLOCK, HEAD_DIM), lambda b: (0, 0, 0)),
            pl.BlockSpec((2, CMP_BLOCK * HEAD_DIM, CMP_HIDDEN), lambda b: (0, 0, 0)),
            pl.BlockSpec((2, 1, CMP_HIDDEN), lambda b: (0, 0, 0)),
            pl.BlockSpec((2, CMP_HIDDEN, HEAD_DIM), lambda b: (0, 0, 0)),
            pl.BlockSpec((2, 1, HEAD_DIM), lambda b: (0, 0, 0)),
        ],
        out_specs=(pl.BlockSpec((None, R, 128), lambda b: (b, 0, 0)),
                   pl.BlockSpec((None, 128, R), lambda b: (b, 0, 0))),
        compiler_params=pltpu.CompilerParams(
            dimension_semantics=("arbitrary",), vmem_limit_bytes=VMEM_LIMIT),
        name="nsa_compress",
    )(xk, xv, pos, w1, b1, w2, b2)


def _softmax_block(s):
    m = jnp.max(s, axis=0, keepdims=True)
    m_use = jnp.where(m < 0.5 * NEG_INF, 0.0, m)
    p = jnp.exp2(s - m_use)
    return p, jnp.sum(p, axis=0, keepdims=True)


def _flash_update(sc, rows, block, vt, m, acc_ref):
    n, L = sc.shape[0] // block, sc.shape[1]
    parts = [sc[block * j:block * (j + 1), :] for j in range(n)]
    tops = [jnp.max(parts[j].reshape(block // 8, 8, L), axis=0) for j in range(n)]
    if rows is not None:
        tops = [tops[j] + rows[j:j + 1, :] for j in range(n)]
    top = tops[0]
    for t in tops[1:]:
        top = jnp.maximum(top, t)
    m_new = jnp.maximum(m, jnp.max(top, axis=0, keepdims=True))
    m_use = jnp.where(m_new < 0.5 * NEG_INF, 0.0, m_new)
    alpha = jnp.exp2(m - m_use)
    if rows is None:
        p = jnp.exp2(sc - m_use)
    else:
        p = jnp.concatenate([jnp.exp2(parts[j] + (rows[j:j + 1, :] - m_use)) for j in range(n)], axis=0)
    acc_ref[...] = alpha * acc_ref[...] + _dot(vt, p.astype(BF16))
    return m_new


def _run_steps(steps, pending, tail):
    pending = list(pending)
    fns = [s for s, _ in steps] + list(tail)
    for t, (_, consume) in enumerate(steps):
        s = pending.pop(0)
        if t + LOOKAHEAD < len(fns):
            pending.append(fns[t + LOOKAHEAD]())
        consume(s)
    return tuple(pending)


def _chunk_loop(body, n, carry, unroll):
    def multi(i, cr):
        for j in range(unroll):
            cr = body(unroll * i + j, cr)
        return cr

    carry = lax.fori_loop(0, n // unroll, multi, carry)
    return lax.fori_loop(unroll * (n // unroll), n, body, carry)


def _mask_bias(ok, reps):
    return jnp.concatenate([jnp.where(ok, 0.0, NEG_INF)] * reps, axis=1)


def _topk_mask(scores, rows_f, k):
    work = scores
    for _ in range(k):
        cm = jnp.max(work, axis=0, keepdims=True)
        idx = jnp.min(jnp.where(work == cm, rows_f, 1e9), axis=0, keepdims=True)
        work = jnp.where(rows_f == idx, REMOVED, work)
    return jnp.where(work == REMOVED, 1.0, 0.0)


def _nsa_kernel(qt_ref, gt_ref, kc_ref, vct_ref, ksel_ref, vselt_ref, kwin_ref, vwint_ref, out_ref,
                qt_s, pg_s, bias_s, acc_s, accw_s, ocmp_s):
    qi = pl.program_id(1)
    q0 = qi * Q_TILE
    G = NSA_KV_GROUPS
    HPL = LANES // Q_TILE
    PARTS = NSA_HPG // HPL
    NLG = G * PARTS
    n_cmp_rows = kc_ref.shape[0]
    n_sel = bias_s.shape[1]
    group_of = lambda lg: lg // PARTS
    rows_of = lambda g: slice(HEAD_DIM * g, HEAD_DIM * (g + 1))

    qt_s[...] = jnp.zeros_like(qt_s)
    for h in range(NSA_HEADS):
        g, hh = divmod(h, NSA_HPG)
        part, hl = divmod(hh, HPL)
        qt_s[g * PARTS + part, rows_of(g), Q_TILE * hl:Q_TILE * (hl + 1)] = qt_ref[HEAD_DIM * h:HEAD_DIM * (h + 1), :]
    qts = [qt_s[lg] for lg in range(NLG)]
    tq1 = q0 + lax.broadcasted_iota(jnp.int32, (1, Q_TILE), 1)

    def ones_under(vt):
        return jnp.concatenate([vt, jnp.ones((ONES_ROWS, vt.shape[1]), BF16)], axis=0)

    def cmp_branch(nr):
        n_idx = lax.broadcasted_iota(jnp.int32, (nr, 1), 0)
        cbias = _mask_bias(n_idx * CMP_STRIDE + (CMP_BLOCK - 1) <= tq1, HPL)
        for g in range(G):
            pg = None
            for part in range(PARTS):
                lg = g * PARTS + part
                s = jnp.concatenate([_dot(kc_ref[r:min(r + KV_SUB, nr), :], qts[lg])
                                     for r in range(0, nr, KV_SUB)], axis=0) + cbias
                p, l = _softmax_block(s)
                den = jnp.maximum(l, 1e-30)
                ocmp_s[lg] = _dot(vct_ref[rows_of(g), 0:nr], p.astype(BF16)) / den
                pn = p / den
                for hl in range(HPL):
                    ph = pn[:, hl * Q_TILE:(hl + 1) * Q_TILE]
                    pg = ph if pg is None else pg + ph
            for sl in range(Q_TILE // 128):
                slab = g * (Q_TILE // 128) + sl
                pg_s[slab, 0:8, :] = jnp.zeros((8, 128), F32)
                pg_s[slab, 8:8 + nr, :] = pg[:, 128 * sl:128 * (sl + 1)]
                if nr < n_cmp_rows:
                    pg_s[slab, 8 + nr:8 + n_cmp_rows, :] = jnp.zeros((n_cmp_rows - nr, 128), F32)

    n_visible = (q0 + Q_TILE - CMP_BLOCK) // CMP_STRIDE + 1
    sizes = list(range(128, n_cmp_rows + 1, 128))
    for i, nr in enumerate(sizes):
        lo_ok = n_visible > sizes[i - 1] if i > 0 else True
        hi_ok = n_visible <= nr if i + 1 < len(sizes) else True
        pl.when(jnp.logical_and(lo_ok, hi_ok))(functools.partial(cmp_branch, nr))

    W = WINDOW + Q_TILE
    wstart = jnp.maximum(q0 - WINDOW, 0)
    wc0 = wstart // 128
    kposw = wstart + lax.broadcasted_iota(jnp.int32, (W, 1), 0)
    wbias = _mask_bias((kposw <= tq1) & (kposw > tq1 - WINDOW), HPL)
    vtw = jnp.concatenate([vwint_ref[wc0 + j] for j in range(W // 128)], axis=1)
    accw_s[...] = jnp.zeros_like(accw_s)
    mw = [jnp.full((1, LANES), NEG_INF, F32)] * NLG
    win_steps = []
    for r0 in range(0, W, KV_SUB):
        r1 = min(r0 + KV_SUB, W)
        for lg in range(NLG):
            def score(r0=r0, r1=r1, lg=lg):
                k = kwin_ref[pl.ds(pl.multiple_of(wstart + r0, 128), r1 - r0), :]
                return _dot(k, qts[lg]) + wbias[r0:r1, :]

            def consume(s, r0=r0, r1=r1, lg=lg):
                vt = ones_under(vtw[rows_of(group_of(lg)), r0:r1])
                mw[lg] = _flash_update(s, None, r1 - r0, vt, mw[lg], accw_s.at[lg])

            win_steps.append((score, consume))
    _run_steps(win_steps, [fn() for fn, _ in win_steps[:LOOKAHEAD]], [])

    blk_i = lax.broadcasted_iota(jnp.int32, (n_sel, 1), 0)
    cur = tq1 // SEL_BLOCK
    forced = (blk_i == 0) | (blk_i == cur) | (blk_i == cur - 1)
    started = blk_i <= cur
    n_top = min(SEL_COUNT, n_sel)
    slc = []
    for g in range(G):
        slabs = []
        for sl in range(Q_TILE // 128):
            def tap(w):
                return pg_s[g * (Q_TILE // 128) + sl, pl.ds(8 + w, n_sel, stride=SEL_BLOCK // CMP_STRIDE), :]

            t = tap(-1) + 2.0 * tap(0)
            t = t + 2.0 * tap(1)
            t = t + 2.0 * tap(2)
            t = t + tap(3)
            slabs.append(t)
        t = slabs[0] if len(slabs) == 1 else jnp.concatenate(slabs, axis=1)
        slc.append(jnp.where(forced, NEG_INF, jnp.where(started, t, NEG_INF)))
    free = _topk_mask(jnp.concatenate(slc, axis=1), jnp.broadcast_to(blk_i.astype(F32), (n_sel, G * Q_TILE)),
                      n_top - N_FORCED)
    for g in range(G):
        bias = jnp.where(forced, 0.0, jnp.where(free[:, Q_TILE * g:Q_TILE * (g + 1)] > 0.5, 0.0, NEG_INF))
        bias_s[g] = jnp.concatenate([bias] * HPL, axis=1)

    bpc = KV_CHUNK // SEL_BLOCK
    bps = KV_SUB // SEL_BLOCK
    c_last = q0 // KV_CHUNK

    def chunk_steps(c, ms, causal):
        steps = []
        for sub in range(KV_CHUNK // KV_SUB):
            for lg in range(NLG):
                def score(sub=sub, lg=lg):
                    base = pl.multiple_of(c * KV_CHUNK, KV_CHUNK) + sub * KV_SUB
                    s = _dot(ksel_ref[pl.ds(pl.multiple_of(base, KV_SUB), KV_SUB), :], qts[lg])
                    return s if causal is None else s + causal[sub]

                def consume(s, sub=sub, lg=lg):
                    g = group_of(lg)
                    rows = bias_s[g, pl.ds(pl.multiple_of(c * bpc, bpc), bpc), :][bps * sub:bps * (sub + 1), :]
                    vt = vselt_ref[c, rows_of(g), KV_SUB * sub:KV_SUB * (sub + 1)]
                    ms[lg] = _flash_update(s, rows, SEL_BLOCK, ones_under(vt), ms[lg], acc_s.at[lg])

                steps.append((score, consume))
        return steps

    acc_s[...] = jnp.zeros_like(acc_s)

    def body(c, carry):
        ms = list(carry[:NLG])
        nxt = [fn for fn, _ in chunk_steps(c + 1, ms, None)[:LOOKAHEAD]]
        pending = _run_steps(chunk_steps(c, ms, None), carry[NLG:], nxt)
        return tuple(ms) + pending

    first = [fn() for fn, _ in chunk_steps(0, None, None)[:LOOKAHEAD]]
    carry = (jnp.full((1, LANES), NEG_INF, F32),) * NLG + tuple(first)
    carry = _chunk_loop(body, c_last, carry, NSA_TRIP_UNROLL)
    kpos = c_last * KV_CHUNK + lax.broadcasted_iota(jnp.int32, (KV_CHUNK, 1), 0)
    causal = _mask_bias(kpos <= tq1, HPL)
    n_sub = KV_CHUNK // KV_SUB
    needed = ((q0 % KV_CHUNK) + Q_TILE + KV_SUB - 1) // KV_SUB

    def run_diag(n_pieces):
        steps = chunk_steps(c_last, list(carry[:NLG]), [causal[KV_SUB * sub:KV_SUB * (sub + 1), :]
                                                       for sub in range(n_sub)])[:n_pieces * NLG]
        _run_steps(steps, [fn() for fn, _ in steps[:LOOKAHEAD]], [])

    for n_pieces in range(1, n_sub + 1):
        cond = needed == n_pieces if n_pieces < n_sub else needed >= n_pieces
        pl.when(cond)(functools.partial(run_diag, n_pieces))

    gates = jax.nn.sigmoid(gt_ref[0:3 * NSA_HEADS, :])
    outs = []
    for h in range(NSA_HEADS):
        g, hh = divmod(h, NSA_HPG)
        part, hl = divmod(hh, HPL)
        lg = g * PARTS + part
        cols = slice(Q_TILE * hl, Q_TILE * (hl + 1))
        o_sel = acc_s[lg, 0:HEAD_DIM, cols] / jnp.maximum(acc_s[lg, HEAD_DIM:HEAD_DIM + 1, cols], 1e-30)
        o_win = accw_s[lg, 0:HEAD_DIM, cols] / jnp.maximum(accw_s[lg, HEAD_DIM:HEAD_DIM + 1, cols], 1e-30)
        o = gates[3 * h:3 * h + 1, :] * ocmp_s[lg, :, cols]
        o = o + gates[3 * h + 1:3 * h + 2, :] * o_sel
        o = o + gates[3 * h + 2:3 * h + 3, :] * o_win
        outs.append(o)
    out_ref[...] = jnp.concatenate(outs, axis=0).T.astype(BF16)


def _nsa_call(qtn, gt, kc, vct, ksel, vselt, kwin, vwint, B, S):
    R = S // CMP_STRIDE
    n_sel = S // SEL_BLOCK
    nlg = NSA_HEADS * Q_TILE // LANES
    return pl.pallas_call(
        _nsa_kernel,
        out_shape=jax.ShapeDtypeStruct((B * S, NSA_DIM), BF16),
        grid=(B, S // Q_TILE),
        in_specs=[
            pl.BlockSpec((None, NSA_DIM, Q_TILE), lambda b, q: (b, 0, q)),
            pl.BlockSpec((None, 128, Q_TILE), lambda b, q: (b, 0, q)),
            pl.BlockSpec((None, R, 128), lambda b, q: (b, 0, 0)),
            pl.BlockSpec((None, 128, R), lambda b, q: (b, 0, 0)),
            pl.BlockSpec((S, 128), lambda b, q: (b, 0)),
            pl.BlockSpec((None, S // KV_CHUNK, 128, KV_CHUNK), lambda b, q: (b, 0, 0, 0)),
            pl.BlockSpec((S, 128), lambda b, q: (b, 0)),
            pl.BlockSpec((None, S // 128, 128, 128), lambda b, q: (b, 0, 0, 0)),
        ],
        out_specs=pl.BlockSpec((Q_TILE, NSA_DIM), lambda b, q: (b * (S // Q_TILE) + q, 0)),
        scratch_shapes=[
            pltpu.VMEM((nlg, 128, LANES), BF16),
            pltpu.VMEM((NSA_KV_GROUPS * (Q_TILE // 128), 8 + R, 128), F32),
            pltpu.VMEM((NSA_KV_GROUPS, n_sel, LANES), F32),
            pltpu.VMEM((nlg, HEAD_DIM + ONES_ROWS, LANES), F32),
            pltpu.VMEM((nlg, HEAD_DIM + ONES_ROWS, LANES), F32),
            pltpu.VMEM((nlg, HEAD_DIM, LANES), F32),
        ],
        compiler_params=pltpu.CompilerParams(
            dimension_semantics=("arbitrary", "arbitrary"), vmem_limit_bytes=VMEM_LIMIT),
        name="nsa_attn",
    )(qtn, gt, kc, vct, ksel, vselt, kwin, vwint)


def _moba_kernel(qt_ref, kmean_ref, k_ref, vt_ref, out_ref, qt_s, bias_s, acc_s):
    qi = pl.program_id(1)
    q0 = qi * MOBA_Q_TILE
    L = MOBA_Q_TILE
    H = MOBA_HEADS
    n_blk = kmean_ref.shape[0]
    n_top = max(1, min(MOBA_TOPK, n_blk - 1))

    qt_s[...] = jnp.zeros_like(qt_s)
    for h in range(H):
        rows = slice(HEAD_DIM * h, HEAD_DIM * (h + 1))
        qt_s[h, rows, :] = qt_ref[rows, :]
    qts = [qt_s[h] for h in range(H)]
    tq = q0 + lax.broadcasted_iota(jnp.int32, (1, L), 1)

    own = tq // MOBA_BLOCK
    blk_i = lax.broadcasted_iota(jnp.int32, (n_blk, 1), 0)
    blk_f = jnp.broadcast_to(blk_i.astype(F32), (n_blk, L))
    kmean = kmean_ref[...].astype(BF16)
    for h in range(H):
        sg = jnp.where(blk_i < own, _dot(kmean, qts[h]), NEG_INF)
        sel = _topk_mask(sg, blk_f, n_top)
        allow = ((sel > 0.5) & (blk_i < own)) | (blk_i == own)
        bias = jnp.where(allow, 0.0, NEG_INF)
        for j in range(n_blk):
            bias_s[h, j] = jnp.broadcast_to(bias[j:j + 1, :], (8, L))

    bpc = KV_CHUNK // MOBA_BLOCK

    steps = [(j, h) for j in range(bpc) for h in range(H)]

    ones = jnp.ones((ONES_ROWS, MOBA_BLOCK), BF16)

    def scores(c, j, h, diagonal):
        base = pl.multiple_of(c * KV_CHUNK, KV_CHUNK) + j * MOBA_BLOCK
        k = k_ref[pl.ds(pl.multiple_of(base, MOBA_BLOCK), MOBA_BLOCK), :]
        s = _dot(k, qts[h])
        if diagonal:
            kpos = base + lax.broadcasted_iota(jnp.int32, (MOBA_BLOCK, 1), 0)
            s = jnp.where(kpos <= tq, s, NEG_INF)
        return s

    def chunk_steps(c, ms, pending, diagonal):
        ms, pending = list(ms), list(pending)
        for t, (j, h) in enumerate(steps):
            s = pending.pop(0)
            u = t + LOOKAHEAD
            if u < len(steps):
                pending.append(scores(c, *steps[u], diagonal))
            elif not diagonal:
                pending.append(scores(c + 1, *steps[u - len(steps)], False))
            vt = vt_ref[c, HEAD_DIM * h:HEAD_DIM * (h + 1), MOBA_BLOCK * j:MOBA_BLOCK * (j + 1)]
            ms[h] = _flash_update(s, bias_s[h, c * bpc + j], MOBA_BLOCK, jnp.concatenate([vt, ones], axis=0),
                                  ms[h], acc_s.at[h])
        return tuple(ms), tuple(pending)

    acc_s[...] = jnp.zeros_like(acc_s)
    c_last = q0 // KV_CHUNK
    init = (jnp.full((1, L), NEG_INF, F32),) * H

    def body(c, carry):
        ms, pending = chunk_steps(c, carry[:H], carry[H:], False)
        return ms + pending

    carry = init + tuple(scores(0, *steps[i], False) for i in range(LOOKAHEAD))
    carry = _chunk_loop(body, c_last, carry, MOBA_TRIP_UNROLL)
    chunk_steps(c_last, carry[:H], [scores(c_last, *steps[i], True) for i in range(LOOKAHEAD)], True)
    outs = [acc_s[h, 0:HEAD_DIM, :] / jnp.maximum(acc_s[h, HEAD_DIM:HEAD_DIM + 1, :], 1e-30) for h in range(H)]
    out_ref[...] = jnp.concatenate(outs, axis=0).T.astype(BF16)


def _moba_call(qtm, kmean, km, vtm, B, S):
    n_blk = S // MOBA_BLOCK
    L = MOBA_Q_TILE
    return pl.pallas_call(
        _moba_kernel,
        out_shape=jax.ShapeDtypeStruct((B * S, MOBA_DIM), BF16),
        grid=(B, S // L),
        in_specs=[
            pl.BlockSpec((None, MOBA_DIM, L), lambda b, q: (b, 0, q)),
            pl.BlockSpec((None, n_blk, MOBA_DIM), lambda b, q: (b, 0, 0)),
            pl.BlockSpec((S, MOBA_DIM), lambda b, q: (b, 0)),
            pl.BlockSpec((None, S // KV_CHUNK, MOBA_DIM, KV_CHUNK), lambda b, q: (b, 0, 0, 0)),
        ],
        out_specs=pl.BlockSpec((L, MOBA_DIM), lambda b, q: (b * (S // L) + q, 0)),
        scratch_shapes=[
            pltpu.VMEM((MOBA_HEADS, MOBA_DIM, L), BF16),
            pltpu.VMEM((MOBA_HEADS, n_blk, 8, L), F32),
            pltpu.VMEM((MOBA_HEADS, HEAD_DIM + ONES_ROWS, L), F32),
        ],
        compiler_params=pltpu.CompilerParams(
            dimension_semantics=("arbitrary", "arbitrary"), vmem_limit_bytes=VMEM_LIMIT),
        name="moba_attn",
    )(qtm, kmean, km, vtm)


def _merge_kernel(u_ref, halo_ref, b_ref, c_ref, gbr_ref, x_ref, pw_ref, ps_ref, wa_ref, wb_ref, wc_ref,
                  wo_ref, gam_ref, x1_ref, h2_ref, ext_s, *, tiles_per_seq):
    tm = u_ref.shape[0]
    i = pl.program_id(0)
    first = (i % tiles_per_seq) == 0
    u = u_ref[...]
    ext_s[0:POOL_MAXW, :] = jnp.where(first, 0.0, halo_ref[...])
    ext_s[POOL_MAXW:POOL_MAXW + tm, :] = u

    def tail_sum(col, k0, k1):
        tot = None
        for k in range(k0, k1):
            v = ext_s[pl.ds(POOL_MAXW - k, tm), 128 * col:128 * col + 128]
            tot = v if tot is None else tot + v
        return tot

    t_glob = (i % tiles_per_seq) * tm + lax.broadcasted_iota(jnp.int32, (tm, 1), 0)
    lane = lax.broadcasted_iota(jnp.int32, (1, 128), 1)
    low = lane < POOL_GROUP_DIM
    pooled = []
    for col in range(2):
        wa_, wb_ = POOL_WINDOWS[2 * col], POOL_WINDOWS[2 * col + 1]
        sa = tail_sum(col, 0, wa_)
        sb = sa + tail_sum(col, wa_, wb_)
        ca = jnp.minimum(t_glob + 1, wa_).astype(F32)
        cb = jnp.minimum(t_glob + 1, wb_).astype(F32)
        pooled.append(jnp.where(low, sa, sb) / jnp.where(low, ca, cb))
    d = jnp.concatenate(pooled, axis=1) - u
    a = _dot(d.astype(BF16), pw_ref[...]) * ps_ref[...]

    av = _dot(a.astype(BF16), wa_ref[...])
    bv = _dot(b_ref[...], wb_ref[...])
    cv = _dot(c_ref[...], wc_ref[...])
    merged = jax.nn.sigmoid(gbr_ref[:, 0:D_MODEL]) * av
    merged = merged + jax.nn.sigmoid(gbr_ref[:, D_MODEL:2 * D_MODEL]) * bv
    merged = merged + jax.nn.sigmoid(gbr_ref[:, 2 * D_MODEL:3 * D_MODEL]) * cv
    x1 = x_ref[...] + _dot(merged.astype(BF16), wo_ref[...])
    x1_ref[...] = x1
    h2_ref[...] = _rms(x1, gam_ref[...]).astype(BF16)


def _merge_call(upool, bn, cm, gbr, x2, pw, ps, wa, wb, wc, wo, gamma, B, S):
    N = B * S
    tm = ROW_TILE
    nt = S // tm
    hb = tm // POOL_MAXW
    row = lambda w_: pl.BlockSpec((tm, w_), lambda i: (i, 0))
    const = lambda shape: pl.BlockSpec(shape, lambda i: (0,) * len(shape))
    return pl.pallas_call(
        functools.partial(_merge_kernel, tiles_per_seq=nt),
        out_shape=(jax.ShapeDtypeStruct((N, D_MODEL), F32), jax.ShapeDtypeStruct((N, D_MODEL), BF16)),
        grid=(N // tm,),
        in_specs=[
            row(POOL_DIM),
            pl.BlockSpec((POOL_MAXW, POOL_DIM), lambda i: (jnp.maximum(i * hb - 1, 0), 0)),
            row(NSA_DIM),
            row(MOBA_DIM),
            row(3 * D_MODEL),
            row(D_MODEL),
            const((POOL_DIM, POOL_DIM)),
            const((1, POOL_DIM)),
            const((POOL_DIM, D_MODEL)),
            const((NSA_DIM, D_MODEL)),
            const((MOBA_DIM, D_MODEL)),
            const((D_MODEL, D_MODEL)),
            const((1, D_MODEL)),
        ],
        out_specs=(row(D_MODEL), row(D_MODEL)),
        scratch_shapes=[pltpu.VMEM((POOL_MAXW + tm, POOL_DIM), F32)],
        compiler_params=pltpu.CompilerParams(
            dimension_semantics=("arbitrary",), vmem_limit_bytes=VMEM_LIMIT),
        name="merge",
    )(upool, upool, bn, cm, gbr, x2, pw, ps, wa, wb, wc, wo, gamma)


def _ffn_kernel(h_ref, x_ref, wg_ref, wu_ref, wd_ref, gam_ref, out_ref, acc_s, *, final_norm):
    f = pl.program_id(1)

    @pl.when(f == 0)
    def _():
        acc_s[...] = jnp.zeros_like(acc_s)

    h = h_ref[...]
    g = _dot(h, wg_ref[...])
    u = _dot(h, wu_ref[...])
    act = (g * jax.nn.sigmoid(g)) * u
    acc_s[...] += _dot(act.astype(BF16), wd_ref[...])

    @pl.when(f == pl.num_programs(1) - 1)
    def _():
        y = x_ref[...] + acc_s[...]
        if final_norm:
            y = _rms(y, gam_ref[...])
        out_ref[...] = y


def _ffn_call(h2, x1, wg, wu, wd, gamma, final_norm):
    N = x1.shape[0]
    F = wg.shape[1]
    tm, tf = FFN_ROW_TILE, FFN_COL_TILE
    return pl.pallas_call(
        functools.partial(_ffn_kernel, final_norm=final_norm),
        out_shape=jax.ShapeDtypeStruct((N, D_MODEL), F32),
        grid=(N // tm, F // tf),
        in_specs=[
            pl.BlockSpec((tm, D_MODEL), lambda i, f: (i, 0)),
            pl.BlockSpec((tm, D_MODEL), lambda i, f: (i, 0)),
            pl.BlockSpec((D_MODEL, tf), lambda i, f: (0, f)),
            pl.BlockSpec((D_MODEL, tf), lambda i, f: (0, f)),
            pl.BlockSpec((tf, D_MODEL), lambda i, f: (f, 0)),
            pl.BlockSpec((1, D_MODEL), lambda i, f: (0, 0)),
        ],
        out_specs=pl.BlockSpec((tm, D_MODEL), lambda i, f: (i, 0)),
        scratch_shapes=[pltpu.VMEM((tm, D_MODEL), F32)],
        compiler_params=pltpu.CompilerParams(
            dimension_semantics=("arbitrary", "arbitrary"), vmem_limit_bytes=VMEM_LIMIT),
        name="ffn",
    )(h2, x1, wg, wu, wd, gamma)


def _reorder_in_proj(w, b):
    o_qn = POOL_DIM
    o_kv = o_qn + NSA_DIM
    o_gn = o_kv + 6 * 128
    o_mo = o_gn + 3 * NSA_HEADS
    o_gb = o_mo + 3 * MOBA_DIM
    pad = 128 - 3 * NSA_HEADS

    def pick(a):
        parts = [a[..., 0:o_qn], a[..., o_kv:o_kv + 256], a[..., o_qn:o_kv], a[..., o_kv + 256:o_gn],
                 a[..., o_gn:o_mo], jnp.zeros(a.shape[:-1] + (pad,), a.dtype), a[..., o_mo:o_gb], a[..., o_gb:]]
        return jnp.concatenate(parts, axis=-1)

    return pick(w), pick(b)


def _rope_tables(S):
    pos = jnp.arange(S, dtype=F32)
    inv_freq = ROPE_THETA ** (-jnp.arange(0, HEAD_DIM, 2, dtype=F32) / HEAD_DIM)
    ang = pos[:, None] * inv_freq[None, :]
    cos, sin = jnp.cos(ang), jnp.sin(ang)
    cos_t = jnp.tile(cos, (1, 4))
    sin_t = jnp.tile(jnp.concatenate([-sin, sin], axis=1), (1, 2))
    return cos_t, sin_t


def kernel(x, attn_norm, w_in, b_in, pool_w, pool_scale, cmp_pos, cmp_w1, cmp_b1, cmp_w2, cmp_b2,
           w_br_pool, w_br_nsa, w_br_moba, w_out, ffn_norm, w_gate, w_up, w_down, final_norm):
    B, S, D = x.shape
    depth = w_in.shape[0]
    assert D == D_MODEL and S % KV_CHUNK == 0 and S % ROW_TILE == 0 and S >= WINDOW + Q_TILE
    assert (B * S) % FFN_ROW_TILE == 0 and w_gate.shape[2] % FFN_COL_TILE == 0
    N = B * S
    R = S // CMP_STRIDE
    cos_t, sin_t = _rope_tables(S)
    x2 = x.reshape(N, D)
    for l in range(depth):
        w_all, b_all = _reorder_in_proj(w_in[l].astype(BF16), b_in[l])
        (upool, cmpk, cmpv, qtn, ksel, vselt, kwin, vwint, gt, qtm, km, vtm, kmean, gbr) = _proj_call(
            x2, attn_norm[l][None, :], w_all, b_all[None, :], cos_t, sin_t, B, S)

        kc, vct = _compress_call(
            cmpk, cmpv, cmp_pos[l], cmp_w1[l].astype(BF16),
            cmp_b1[l][:, None, :], cmp_w2[l].astype(BF16), cmp_b2[l][:, None, :], B, R)

        bn = _nsa_call(qtn, gt, kc, vct, ksel, vselt, kwin, vwint, B, S)
        cm = _moba_call(qtm, kmean.reshape(B, S // MOBA_BLOCK, MOBA_DIM), km, vtm, B, S)

        pw_bd = jax.scipy.linalg.block_diag(*[pool_w[l, g] for g in range(len(POOL_WINDOWS))])
        x1, h2 = _merge_call(
            upool, bn, cm, gbr, x2, pw_bd.astype(BF16), pool_scale[l][None, :],
            w_br_pool[l].astype(BF16), w_br_nsa[l].astype(BF16), w_br_moba[l].astype(BF16),
            w_out[l].astype(BF16), ffn_norm[l][None, :], B, S)

        x2 = _ffn_call(h2, x1, w_gate[l].astype(BF16), w_up[l].astype(BF16), w_down[l].astype(BF16),
                       final_norm[None, :], final_norm=(l == depth - 1))
    return x2.reshape(B, S, D)
```

```python
import functools

import numpy as np
import jax
import jax.numpy as jnp
from jax import lax
from jax.experimental import pallas as pl
from jax.experimental.pallas import tpu as pltpu

F32 = jnp.float32
BF16 = jnp.bfloat16

D_MODEL = 1024
HEAD_DIM = 64
ROPE_THETA = 10000.0
RMS_EPS = 1e-6
NEG_INF = -1e30
LOG2E = 1.4426950408889634
LOOKAHEAD = 2
NSA_TRIP_UNROLL = 4
MOBA_TRIP_UNROLL = 4
ONES_ROWS = 16
REMOVED = -(2.0 ** 127)

POOL_WINDOWS = (2, 4, 8, 16)
POOL_GROUP_DIM = 64
POOL_DIM = 256
POOL_MAXW = 16

NSA_HEADS = 8
NSA_KV_GROUPS = 2
NSA_HPG = 4
NSA_DIM = 512
CMP_BLOCK = 32
CMP_STRIDE = 16
CMP_HIDDEN = 256
SEL_BLOCK = 64
SEL_COUNT = 16
N_FORCED = 3
WINDOW = 512

MOBA_HEADS = 4
MOBA_DIM = 256
MOBA_BLOCK = 256
MOBA_TOPK = 3

Q_TILE = 256
LANES = 512
MOBA_Q_TILE = 512
KV_CHUNK = 512
KV_SUB = 256
ROW_TILE = 512
FFN_ROW_TILE = 1024
FFN_COL_TILE = 1408
VMEM_LIMIT = 56 * 1024 * 1024

C_POOL = 0
C_CMP = 256
C_QN = 512
C_KS = 1024
C_VS = 1152
C_KW = 1280
C_VW = 1408
C_GN = 1536
C_QM = 1664
C_KM = 1920
C_VM = 2176
C_TOTAL = 2432


def _dot(a, b):
    return jnp.dot(a, b, preferred_element_type=F32)


def _rms(x, gamma):
    return x * lax.rsqrt(jnp.mean(x * x, axis=-1, keepdims=True) + RMS_EPS) * gamma


def _proj_kernel(x_ref, gam_ref, w_ref, b_ref, cos_ref, sin_ref,
                 upool_ref, cmpk_ref, cmpv_ref, qtn_ref, ksel_ref, vselt_ref, kwin_ref, vwint_ref, gt_ref,
                 qtm_ref, km_ref, vtm_ref, kmean_ref):
    tm = x_ref.shape[0]
    h = _rms(x_ref[...], gam_ref[...]).astype(BF16)
    cos = cos_ref[...]
    sin = sin_ref[...]
    lane = lax.broadcasted_iota(jnp.int32, (tm, 128), 1)
    first_half = (lane & 32) == 0
    scale = HEAD_DIM ** -0.5 * LOG2E

    def seg(a, b):
        return _dot(h, w_ref[:, a:b]) + b_ref[:, a:b]

    def rope(y):
        swap = jnp.where(first_half, pltpu.roll(y, 96, 1), pltpu.roll(y, 32, 1))
        return y * cos + swap * sin

    y = seg(C_POOL, C_POOL + 512)
    upool_ref[...] = y[:, :256]
    cmpk_ref[...] = rope(y[:, 256:384])
    cmpv_ref[...] = y[:, 384:512]

    y = seg(C_QN, C_QN + 512)
    for j in range(4):
        q = rope(y[:, 128 * j:128 * j + 128]) * scale
        qtn_ref[128 * j:128 * j + 128, :] = q.T.astype(BF16)

    y = seg(C_KS, C_KS + 512)
    ksel_ref[...] = rope(y[:, 0:128]).astype(BF16)
    vselt_ref[0] = y[:, 128:256].T.astype(BF16)
    kwin_ref[...] = rope(y[:, 256:384]).astype(BF16)
    vwt = y[:, 384:512].T.astype(BF16)
    for j in range(tm // 128):
        vwint_ref[j] = vwt[:, 128 * j:128 * j + 128]

    y = seg(C_GN, C_GN + 128)
    gt_ref[...] = y.T

    y = seg(C_QM, C_QM + 256)
    for j in range(2):
        q = rope(y[:, 128 * j:128 * j + 128]) * scale
        qtm_ref[128 * j:128 * j + 128, :] = q.T.astype(BF16)

    y = seg(C_KM, C_KM + 256)
    km = jnp.concatenate([rope(y[:, 0:128]), rope(y[:, 128:256])], axis=1)
    km_ref[...] = km.astype(BF16)
    nblk = tm // MOBA_BLOCK
    means = [jnp.sum(km[MOBA_BLOCK * j:MOBA_BLOCK * (j + 1), :], axis=0, keepdims=True) * (1.0 / MOBA_BLOCK)
             for j in range(nblk)]
    kmean_ref[...] = jnp.concatenate(means, axis=0)

    y = seg(C_VM, C_VM + 256)
    vtm_ref[0] = y.T.astype(BF16)


def _proj_call(x2, gamma, w, bias, cos_t, sin_t, B, S):
    N = B * S
    tm = ROW_TILE
    nt = S // tm
    f = lambda shape, dt: jax.ShapeDtypeStruct(shape, dt)
    out_shape = (
        f((N, 256), F32),
        f((N, 128), F32),
        f((N, 128), F32),
        f((B, NSA_DIM, S), BF16),
        f((N, 128), BF16),
        f((B, S // KV_CHUNK, 128, KV_CHUNK), BF16),
        f((N, 128), BF16),
        f((B, S // 128, 128, 128), BF16),
        f((B, 128, S), F32),
        f((B, MOBA_DIM, S), BF16),
        f((N, MOBA_DIM), BF16),
        f((B, S // KV_CHUNK, MOBA_DIM, KV_CHUNK), BF16),
        f((B, nt, tm // MOBA_BLOCK, MOBA_DIM), F32),
    )
    row = lambda w_: pl.BlockSpec((tm, w_), lambda i: (i, 0))
    const = lambda shape: pl.BlockSpec(shape, lambda i: (0,) * len(shape))
    in_specs = [
        row(D_MODEL),
        const((1, D_MODEL)),
        const((D_MODEL, C_TOTAL)),
        const((1, C_TOTAL)),
        pl.BlockSpec((tm, 128), lambda i: (i % nt, 0)),
        pl.BlockSpec((tm, 128), lambda i: (i % nt, 0)),
    ]
    out_specs = (
        row(256),
        row(128),
        row(128),
        pl.BlockSpec((None, NSA_DIM, tm), lambda i: (i // nt, 0, i % nt)),
        row(128),
        pl.BlockSpec((None, tm // KV_CHUNK, 128, KV_CHUNK), lambda i: (i // nt, i % nt, 0, 0)),
        row(128),
        pl.BlockSpec((None, tm // 128, 128, 128), lambda i: (i // nt, i % nt, 0, 0)),
        pl.BlockSpec((None, 128, tm), lambda i: (i // nt, 0, i % nt)),
        pl.BlockSpec((None, MOBA_DIM, tm), lambda i: (i // nt, 0, i % nt)),
        row(MOBA_DIM),
        pl.BlockSpec((None, tm // KV_CHUNK, MOBA_DIM, KV_CHUNK), lambda i: (i // nt, i % nt, 0, 0)),
        pl.BlockSpec((None, None, tm // MOBA_BLOCK, MOBA_DIM), lambda i: (i // nt, i % nt, 0, 0)),
    )
    return pl.pallas_call(
        _proj_kernel,
        out_shape=out_shape,
        grid=(N // tm,),
        in_specs=in_specs,
        out_specs=out_specs,
        compiler_params=pltpu.CompilerParams(
            dimension_semantics=("arbitrary",), vmem_limit_bytes=VMEM_LIMIT),
        name="proj",
    )(x2, gamma, w, bias, cos_t, sin_t)


def _gelu_tanh(x):
    return x * (0.5 * (1.0 + jnp.tanh(np.sqrt(2.0 / np.pi).astype(np.float32) * (x + 0.044715 * (x * x * x)))))


def _compress_kernel(xk_ref, xv_ref, pos_ref, w1_ref, b1_ref, w2_ref, b2_ref, kc_ref, vct_ref):
    R = xk_ref.shape[0] // CMP_STRIDE
    top = [None] * 4
    bot = [None] * 4
    for l in range(CMP_STRIDE):
        xl = [r[pl.ds(l, R, stride=CMP_STRIDE), :] for r in (xk_ref, xv_ref)]
        for j in range(4):
            t, g = divmod(j, NSA_KV_GROUPS)
            piece = xl[t][:, HEAD_DIM * g:HEAD_DIM * (g + 1)]
            lo = (piece + pos_ref[t, l:l + 1, :]).astype(BF16)
            hi = (piece + pos_ref[t, CMP_STRIDE + l:CMP_STRIDE + l + 1, :]).astype(BF16)
            a = _dot(lo, w1_ref[t, HEAD_DIM * l:HEAD_DIM * (l + 1), :])
            b = _dot(hi, w1_ref[t, HEAD_DIM * (CMP_STRIDE + l):HEAD_DIM * (CMP_STRIDE + l + 1), :])
            top[j] = a if top[j] is None else top[j] + a
            bot[j] = b if bot[j] is None else bot[j] + b
    outs = []
    for j in range(4):
        t = j // NSA_KV_GROUPS
        hid = _gelu_tanh(top[j] + pltpu.roll(bot[j], R - 1, 0) + b1_ref[t])
        outs.append(_dot(hid.astype(BF16), w2_ref[t]) + b2_ref[t])
    kc_ref[...] = jnp.concatenate(outs[0:2], axis=1).astype(BF16)
    vct_ref[...] = jnp.concatenate(outs[2:4], axis=1).T.astype(BF16)


def _compress_call(xk, xv, pos, w1, b1, w2, b2, B, R):
    S = R * CMP_STRIDE
    return pl.pallas_call(
        _compress_kernel,
        out_shape=(jax.ShapeDtypeStruct((B, R, 128), BF16), jax.ShapeDtypeStruct((B, 128, R), BF16)),
        grid=(B,),
        in_specs=[
            pl.BlockSpec((S, 128), lambda b: (b, 0)),
            pl.BlockSpec((S, 128), lambda b: (b, 0)),
            pl.BlockSpec((2, CMP_BLOCK, HEAD_DIM), lambda b: (0, 0, 0)),
            pl.BlockSpec((2, CMP_BLOCK * HEAD_DIM, CMP_HIDDEN), lambda b: (0, 0, 0)),
            pl.BlockSpec((2, 1, CMP_HIDDEN), lambda b: (0, 0, 0)),
            pl.BlockSpec((2, CMP_HIDDEN, HEAD_DIM), lambda b: (0, 0, 0)),
            pl.BlockSpec((2, 1, HEAD_DIM), lambda b: (0, 0, 0)),
        ],
        out_specs=(pl.BlockSpec((None, R, 128), lambda b: (b, 0, 0)),
                   pl.BlockSpec((None, 128, R), lambda b: (b, 0, 0))),
        compiler_params=pltpu.CompilerParams(
            dimension_semantics=("arbitrary",), vmem_limit_bytes=VMEM_LIMIT),
        name="nsa_compress",
    )(xk, xv, pos, w1, b1, w2, b2)


def _softmax_block(s):
    m = jnp.max(s, axis=0, keepdims=True)
    m_use = jnp.where(m < 0.5 * NEG_INF, 0.0, m)
    p = jnp.exp2(s - m_use)
    return p, jnp.sum(p, axis=0, keepdims=True)


def _flash_update(sc, rows, block, vt, m, acc_ref):
    n, L = sc.shape[0] // block, sc.shape[1]
    parts = [sc[block * j:block * (j + 1), :] for j in range(n)]
    tops = [jnp.max(parts[j].reshape(block // 8, 8, L), axis=0) for j in range(n)]
    if rows is not None:
        tops = [tops[j] + rows[j:j + 1, :] for j in range(n)]
    top = tops[0]
    for t in tops[1:]:
        top = jnp.maximum(top, t)
    m_new = jnp.maximum(m, jnp.max(top, axis=0, keepdims=True))
    m_use = jnp.where(m_new < 0.5 * NEG_INF, 0.0, m_new)
    alpha = jnp.exp2(m - m_use)
    if rows is None:
        p = jnp.exp2(sc - m_use)
    else:
        p = jnp.concatenate([jnp.exp2(parts[j] + (rows[j:j + 1, :] - m_use)) for j in range(n)], axis=0)
    acc_ref[...] = alpha * acc_ref[...] + _dot(vt, p.astype(BF16))
    return m_new


def _run_steps(steps, pending, tail):
    pending = list(pending)
    fns = [s for s, _ in steps] + list(tail)
    for t, (_, consume) in enumerate(steps):
        s = pending.pop(0)
        if t + LOOKAHEAD < len(fns):
            pending.append(fns[t + LOOKAHEAD]())
        consume(s)
    return tuple(pending)


def _chunk_loop(body, n, carry, unroll):
    def multi(i, cr):
        for j in range(unroll):
            cr = body(unroll * i + j, cr)
        return cr

    carry = lax.fori_loop(0, n // unroll, multi, carry)
    return lax.fori_loop(unroll * (n // unroll), n, body, carry)


def _mask_bias(ok, reps):
    return jnp.concatenate([jnp.where(ok, 0.0, NEG_INF)] * reps, axis=1)


def _topk_mask(scores, rows_f, k):
    work = scores
    for _ in range(k):
        cm = jnp.max(work, axis=0, keepdims=True)
        idx = jnp.min(jnp.where(work == cm, rows_f, 1e9), axis=0, keepdims=True)
        work = jnp.where(rows_f == idx, REMOVED, work)
    return jnp.where(work == REMOVED, 1.0, 0.0)


def _nsa_kernel(qt_ref, gt_ref, kc_ref, vct_ref, ksel_ref, vselt_ref, kwin_ref, vwint_ref, out_ref,
                qt_s, pg_s, bias_s, acc_s, accw_s, ocmp_s):
    qi = pl.program_id(1)
    q0 = qi * Q_TILE
    G = NSA_KV_GROUPS
    HPL = LANES // Q_TILE
    PARTS = NSA_HPG // HPL
    NLG = G * PARTS
    n_cmp_rows = kc_ref.shape[0]
    n_sel = bias_s.shape[1]
    group_of = lambda lg: lg // PARTS
    rows_of = lambda g: slice(HEAD_DIM * g, HEAD_DIM * (g + 1))

    qt_s[...] = jnp.zeros_like(qt_s)
    for h in range(NSA_HEADS):
        g, hh = divmod(h, NSA_HPG)
        part, hl = divmod(hh, HPL)
        qt_s[g * PARTS + part, rows_of(g), Q_TILE * hl:Q_TILE * (hl + 1)] = qt_ref[HEAD_DIM * h:HEAD_DIM * (h + 1), :]
    qts = [qt_s[lg] for lg in range(NLG)]
    tq1 = q0 + lax.broadcasted_iota(jnp.int32, (1, Q_TILE), 1)

    def ones_under(vt):
        return jnp.concatenate([vt, jnp.ones((ONES_ROWS, vt.shape[1]), BF16)], axis=0)

    def cmp_branch(nr):
        n_idx = lax.broadcasted_iota(jnp.int32, (nr, 1), 0)
        cbias = _mask_bias(n_idx * CMP_STRIDE + (CMP_BLOCK - 1) <= tq1, HPL)
        for g in range(G):
            pg = None
            for part in range(PARTS):
                lg = g * PARTS + part
                s = jnp.concatenate([_dot(kc_ref[r:min(r + KV_SUB, nr), :], qts[lg])
                                     for r in range(0, nr, KV_SUB)], axis=0) + cbias
                p, l = _softmax_block(s)
                den = jnp.maximum(l, 1e-30)
                ocmp_s[lg] = _dot(vct_ref[rows_of(g), 0:nr], p.astype(BF16)) / den
                pn = p / den
                for hl in range(HPL):
                    ph = pn[:, hl * Q_TILE:(hl + 1) * Q_TILE]
                    pg = ph if pg is None else pg + ph
            for sl in range(Q_TILE // 128):
                slab = g * (Q_TILE // 128) + sl
                pg_s[slab, 0:8, :] = jnp.zeros((8, 128), F32)
                pg_s[slab, 8:8 + nr, :] = pg[:, 128 * sl:128 * (sl + 1)]
                if nr < n_cmp_rows:
                    pg_s[slab, 8 + nr:8 + n_cmp_rows, :] = jnp.zeros((n_cmp_rows - nr, 128), F32)

    n_visible = (q0 + Q_TILE - CMP_BLOCK) // CMP_STRIDE + 1
    sizes = list(range(128, n_cmp_rows + 1, 128))
    for i, nr in enumerate(sizes):
        lo_ok = n_visible > sizes[i - 1] if i > 0 else True
        hi_ok = n_visible <= nr if i + 1 < len(sizes) else True
        pl.when(jnp.logical_and(lo_ok, hi_ok))(functools.partial(cmp_branch, nr))

    W = WINDOW + Q_TILE
    wstart = jnp.maximum(q0 - WINDOW, 0)
    wc0 = wstart // 128
    kposw = wstart + lax.broadcasted_iota(jnp.int32, (W, 1), 0)
    wbias = _mask_bias((kposw <= tq1) & (kposw > tq1 - WINDOW), HPL)
    vtw = jnp.concatenate([vwint_ref[wc0 + j] for j in range(W // 128)], axis=1)
    accw_s[...] = jnp.zeros_like(accw_s)
    mw = [jnp.full((1, LANES), NEG_INF, F32)] * NLG
    win_steps = []
    for r0 in range(0, W, KV_SUB):
        r1 = min(r0 + KV_SUB, W)
        for lg in range(NLG):
            def score(r0=r0, r1=r1, lg=lg):
                k = kwin_ref[pl.ds(pl.multiple_of(wstart + r0, 128), r1 - r0), :]
                return _dot(k, qts[lg]) + wbias[r0:r1, :]

            def consume(s, r0=r0, r1=r1, lg=lg):
                vt = ones_under(vtw[rows_of(group_of(lg)), r0:r1])
                mw[lg] = _flash_update(s, None, r1 - r0, vt, mw[lg], accw_s.at[lg])

            win_steps.append((score, consume))
    _run_steps(win_steps, [fn() for fn, _ in win_steps[:LOOKAHEAD]], [])

    blk_i = lax.broadcasted_iota(jnp.int32, (n_sel, 1), 0)
    cur = tq1 // SEL_BLOCK
    forced = (blk_i == 0) | (blk_i == cur) | (blk_i == cur - 1)
    started = blk_i <= cur
    n_top = min(SEL_COUNT, n_sel)
    slc = []
    for g in range(G):
        slabs = []
        for sl in range(Q_TILE // 128):
            def tap(w):
                return pg_s[g * (Q_TILE // 128) + sl, pl.ds(8 + w, n_sel, stride=SEL_BLOCK // CMP_STRIDE), :]

            t = tap(-1) + 2.0 * tap(0)
            t = t + 2.0 * tap(1)
            t = t + 2.0 * tap(2)
            t = t + tap(3)
            slabs.append(t)
        t = slabs[0] if len(slabs) == 1 else jnp.concatenate(slabs, axis=1)
        slc.append(jnp.where(forced, NEG_INF, jnp.where(started, t, NEG_INF)))
    free = _topk_mask(jnp.concatenate(slc, axis=1), jnp.broadcast_to(blk_i.astype(F32), (n_sel, G * Q_TILE)),
                      n_top - N_FORCED)
    for g in range(G):
        bias = jnp.where(forced, 0.0, jnp.where(free[:, Q_TILE * g:Q_TILE * (g + 1)] > 0.5, 0.0, NEG_INF))
        bias_s[g] = jnp.concatenate([bias] * HPL, axis=1)

    bpc = KV_CHUNK // SEL_BLOCK
    bps = KV_SUB // SEL_BLOCK
    c_last = q0 // KV_CHUNK

    def chunk_steps(c, ms, causal):
        steps = []
        for sub in range(KV_CHUNK // KV_SUB):
            for lg in range(NLG):
                def score(sub=sub, lg=lg):
                    base = pl.multiple_of(c * KV_CHUNK, KV_CHUNK) + sub * KV_SUB
                    s = _dot(ksel_ref[pl.ds(pl.multiple_of(base, KV_SUB), KV_SUB), :], qts[lg])
                    return s if causal is None else s + causal[sub]

                def consume(s, sub=sub, lg=lg):
                    g = group_of(lg)
                    rows = bias_s[g, pl.ds(pl.multiple_of(c * bpc, bpc), bpc), :][bps * sub:bps * (sub + 1), :]
                    vt = vselt_ref[c, rows_of(g), KV_SUB * sub:KV_SUB * (sub + 1)]
                    ms[lg] = _flash_update(s, rows, SEL_BLOCK, ones_under(vt), ms[lg], acc_s.at[lg])

                steps.append((score, consume))
        return steps

    acc_s[...] = jnp.zeros_like(acc_s)

    def body(c, carry):
        ms = list(carry[:NLG])
        nxt = [fn for fn, _ in chunk_steps(c + 1, ms, None)[:LOOKAHEAD]]
        pending = _run_steps(chunk_steps(c, ms, None), carry[NLG:], nxt)
        return tuple(ms) + pending

    first = [fn() for fn, _ in chunk_steps(0, None, None)[:LOOKAHEAD]]
    carry = (jnp.full((1, LANES), NEG_INF, F32),) * NLG + tuple(first)
    carry = _chunk_loop(body, c_last, carry, NSA_TRIP_UNROLL)
    kpos = c_last * KV_CHUNK + lax.broadcasted_iota(jnp.int32, (KV_CHUNK, 1), 0)
    causal = _mask_bias(kpos <= tq1, HPL)
    n_sub = KV_CHUNK // KV_SUB
    needed = ((q0 % KV_CHUNK) + Q_TILE + KV_SUB - 1) // KV_SUB

    def run_diag(n_pieces):
        steps = chunk_steps(c_last, list(carry[:NLG]), [causal[KV_SUB * sub:KV_SUB * (sub + 1), :]
                                                       for sub in range(n_sub)])[:n_pieces * NLG]
        _run_steps(steps, [fn() for fn, _ in steps[:LOOKAHEAD]], [])

    for n_pieces in range(1, n_sub + 1):
        cond = needed == n_pieces if n_pieces < n_sub else needed >= n_pieces
        pl.when(cond)(functools.partial(run_diag, n_pieces))

    gates = jax.nn.sigmoid(gt_ref[0:3 * NSA_HEADS, :])
    outs = []
    for h in range(NSA_HEADS):
        g, hh = divmod(h, NSA_HPG)
        part, hl = divmod(hh, HPL)
        lg = g * PARTS + part
        cols = slice(Q_TILE * hl, Q_TILE * (hl + 1))
        o_sel = acc_s[lg, 0:HEAD_DIM, cols] / jnp.maximum(acc_s[lg, HEAD_DIM:HEAD_DIM + 1, cols], 1e-30)
        o_win = accw_s[lg, 0:HEAD_DIM, cols] / jnp.maximum(accw_s[lg, HEAD_DIM:HEAD_DIM + 1, cols], 1e-30)
        o = gates[3 * h:3 * h + 1, :] * ocmp_s[lg, :, cols]
        o = o + gates[3 * h + 1:3 * h + 2, :] * o_sel
        o = o + gates[3 * h + 2:3 * h + 3, :] * o_win
        outs.append(o)
    out_ref[...] = jnp.concatenate(outs, axis=0).T.astype(BF16)


def _nsa_call(qtn, gt, kc, vct, ksel, vselt, kwin, vwint, B, S):
    R = S // CMP_STRIDE
    n_sel = S // SEL_BLOCK
    nlg = NSA_HEADS * Q_TILE // LANES
    return pl.pallas_call(
        _nsa_kernel,
        out_shape=jax.ShapeDtypeStruct((B * S, NSA_DIM), BF16),
        grid=(B, S // Q_TILE),
        in_specs=[
            pl.BlockSpec((None, NSA_DIM, Q_TILE), lambda b, q: (b, 0, q)),
            pl.BlockSpec((None, 128, Q_TILE), lambda b, q: (b, 0, q)),
            pl.BlockSpec((None, R, 128), lambda b, q: (b, 0, 0)),
            pl.BlockSpec((None, 128, R), lambda b, q: (b, 0, 0)),
            pl.BlockSpec((S, 128), lambda b, q: (b, 0)),
            pl.BlockSpec((None, S // KV_CHUNK, 128, KV_CHUNK), lambda b, q: (b, 0, 0, 0)),
            pl.BlockSpec((S, 128), lambda b, q: (b, 0)),
            pl.BlockSpec((None, S // 128, 128, 128), lambda b, q: (b, 0, 0, 0)),
        ],
        out_specs=pl.BlockSpec((Q_TILE, NSA_DIM), lambda b, q: (b * (S // Q_TILE) + q, 0)),
        scratch_shapes=[
            pltpu.VMEM((nlg, 128, LANES), BF16),
            pltpu.VMEM((NSA_KV_GROUPS * (Q_TILE // 128), 8 + R, 128), F32),
            pltpu.VMEM((NSA_KV_GROUPS, n_sel, LANES), F32),
            pltpu.VMEM((nlg, HEAD_DIM + ONES_ROWS, LANES), F32),
            pltpu.VMEM((nlg, HEAD_DIM + ONES_ROWS, LANES), F32),
            pltpu.VMEM((nlg, HEAD_DIM, LANES), F32),
        ],
        compiler_params=pltpu.CompilerParams(
            dimension_semantics=("arbitrary", "arbitrary"), vmem_limit_bytes=VMEM_LIMIT),
        name="nsa_attn",
    )(qtn, gt, kc, vct, ksel, vselt, kwin, vwint)


def _moba_kernel(qt_ref, kmean_ref, k_ref, vt_ref, out_ref, qt_s, bias_s, acc_s):
    qi = pl.program_id(1)
    q0 = qi * MOBA_Q_TILE
    L = MOBA_Q_TILE
    H = MOBA_HEADS
    n_blk = kmean_ref.shape[0]
    n_top = max(1, min(MOBA_TOPK, n_blk - 1))

    qt_s[...] = jnp.zeros_like(qt_s)
    for h in range(H):
        rows = slice(HEAD_DIM * h, HEAD_DIM * (h + 1))
        qt_s[h, rows, :] = qt_ref[rows, :]
    qts = [qt_s[h] for h in range(H)]
    tq = q0 + lax.broadcasted_iota(jnp.int32, (1, L), 1)

    own = tq // MOBA_BLOCK
    blk_i = lax.broadcasted_iota(jnp.int32, (n_blk, 1), 0)
    blk_f = jnp.broadcast_to(blk_i.astype(F32), (n_blk, L))
    kmean = kmean_ref[...].astype(BF16)
    for h in range(H):
        sg = jnp.where(blk_i < own, _dot(kmean, qts[h]), NEG_INF)
        sel = _topk_mask(sg, blk_f, n_top)
        allow = ((sel > 0.5) & (blk_i < own)) | (blk_i == own)
        bias = jnp.where(allow, 0.0, NEG_INF)
        for j in range(n_blk):
            bias_s[h, j] = jnp.broadcast_to(bias[j:j + 1, :], (8, L))

    bpc = KV_CHUNK // MOBA_BLOCK

    steps = [(j, h) for j in range(bpc) for h in range(H)]

    ones = jnp.ones((ONES_ROWS, MOBA_BLOCK), BF16)

    def scores(c, j, h, diagonal):
        base = pl.multiple_of(c * KV_CHUNK, KV_CHUNK) + j * MOBA_BLOCK
        k = k_ref[pl.ds(pl.multiple_of(base, MOBA_BLOCK), MOBA_BLOCK), :]
        s = _dot(k, qts[h])
        if diagonal:
            kpos = base + lax.broadcasted_iota(jnp.int32, (MOBA_BLOCK, 1), 0)
            s = jnp.where(kpos <= tq, s, NEG_INF)
        return s

    def chunk_steps(c, ms, pending, diagonal):
        ms, pending = list(ms), list(pending)
        for t, (j, h) in enumerate(steps):
            s = pending.pop(0)
            u = t + LOOKAHEAD
            if u < len(steps):
                pending.append(scores(c, *steps[u], diagonal))
            elif not diagonal:
                pending.append(scores(c + 1, *steps[u - len(steps)], False))
            vt = vt_ref[c, HEAD_DIM * h:HEAD_DIM * (h + 1), MOBA_BLOCK * j:MOBA_BLOCK * (j + 1)]
            ms[h] = _flash_update(s, bias_s[h, c * bpc + j], MOBA_BLOCK, jnp.concatenate([vt, ones], axis=0),
                                  ms[h], acc_s.at[h])
        return tuple(ms), tuple(pending)

    acc_s[...] = jnp.zeros_like(acc_s)
    c_last = q0 // KV_CHUNK
    init = (jnp.full((1, L), NEG_INF, F32),) * H

    def body(c, carry):
        ms, pending = chunk_steps(c, carry[:H], carry[H:], False)
        return ms + pending

    carry = init + tuple(scores(0, *steps[i], False) for i in range(LOOKAHEAD))
    carry = _chunk_loop(body, c_last, carry, MOBA_TRIP_UNROLL)
    chunk_steps(c_last, carry[:H], [scores(c_last, *steps[i], True) for i in range(LOOKAHEAD)], True)
    outs = [acc_s[h, 0:HEAD_DIM, :] / jnp.maximum(acc_s[h, HEAD_DIM:HEAD_DIM + 1, :], 1e-30) for h in range(H)]
    out_ref[...] = jnp.concatenate(outs, axis=0).T.astype(BF16)


def _moba_call(qtm, kmean, km, vtm, B, S):
    n_blk = S // MOBA_BLOCK
    L = MOBA_Q_TILE
    return pl.pallas_call(
        _moba_kernel,
        out_shape=jax.ShapeDtypeStruct((B * S, MOBA_DIM), BF16),
        grid=(B, S // L),
        in_specs=[
            pl.BlockSpec((None, MOBA_DIM, L), lambda b, q: (b, 0, q)),
            pl.BlockSpec((None, n_blk, MOBA_DIM), lambda b, q: (b, 0, 0)),
            pl.BlockSpec((S, MOBA_DIM), lambda b, q: (b, 0)),
            pl.BlockSpec((None, S // KV_CHUNK, MOBA_DIM, KV_CHUNK), lambda b, q: (b, 0, 0, 0)),
        ],
        out_specs=pl.BlockSpec((L, MOBA_DIM), lambda b, q: (b * (S // L) + q, 0)),
        scratch_shapes=[
            pltpu.VMEM((MOBA_HEADS, MOBA_DIM, L), BF16),
            pltpu.VMEM((MOBA_HEADS, n_blk, 8, L), F32),
            pltpu.VMEM((MOBA_HEADS, HEAD_DIM + ONES_ROWS, L), F32),
        ],
        compiler_params=pltpu.CompilerParams(
            dimension_semantics=("arbitrary", "arbitrary"), vmem_limit_bytes=VMEM_LIMIT),
        name="moba_attn",
    )(qtm, kmean, km, vtm)


def _merge_kernel(u_ref, halo_ref, b_ref, c_ref, x_ref, gam_in_ref, wg_ref, bg_ref, pw_ref, ps_ref,
                  wa_ref, wb_ref, wc_ref, wo_ref, gam_ref, x1_ref, h2_ref, ext_s, *, tiles_per_seq):
    tm = u_ref.shape[0]
    i = pl.program_id(0)
    first = (i % tiles_per_seq) == 0
    u = u_ref[...]
    ext_s[0:POOL_MAXW, :] = jnp.where(first, 0.0, halo_ref[...])
    ext_s[POOL_MAXW:POOL_MAXW + tm, :] = u

    def tail_sum(col, k0, k1):
        tot = None
        for k in range(k0, k1):
            v = ext_s[pl.ds(POOL_MAXW - k, tm), 128 * col:128 * col + 128]
            tot = v if tot is None else tot + v
        return tot

    t_glob = (i % tiles_per_seq) * tm + lax.broadcasted_iota(jnp.int32, (tm, 1), 0)
    lane = lax.broadcasted_iota(jnp.int32, (1, 128), 1)
    low = lane < POOL_GROUP_DIM
    pooled = []
    for col in range(2):
        wa_, wb_ = POOL_WINDOWS[2 * col], POOL_WINDOWS[2 * col + 1]
        sa = tail_sum(col, 0, wa_)
        sb = sa + tail_sum(col, wa_, wb_)
        ca = jnp.minimum(t_glob + 1, wa_).astype(F32)
        cb = jnp.minimum(t_glob + 1, wb_).astype(F32)
        pooled.append(jnp.where(low, sa, sb) / jnp.where(low, ca, cb))
    d = jnp.concatenate(pooled, axis=1) - u
    a = _dot(d.astype(BF16), pw_ref[...]) * ps_ref[...]

    av = _dot(a.astype(BF16), wa_ref[...])
    bv = _dot(b_ref[...], wb_ref[...])
    cv = _dot(c_ref[...], wc_ref[...])
    x = x_ref[...]
    h = _rms(x, gam_in_ref[...]).astype(BF16)

    def gate(j):
        cols = slice(D_MODEL * j, D_MODEL * (j + 1))
        return jax.nn.sigmoid(_dot(h, wg_ref[:, cols]) + bg_ref[:, cols])

    merged = gate(0) * av
    merged = merged + gate(1) * bv
    merged = merged + gate(2) * cv
    x1 = x + _dot(merged.astype(BF16), wo_ref[...])
    x1_ref[...] = x1
    h2_ref[...] = _rms(x1, gam_ref[...]).astype(BF16)


def _merge_call(upool, bn, cm, x2, gamma_in, wg, bg, pw, ps, wa, wb, wc, wo, gamma, B, S):
    N = B * S
    tm = ROW_TILE
    nt = S // tm
    hb = tm // POOL_MAXW
    row = lambda w_: pl.BlockSpec((tm, w_), lambda i: (i, 0))
    const = lambda shape: pl.BlockSpec(shape, lambda i: (0,) * len(shape))
    return pl.pallas_call(
        functools.partial(_merge_kernel, tiles_per_seq=nt),
        out_shape=(jax.ShapeDtypeStruct((N, D_MODEL), F32), jax.ShapeDtypeStruct((N, D_MODEL), BF16)),
        grid=(N // tm,),
        in_specs=[
            row(POOL_DIM),
            pl.BlockSpec((POOL_MAXW, POOL_DIM), lambda i: (jnp.maximum(i * hb - 1, 0), 0)),
            row(NSA_DIM),
            row(MOBA_DIM),
            row(D_MODEL),
            const((1, D_MODEL)),
            const((D_MODEL, 3 * D_MODEL)),
            const((1, 3 * D_MODEL)),
            const((POOL_DIM, POOL_DIM)),
            const((1, POOL_DIM)),
            const((POOL_DIM, D_MODEL)),
            const((NSA_DIM, D_MODEL)),
            const((MOBA_DIM, D_MODEL)),
            const((D_MODEL, D_MODEL)),
            const((1, D_MODEL)),
        ],
        out_specs=(row(D_MODEL), row(D_MODEL)),
        scratch_shapes=[pltpu.VMEM((POOL_MAXW + tm, POOL_DIM), F32)],
        compiler_params=pltpu.CompilerParams(
            dimension_semantics=("arbitrary",), vmem_limit_bytes=VMEM_LIMIT),
        name="merge",
    )(upool, upool, bn, cm, x2, gamma_in, wg, bg, pw, ps, wa, wb, wc, wo, gamma)


def _ffn_kernel(h_ref, x_ref, wg_ref, wu_ref, wd_ref, gam_ref, out_ref, acc_s, *, final_norm):
    f = pl.program_id(1)

    @pl.when(f == 0)
    def _():
        acc_s[...] = jnp.zeros_like(acc_s)

    h = h_ref[...]
    g = _dot(h, wg_ref[...])
    u = _dot(h, wu_ref[...])
    act = (g * jax.nn.sigmoid(g)) * u
    acc_s[...] += _dot(act.astype(BF16), wd_ref[...])

    @pl.when(f == pl.num_programs(1) - 1)
    def _():
        y = x_ref[...] + acc_s[...]
        if final_norm:
            y = _rms(y, gam_ref[...])
        out_ref[...] = y


def _ffn_call(h2, x1, wg, wu, wd, gamma, final_norm):
    N = x1.shape[0]
    F = wg.shape[1]
    tm, tf = FFN_ROW_TILE, FFN_COL_TILE
    return pl.pallas_call(
        functools.partial(_ffn_kernel, final_norm=final_norm),
        out_shape=jax.ShapeDtypeStruct((N, D_MODEL), F32),
        grid=(N // tm, F // tf),
        in_specs=[
            pl.BlockSpec((tm, D_MODEL), lambda i, f: (i, 0)),
            pl.BlockSpec((tm, D_MODEL), lambda i, f: (i, 0)),
            pl.BlockSpec((D_MODEL, tf), lambda i, f: (0, f)),
            pl.BlockSpec((D_MODEL, tf), lambda i, f: (0, f)),
            pl.BlockSpec((tf, D_MODEL), lambda i, f: (f, 0)),
            pl.BlockSpec((1, D_MODEL), lambda i, f: (0, 0)),
        ],
        out_specs=pl.BlockSpec((tm, D_MODEL), lambda i, f: (i, 0)),
        scratch_shapes=[pltpu.VMEM((tm, D_MODEL), F32)],
        compiler_params=pltpu.CompilerParams(
            dimension_semantics=("arbitrary", "arbitrary"), vmem_limit_bytes=VMEM_LIMIT),
        name="ffn",
    )(h2, x1, wg, wu, wd, gamma)


def _reorder_in_proj(w, b):
    o_qn = POOL_DIM
    o_kv = o_qn + NSA_DIM
    o_gn = o_kv + 6 * 128
    o_mo = o_gn + 3 * NSA_HEADS
    o_gb = o_mo + 3 * MOBA_DIM
    pad = 128 - 3 * NSA_HEADS

    def pick(a):
        parts = [a[..., 0:o_qn], a[..., o_kv:o_kv + 256], a[..., o_qn:o_kv], a[..., o_kv + 256:o_gn],
                 a[..., o_gn:o_mo], jnp.zeros(a.shape[:-1] + (pad,), a.dtype), a[..., o_mo:o_gb]]
        return jnp.concatenate(parts, axis=-1)

    return pick(w), pick(b), w[..., o_gb:], b[..., o_gb:]


def _rope_tables(S):
    pos = jnp.arange(S, dtype=F32)
    inv_freq = ROPE_THETA ** (-jnp.arange(0, HEAD_DIM, 2, dtype=F32) / HEAD_DIM)
    ang = pos[:, None] * inv_freq[None, :]
    cos, sin = jnp.cos(ang), jnp.sin(ang)
    cos_t = jnp.tile(cos, (1, 4))
    sin_t = jnp.tile(jnp.concatenate([-sin, sin], axis=1), (1, 2))
    return cos_t, sin_t


def kernel(x, attn_norm, w_in, b_in, pool_w, pool_scale, cmp_pos, cmp_w1, cmp_b1, cmp_w2, cmp_b2,
           w_br_pool, w_br_nsa, w_br_moba, w_out, ffn_norm, w_gate, w_up, w_down, final_norm):
    B, S, D = x.shape
    depth = w_in.shape[0]
    assert D == D_MODEL and S % KV_CHUNK == 0 and S % ROW_TILE == 0 and S >= WINDOW + Q_TILE
    assert (B * S) % FFN_ROW_TILE == 0 and w_gate.shape[2] % FFN_COL_TILE == 0
    N = B * S
    R = S // CMP_STRIDE
    cos_t, sin_t = _rope_tables(S)
    x2 = x.reshape(N, D)
    for l in range(depth):
        w_all, b_all, w_gbr, b_gbr = _reorder_in_proj(w_in[l].astype(BF16), b_in[l])
        (upool, cmpk, cmpv, qtn, ksel, vselt, kwin, vwint, gt, qtm, km, vtm, kmean) = _proj_call(
            x2, attn_norm[l][None, :], w_all, b_all[None, :], cos_t, sin_t, B, S)

        kc, vct = _compress_call(
            cmpk, cmpv, cmp_pos[l], cmp_w1[l].astype(BF16),
            cmp_b1[l][:, None, :], cmp_w2[l].astype(BF16), cmp_b2[l][:, None, :], B, R)

        bn = _nsa_call(qtn, gt, kc, vct, ksel, vselt, kwin, vwint, B, S)
        cm = _moba_call(qtm, kmean.reshape(B, S // MOBA_BLOCK, MOBA_DIM), km, vtm, B, S)

        pw_bd = jax.scipy.linalg.block_diag(*[pool_w[l, g] for g in range(len(POOL_WINDOWS))])
        x1, h2 = _merge_call(
            upool, bn, cm, x2, attn_norm[l][None, :], w_gbr, b_gbr[None, :], pw_bd.astype(BF16), pool_scale[l][None, :],
            w_br_pool[l].astype(BF16), w_br_nsa[l].astype(BF16), w_br_moba[l].astype(BF16),
            w_out[l].astype(BF16), ffn_norm[l][None, :], B, S)

        x2 = _ffn_call(h2, x1, w_gate[l].astype(BF16), w_up[l].astype(BF16), w_down[l].astype(BF16),
                       final_norm[None, :], final_norm=(l == depth - 1))
    return x2.reshape(B, S, D)
```

```python
import functools

import numpy as np
import jax
import jax.numpy as jnp
from jax import lax
from jax.experimental import pallas as pl
from jax.experimental.pallas import tpu as pltpu

F32 = jnp.float32
BF16 = jnp.bfloat16

D_MODEL = 1024
HEAD_DIM = 64
ROPE_THETA = 10000.0
RMS_EPS = 1e-6
NEG_INF = -1e30
LOG2E = 1.4426950408889634
LOOKAHEAD = 2
NSA_TRIP_UNROLL = 4
MOBA_TRIP_UNROLL = 4
ONES_ROWS = 16
REMOVED = -(2.0 ** 127)

POOL_WINDOWS = (2, 4, 8, 16)
POOL_GROUP_DIM = 64
POOL_DIM = 256
POOL_MAXW = 16

NSA_HEADS = 8
NSA_KV_GROUPS = 2
NSA_HPG = 4
NSA_DIM = 512
CMP_BLOCK = 32
CMP_STRIDE = 16
CMP_HIDDEN = 256
SEL_BLOCK = 64
SEL_COUNT = 16
N_FORCED = 3
WINDOW = 512

MOBA_HEADS = 4
MOBA_DIM = 256
MOBA_BLOCK = 256
MOBA_TOPK = 3

Q_TILE = 256
LANES = 512
MOBA_Q_TILE = 512
KV_CHUNK = 512
KV_SUB = 256
ROW_TILE = 512
FFN_ROW_TILE = 1024
FFN_COL_TILE = 1408
VMEM_LIMIT = 56 * 1024 * 1024

C_POOL = 0
C_CMP = 256
C_QN = 512
C_KS = 1024
C_VS = 1152
C_KW = 1280
C_VW = 1408
C_GN = 1536
C_QM = 1664
C_KM = 1920
C_VM = 2176
C_TOTAL = 2432


def _dot(a, b):
    return jnp.dot(a, b, preferred_element_type=F32)


def _rms(x, gamma):
    return x * lax.rsqrt(jnp.mean(x * x, axis=-1, keepdims=True) + RMS_EPS) * gamma


def _proj_kernel(x_ref, gam_ref, w_ref, b_ref, cos_ref, sin_ref,
                 upool_ref, cmpk_ref, cmpv_ref, qtn_ref, ksel_ref, vselt_ref, kwin_ref, vwint_ref, gt_ref,
                 qtm_ref, km_ref, vtm_ref, kmean_ref):
    tm = x_ref.shape[0]
    h = _rms(x_ref[...], gam_ref[...]).astype(BF16)
    cos = cos_ref[...]
    sin = sin_ref[...]
    lane = lax.broadcasted_iota(jnp.int32, (tm, 128), 1)
    first_half = (lane & 32) == 0
    scale = HEAD_DIM ** -0.5 * LOG2E

    def seg(a, b):
        return _dot(h, w_ref[:, a:b]) + b_ref[:, a:b]

    def rope(y):
        swap = jnp.where(first_half, pltpu.roll(y, 96, 1), pltpu.roll(y, 32, 1))
        return y * cos + swap * sin

    y = seg(C_POOL, C_POOL + 512)
    upool_ref[...] = y[:, :256]
    cmpk_ref[...] = rope(y[:, 256:384])
    cmpv_ref[...] = y[:, 384:512]

    y = seg(C_QN, C_QN + 512)
    for j in range(4):
        q = rope(y[:, 128 * j:128 * j + 128]) * scale
        qtn_ref[128 * j:128 * j + 128, :] = q.T.astype(BF16)

    y = seg(C_KS, C_KS + 512)
    ksel_ref[...] = rope(y[:, 0:128]).astype(BF16)
    vselt_ref[0] = y[:, 128:256].T.astype(BF16)
    kwin_ref[...] = rope(y[:, 256:384]).astype(BF16)
    vwt = y[:, 384:512].T.astype(BF16)
    for j in range(tm // 128):
        vwint_ref[j] = vwt[:, 128 * j:128 * j + 128]

    y = seg(C_GN, C_GN + 128)
    gt_ref[...] = y.T

    y = seg(C_QM, C_QM + 256)
    for j in range(2):
        q = rope(y[:, 128 * j:128 * j + 128]) * scale
        qtm_ref[128 * j:128 * j + 128, :] = q.T.astype(BF16)

    y = seg(C_KM, C_KM + 256)
    km = jnp.concatenate([rope(y[:, 0:128]), rope(y[:, 128:256])], axis=1)
    km_ref[...] = km.astype(BF16)
    nblk = tm // MOBA_BLOCK
    means = [jnp.sum(km[MOBA_BLOCK * j:MOBA_BLOCK * (j + 1), :], axis=0, keepdims=True) * (1.0 / MOBA_BLOCK)
             for j in range(nblk)]
    kmean_ref[...] = jnp.concatenate(means, axis=0)

    y = seg(C_VM, C_VM + 256)
    vtm_ref[0] = y.T.astype(BF16)


def _proj_call(x2, gamma, w, bias, cos_t, sin_t, B, S):
    N = B * S
    tm = ROW_TILE
    nt = S // tm
    f = lambda shape, dt: jax.ShapeDtypeStruct(shape, dt)
    out_shape = (
        f((N, 256), F32),
        f((N, 128), F32),
        f((N, 128), F32),
        f((B, NSA_DIM, S), BF16),
        f((N, 128), BF16),
        f((B, S // KV_CHUNK, 128, KV_CHUNK), BF16),
        f((N, 128), BF16),
        f((B, S // 128, 128, 128), BF16),
        f((B, 128, S), F32),
        f((B, MOBA_DIM, S), BF16),
        f((N, MOBA_DIM), BF16),
        f((B, S // KV_CHUNK, MOBA_DIM, KV_CHUNK), BF16),
        f((B, nt, tm // MOBA_BLOCK, MOBA_DIM), F32),
    )
    row = lambda w_: pl.BlockSpec((tm, w_), lambda i: (i, 0))
    const = lambda shape: pl.BlockSpec(shape, lambda i: (0,) * len(shape))
    in_specs = [
        row(D_MODEL),
        const((1, D_MODEL)),
        const((D_MODEL, C_TOTAL)),
        const((1, C_TOTAL)),
        pl.BlockSpec((tm, 128), lambda i: (i % nt, 0)),
        pl.BlockSpec((tm, 128), lambda i: (i % nt, 0)),
    ]
    out_specs = (
        row(256),
        row(128),
        row(128),
        pl.BlockSpec((None, NSA_DIM, tm), lambda i: (i // nt, 0, i % nt)),
        row(128),
        pl.BlockSpec((None, tm // KV_CHUNK, 128, KV_CHUNK), lambda i: (i // nt, i % nt, 0, 0)),
        row(128),
        pl.BlockSpec((None, tm // 128, 128, 128), lambda i: (i // nt, i % nt, 0, 0)),
        pl.BlockSpec((None, 128, tm), lambda i: (i // nt, 0, i % nt)),
        pl.BlockSpec((None, MOBA_DIM, tm), lambda i: (i // nt, 0, i % nt)),
        row(MOBA_DIM),
        pl.BlockSpec((None, tm // KV_CHUNK, MOBA_DIM, KV_CHUNK), lambda i: (i // nt, i % nt, 0, 0)),
        pl.BlockSpec((None, None, tm // MOBA_BLOCK, MOBA_DIM), lambda i: (i // nt, i % nt, 0, 0)),
    )
    return pl.pallas_call(
        _proj_kernel,
        out_shape=out_shape,
        grid=(N // tm,),
        in_specs=in_specs,
        out_specs=out_specs,
        compiler_params=pltpu.CompilerParams(
            dimension_semantics=("arbitrary",), vmem_limit_bytes=VMEM_LIMIT),
        name="proj",
    )(x2, gamma, w, bias, cos_t, sin_t)


def _gelu_tanh(x):
    return x * (0.5 * (1.0 + jnp.tanh(np.sqrt(2.0 / np.pi).astype(np.float32) * (x + 0.044715 * (x * x * x)))))


def _compress_kernel(xk_ref, xv_ref, pos_ref, w1_ref, b1_ref, w2_ref, b2_ref, kc_ref, vct_ref):
    R = xk_ref.shape[0] // CMP_STRIDE
    top = [None] * 4
    bot = [None] * 4
    for l in range(CMP_STRIDE):
        xl = [r[pl.ds(l, R, stride=CMP_STRIDE), :] for r in (xk_ref, xv_ref)]
        for j in range(4):
            t, g = divmod(j, NSA_KV_GROUPS)
            piece = xl[t][:, HEAD_DIM * g:HEAD_DIM * (g + 1)]
            lo = (piece + pos_ref[t, l:l + 1, :]).astype(BF16)
            hi = (piece + pos_ref[t, CMP_STRIDE + l:CMP_STRIDE + l + 1, :]).astype(BF16)
            a = _dot(lo, w1_ref[t, HEAD_DIM * l:HEAD_DIM * (l + 1), :])
            b = _dot(hi, w1_ref[t, HEAD_DIM * (CMP_STRIDE + l):HEAD_DIM * (CMP_STRIDE + l + 1), :])
            top[j] = a if top[j] is None else top[j] + a
            bot[j] = b if bot[j] is None else bot[j] + b
    outs = []
    for j in range(4):
        t = j // NSA_KV_GROUPS
        hid = _gelu_tanh(top[j] + pltpu.roll(bot[j], R - 1, 0) + b1_ref[t])
        outs.append(_dot(hid.astype(BF16), w2_ref[t]) + b2_ref[t])
    kc_ref[...] = jnp.concatenate(outs[0:2], axis=1).astype(BF16)
    vct_ref[...] = jnp.concatenate(outs[2:4], axis=1).T.astype(BF16)


def _compress_call(xk, xv, pos, w1, b1, w2, b2, B, R):
    S = R * CMP_STRIDE
    return pl.pallas_call(
        _compress_kernel,
        out_shape=(jax.ShapeDtypeStruct((B, R, 128), BF16), jax.ShapeDtypeStruct((B, 128, R), BF16)),
        grid=(B,),
        in_specs=[
            pl.BlockSpec((S, 128), lambda b: (b, 0)),
            pl.BlockSpec((S, 128), lambda b: (b, 0)),
            pl.BlockSpec((2, CMP_BLOCK, HEAD_DIM), lambda b: (0, 0, 0)),
            pl.BlockSpec((2, CMP_BLOCK * HEAD_DIM, CMP_HIDDEN), lambda b: (0, 0, 0)),
            pl.BlockSpec((2, 1, CMP_HIDDEN), lambda b: (0, 0, 0)),
            pl.BlockSpec((2, CMP_HIDDEN, HEAD_DIM), lambda b: (0, 0, 0)),
            pl.BlockSpec((2, 1, HEAD_DIM), lambda b: (0, 0, 0)),
        ],
        out_specs=(pl.BlockSpec((None, R, 128), lambda b: (b, 0, 0)),
                   pl.BlockSpec((None, 128, R), lambda b: (b, 0, 0))),
        compiler_params=pltpu.CompilerParams(
            dimension_semantics=("arbitrary",), vmem_limit_bytes=VMEM_LIMIT),
        name="nsa_compress",
    )(xk, xv, pos, w1, b1, w2, b2)


def _softmax_block(s):
    m = jnp.max(s, axis=0, keepdims=True)
    m_use = jnp.where(m < 0.5 * NEG_INF, 0.0, m)
    p = jnp.exp2(s - m_use)
    return p, jnp.sum(p, axis=0, keepdims=True)


def _flash_update(sc, rows, block, vt, m, acc_ref):
    n, L = sc.shape[0] // block, sc.shape[1]
    parts = [sc[block * j:block * (j + 1), :] for j in range(n)]
    tops = [jnp.max(parts[j].reshape(block // 8, 8, L), axis=0) for j in range(n)]
    if rows is not None:
        tops = [tops[j] + rows[j:j + 1, :] for j in range(n)]
    top = tops[0]
    for t in tops[1:]:
        top = jnp.maximum(top, t)
    m_new = jnp.maximum(m, jnp.max(top, axis=0, keepdims=True))
    m_use = jnp.where(m_new < 0.5 * NEG_INF, 0.0, m_new)
    alpha = jnp.exp2(m - m_use)
    if rows is None:
        p = jnp.exp2(sc - m_use)
    else:
        p = jnp.concatenate([jnp.exp2(parts[j] + (rows[j:j + 1, :] - m_use)) for j in range(n)], axis=0)
    acc_ref[...] = alpha * acc_ref[...] + _dot(vt, p.astype(BF16))
    return m_new


def _run_steps(steps, pending, tail):
    pending = list(pending)
    fns = [s for s, _ in steps] + list(tail)
    for t, (_, consume) in enumerate(steps):
        s = pending.pop(0)
        if t + LOOKAHEAD < len(fns):
            pending.append(fns[t + LOOKAHEAD]())
        consume(s)
    return tuple(pending)


def _chunk_loop(body, n, carry, unroll):
    def multi(i, cr):
        for j in range(unroll):
            cr = body(unroll * i + j, cr)
        return cr

    carry = lax.fori_loop(0, n // unroll, multi, carry)
    return lax.fori_loop(unroll * (n // unroll), n, body, carry)


def _mask_bias(ok, reps):
    return jnp.concatenate([jnp.where(ok, 0.0, NEG_INF)] * reps, axis=1)


def _topk_mask(scores, rows_f, k):
    work = scores
    for _ in range(k):
        cm = jnp.max(work, axis=0, keepdims=True)
        idx = jnp.min(jnp.where(work == cm, rows_f, 1e9), axis=0, keepdims=True)
        work = jnp.where(rows_f == idx, REMOVED, work)
    return jnp.where(work == REMOVED, 1.0, 0.0)


def _nsa_kernel(qt_ref, gt_ref, kc_ref, vct_ref, ksel_ref, vselt_ref, kwin_ref, vwint_ref, out_ref,
                qt_s, pg_s, bias_s, acc_s, accw_s, ocmp_s):
    qi = pl.program_id(1)
    q0 = qi * Q_TILE
    G = NSA_KV_GROUPS
    HPL = LANES // Q_TILE
    PARTS = NSA_HPG // HPL
    NLG = G * PARTS
    n_cmp_rows = kc_ref.shape[0]
    n_sel = bias_s.shape[1]
    group_of = lambda lg: lg // PARTS
    rows_of = lambda g: slice(HEAD_DIM * g, HEAD_DIM * (g + 1))

    qt_s[...] = jnp.zeros_like(qt_s)
    for h in range(NSA_HEADS):
        g, hh = divmod(h, NSA_HPG)
        part, hl = divmod(hh, HPL)
        qt_s[g * PARTS + part, rows_of(g), Q_TILE * hl:Q_TILE * (hl + 1)] = qt_ref[HEAD_DIM * h:HEAD_DIM * (h + 1), :]
    qts = [qt_s[lg] for lg in range(NLG)]
    tq1 = q0 + lax.broadcasted_iota(jnp.int32, (1, Q_TILE), 1)

    def ones_under(vt):
        return jnp.concatenate([vt, jnp.ones((ONES_ROWS, vt.shape[1]), BF16)], axis=0)

    def cmp_branch(nr):
        n_idx = lax.broadcasted_iota(jnp.int32, (nr, 1), 0)
        cbias = _mask_bias(n_idx * CMP_STRIDE + (CMP_BLOCK - 1) <= tq1, HPL)
        for g in range(G):
            pg = None
            for part in range(PARTS):
                lg = g * PARTS + part
                s = jnp.concatenate([_dot(kc_ref[r:min(r + KV_SUB, nr), :], qts[lg])
                                     for r in range(0, nr, KV_SUB)], axis=0) + cbias
                p, l = _softmax_block(s)
                den = jnp.maximum(l, 1e-30)
                ocmp_s[lg] = _dot(vct_ref[rows_of(g), 0:nr], p.astype(BF16)) / den
                pn = p / den
                for hl in range(HPL):
                    ph = pn[:, hl * Q_TILE:(hl + 1) * Q_TILE]
                    pg = ph if pg is None else pg + ph
            for sl in range(Q_TILE // 128):
                slab = g * (Q_TILE // 128) + sl
                pg_s[slab, 0:8, :] = jnp.zeros((8, 128), F32)
                pg_s[slab, 8:8 + nr, :] = pg[:, 128 * sl:128 * (sl + 1)]
                if nr < n_cmp_rows:
                    pg_s[slab, 8 + nr:8 + n_cmp_rows, :] = jnp.zeros((n_cmp_rows - nr, 128), F32)

    n_visible = (q0 + Q_TILE - CMP_BLOCK) // CMP_STRIDE + 1
    sizes = list(range(128, n_cmp_rows + 1, 128))
    for i, nr in enumerate(sizes):
        lo_ok = n_visible > sizes[i - 1] if i > 0 else True
        hi_ok = n_visible <= nr if i + 1 < len(sizes) else True
        pl.when(jnp.logical_and(lo_ok, hi_ok))(functools.partial(cmp_branch, nr))

    W = WINDOW + Q_TILE
    wstart = jnp.maximum(q0 - WINDOW, 0)
    wc0 = wstart // 128
    kposw = wstart + lax.broadcasted_iota(jnp.int32, (W, 1), 0)
    wbias = _mask_bias((kposw <= tq1) & (kposw > tq1 - WINDOW), HPL)
    vtw = jnp.concatenate([vwint_ref[wc0 + j] for j in range(W // 128)], axis=1)
    accw_s[...] = jnp.zeros_like(accw_s)
    mw = [jnp.full((1, LANES), NEG_INF, F32)] * NLG
    win_steps = []
    for r0 in range(0, W, KV_SUB):
        r1 = min(r0 + KV_SUB, W)
        for lg in range(NLG):
            def score(r0=r0, r1=r1, lg=lg):
                k = kwin_ref[pl.ds(pl.multiple_of(wstart + r0, 128), r1 - r0), :]
                return _dot(k, qts[lg]) + wbias[r0:r1, :]

            def consume(s, r0=r0, r1=r1, lg=lg):
                vt = ones_under(vtw[rows_of(group_of(lg)), r0:r1])
                mw[lg] = _flash_update(s, None, r1 - r0, vt, mw[lg], accw_s.at[lg])

            win_steps.append((score, consume))
    _run_steps(win_steps, [fn() for fn, _ in win_steps[:LOOKAHEAD]], [])

    blk_i = lax.broadcasted_iota(jnp.int32, (n_sel, 1), 0)
    cur = tq1 // SEL_BLOCK
    forced = (blk_i == 0) | (blk_i == cur) | (blk_i == cur - 1)
    started = blk_i <= cur
    n_top = min(SEL_COUNT, n_sel)
    slc = []
    for g in range(G):
        slabs = []
        for sl in range(Q_TILE // 128):
            def tap(w):
                return pg_s[g * (Q_TILE // 128) + sl, pl.ds(8 + w, n_sel, stride=SEL_BLOCK // CMP_STRIDE), :]

            t = tap(-1) + 2.0 * tap(0)
            t = t + 2.0 * tap(1)
            t = t + 2.0 * tap(2)
            t = t + tap(3)
            slabs.append(t)
        t = slabs[0] if len(slabs) == 1 else jnp.concatenate(slabs, axis=1)
        slc.append(jnp.where(forced, NEG_INF, jnp.where(started, t, NEG_INF)))
    free = _topk_mask(jnp.concatenate(slc, axis=1), jnp.broadcast_to(blk_i.astype(F32), (n_sel, G * Q_TILE)),
                      n_top - N_FORCED)
    for g in range(G):
        bias = jnp.where(forced, 0.0, jnp.where(free[:, Q_TILE * g:Q_TILE * (g + 1)] > 0.5, 0.0, NEG_INF))
        bias_s[g] = jnp.concatenate([bias] * HPL, axis=1)

    bpc = KV_CHUNK // SEL_BLOCK
    bps = KV_SUB // SEL_BLOCK
    c_last = q0 // KV_CHUNK

    def chunk_steps(c, ms, causal):
        steps = []
        for sub in range(KV_CHUNK // KV_SUB):
            for lg in range(NLG):
                def score(sub=sub, lg=lg):
                    base = pl.multiple_of(c * KV_CHUNK, KV_CHUNK) + sub * KV_SUB
                    s = _dot(ksel_ref[pl.ds(pl.multiple_of(base, KV_SUB), KV_SUB), :], qts[lg])
                    return s if causal is None else s + causal[sub]

                def consume(s, sub=sub, lg=lg):
                    g = group_of(lg)
                    rows = bias_s[g, pl.ds(pl.multiple_of(c * bpc, bpc), bpc), :][bps * sub:bps * (sub + 1), :]
                    vt = vselt_ref[c, rows_of(g), KV_SUB * sub:KV_SUB * (sub + 1)]
                    ms[lg] = _flash_update(s, rows, SEL_BLOCK, ones_under(vt), ms[lg], acc_s.at[lg])

                steps.append((score, consume))
        return steps

    acc_s[...] = jnp.zeros_like(acc_s)

    def body(c, carry):
        ms = list(carry[:NLG])
        nxt = [fn for fn, _ in chunk_steps(c + 1, ms, None)[:LOOKAHEAD]]
        pending = _run_steps(chunk_steps(c, ms, None), carry[NLG:], nxt)
        return tuple(ms) + pending

    first = [fn() for fn, _ in chunk_steps(0, None, None)[:LOOKAHEAD]]
    carry = (jnp.full((1, LANES), NEG_INF, F32),) * NLG + tuple(first)
    carry = _chunk_loop(body, c_last, carry, NSA_TRIP_UNROLL)
    kpos = c_last * KV_CHUNK + lax.broadcasted_iota(jnp.int32, (KV_CHUNK, 1), 0)
    causal = _mask_bias(kpos <= tq1, HPL)
    n_sub = KV_CHUNK // KV_SUB
    needed = ((q0 % KV_CHUNK) + Q_TILE + KV_SUB - 1) // KV_SUB

    def run_diag(n_pieces):
        steps = chunk_steps(c_last, list(carry[:NLG]), [causal[KV_SUB * sub:KV_SUB * (sub + 1), :]
                                                       for sub in range(n_sub)])[:n_pieces * NLG]
        _run_steps(steps, [fn() for fn, _ in steps[:LOOKAHEAD]], [])

    for n_pieces in range(1, n_sub + 1):
        cond = needed == n_pieces if n_pieces < n_sub else needed >= n_pieces
        pl.when(cond)(functools.partial(run_diag, n_pieces))

    gates = jax.nn.sigmoid(gt_ref[0:3 * NSA_HEADS, :])
    outs = []
    for h in range(NSA_HEADS):
        g, hh = divmod(h, NSA_HPG)
        part, hl = divmod(hh, HPL)
        lg = g * PARTS + part
        cols = slice(Q_TILE * hl, Q_TILE * (hl + 1))
        o_sel = acc_s[lg, 0:HEAD_DIM, cols] / jnp.maximum(acc_s[lg, HEAD_DIM:HEAD_DIM + 1, cols], 1e-30)
        o_win = accw_s[lg, 0:HEAD_DIM, cols] / jnp.maximum(accw_s[lg, HEAD_DIM:HEAD_DIM + 1, cols], 1e-30)
        o = gates[3 * h:3 * h + 1, :] * ocmp_s[lg, :, cols]
        o = o + gates[3 * h + 1:3 * h + 2, :] * o_sel
        o = o + gates[3 * h + 2:3 * h + 3, :] * o_win
        outs.append(o)
    out_ref[...] = jnp.concatenate(outs, axis=0).T.astype(BF16)


def _nsa_call(qtn, gt, kc, vct, ksel, vselt, kwin, vwint, B, S):
    R = S // CMP_STRIDE
    n_sel = S // SEL_BLOCK
    nlg = NSA_HEADS * Q_TILE // LANES
    return pl.pallas_call(
        _nsa_kernel,
        out_shape=jax.ShapeDtypeStruct((B * S, NSA_DIM), BF16),
        grid=(B, S // Q_TILE),
        in_specs=[
            pl.BlockSpec((None, NSA_DIM, Q_TILE), lambda b, q: (b, 0, q)),
            pl.BlockSpec((None, 128, Q_TILE), lambda b, q: (b, 0, q)),
            pl.BlockSpec((None, R, 128), lambda b, q: (b, 0, 0)),
            pl.BlockSpec((None, 128, R), lambda b, q: (b, 0, 0)),
            pl.BlockSpec((S, 128), lambda b, q: (b, 0)),
            pl.BlockSpec((None, S // KV_CHUNK, 128, KV_CHUNK), lambda b, q: (b, 0, 0, 0)),
            pl.BlockSpec((S, 128), lambda b, q: (b, 0)),
            pl.BlockSpec((None, S // 128, 128, 128), lambda b, q: (b, 0, 0, 0)),
        ],
        out_specs=pl.BlockSpec((Q_TILE, NSA_DIM), lambda b, q: (b * (S // Q_TILE) + q, 0)),
        scratch_shapes=[
            pltpu.VMEM((nlg, 128, LANES), BF16),
            pltpu.VMEM((NSA_KV_GROUPS * (Q_TILE // 128), 8 + R, 128), F32),
            pltpu.VMEM((NSA_KV_GROUPS, n_sel, LANES), F32),
            pltpu.VMEM((nlg, HEAD_DIM + ONES_ROWS, LANES), F32),
            pltpu.VMEM((nlg, HEAD_DIM + ONES_ROWS, LANES), F32),
            pltpu.VMEM((nlg, HEAD_DIM, LANES), F32),
        ],
        compiler_params=pltpu.CompilerParams(
            dimension_semantics=("arbitrary", "arbitrary"), vmem_limit_bytes=VMEM_LIMIT),
        name="nsa_attn",
    )(qtn, gt, kc, vct, ksel, vselt, kwin, vwint)


def _moba_kernel(qt_ref, kmean_ref, k_ref, vt_ref, out_ref, qt_s, bias_s, acc_s):
    qi = pl.program_id(1)
    q0 = qi * MOBA_Q_TILE
    L = MOBA_Q_TILE
    H = MOBA_HEADS
    n_blk = kmean_ref.shape[0]
    n_top = max(1, min(MOBA_TOPK, n_blk - 1))

    qt_s[...] = jnp.zeros_like(qt_s)
    for h in range(H):
        rows = slice(HEAD_DIM * h, HEAD_DIM * (h + 1))
        qt_s[h, rows, :] = qt_ref[rows, :]
    qts = [qt_s[h] for h in range(H)]
    tq = q0 + lax.broadcasted_iota(jnp.int32, (1, L), 1)

    own = tq // MOBA_BLOCK
    blk_i = lax.broadcasted_iota(jnp.int32, (n_blk, 1), 0)
    blk_f = jnp.broadcast_to(blk_i.astype(F32), (n_blk, L))
    kmean = kmean_ref[...].astype(BF16)
    for h in range(H):
        sg = jnp.where(blk_i < own, _dot(kmean, qts[h]), NEG_INF)
        sel = _topk_mask(sg, blk_f, n_top)
        allow = ((sel > 0.5) & (blk_i < own)) | (blk_i == own)
        bias = jnp.where(allow, 0.0, NEG_INF)
        for j in range(n_blk):
            bias_s[h, j] = jnp.broadcast_to(bias[j:j + 1, :], (8, L))

    bpc = KV_CHUNK // MOBA_BLOCK

    steps = [(j, h) for j in range(bpc) for h in range(H)]

    ones = jnp.ones((ONES_ROWS, MOBA_BLOCK), BF16)

    def scores(c, j, h, diagonal):
        base = pl.multiple_of(c * KV_CHUNK, KV_CHUNK) + j * MOBA_BLOCK
        k = k_ref[pl.ds(pl.multiple_of(base, MOBA_BLOCK), MOBA_BLOCK), :]
        s = _dot(k, qts[h])
        if diagonal:
            kpos = base + lax.broadcasted_iota(jnp.int32, (MOBA_BLOCK, 1), 0)
            s = jnp.where(kpos <= tq, s, NEG_INF)
        return s

    def chunk_steps(c, ms, pending, diagonal):
        ms, pending = list(ms), list(pending)
        for t, (j, h) in enumerate(steps):
            s = pending.pop(0)
            u = t + LOOKAHEAD
            if u < len(steps):
                pending.append(scores(c, *steps[u], diagonal))
            elif not diagonal:
                pending.append(scores(c + 1, *steps[u - len(steps)], False))
            vt = vt_ref[c, HEAD_DIM * h:HEAD_DIM * (h + 1), MOBA_BLOCK * j:MOBA_BLOCK * (j + 1)]
            ms[h] = _flash_update(s, bias_s[h, c * bpc + j], MOBA_BLOCK, jnp.concatenate([vt, ones], axis=0),
                                  ms[h], acc_s.at[h])
        return tuple(ms), tuple(pending)

    acc_s[...] = jnp.zeros_like(acc_s)
    c_last = q0 // KV_CHUNK
    init = (jnp.full((1, L), NEG_INF, F32),) * H

    def body(c, carry):
        ms, pending = chunk_steps(c, carry[:H], carry[H:], False)
        return ms + pending

    carry = init + tuple(scores(0, *steps[i], False) for i in range(LOOKAHEAD))
    carry = _chunk_loop(body, c_last, carry, MOBA_TRIP_UNROLL)
    chunk_steps(c_last, carry[:H], [scores(c_last, *steps[i], True) for i in range(LOOKAHEAD)], True)
    outs = [acc_s[h, 0:HEAD_DIM, :] / jnp.maximum(acc_s[h, HEAD_DIM:HEAD_DIM + 1, :], 1e-30) for h in range(H)]
    out_ref[...] = jnp.concatenate(outs, axis=0).T.astype(BF16)


def _moba_call(qtm, kmean, km, vtm, B, S):
    n_blk = S // MOBA_BLOCK
    L = MOBA_Q_TILE
    return pl.pallas_call(
        _moba_kernel,
        out_shape=jax.ShapeDtypeStruct((B * S, MOBA_DIM), BF16),
        grid=(B, S // L),
        in_specs=[
            pl.BlockSpec((None, MOBA_DIM, L), lambda b, q: (b, 0, q)),
            pl.BlockSpec((None, n_blk, MOBA_DIM), lambda b, q: (b, 0, 0)),
            pl.BlockSpec((S, MOBA_DIM), lambda b, q: (b, 0)),
            pl.BlockSpec((None, S // KV_CHUNK, MOBA_DIM, KV_CHUNK), lambda b, q: (b, 0, 0, 0)),
        ],
        out_specs=pl.BlockSpec((L, MOBA_DIM), lambda b, q: (b * (S // L) + q, 0)),
        scratch_shapes=[
            pltpu.VMEM((MOBA_HEADS, MOBA_DIM, L), BF16),
            pltpu.VMEM((MOBA_HEADS, n_blk, 8, L), F32),
            pltpu.VMEM((MOBA_HEADS, HEAD_DIM + ONES_ROWS, L), F32),
        ],
        compiler_params=pltpu.CompilerParams(
            dimension_semantics=("arbitrary", "arbitrary"), vmem_limit_bytes=VMEM_LIMIT),
        name="moba_attn",
    )(qtm, kmean, km, vtm)


def _merge_kernel(u_ref, halo_ref, b_ref, c_ref, x_ref, gam_in_ref, wg_ref, bg_ref, pw_ref, ps_ref,
                  wa_ref, wb_ref, wc_ref, wo_ref, gam_ref, x1_ref, h2_ref, ext_s, *, tiles_per_seq):
    tm = u_ref.shape[0]
    i = pl.program_id(0)
    first = (i % tiles_per_seq) == 0
    u = u_ref[...]
    ext_s[0:POOL_MAXW, :] = jnp.where(first, 0.0, halo_ref[...])
    ext_s[POOL_MAXW:POOL_MAXW + tm, :] = u

    def tail_sum(col, k0, k1):
        tot = None
        for k in range(k0, k1):
            v = ext_s[pl.ds(POOL_MAXW - k, tm), 128 * col:128 * col + 128]
            tot = v if tot is None else tot + v
        return tot

    t_glob = (i % tiles_per_seq) * tm + lax.broadcasted_iota(jnp.int32, (tm, 1), 0)
    lane = lax.broadcasted_iota(jnp.int32, (1, 128), 1)
    low = lane < POOL_GROUP_DIM
    pooled = []
    for col in range(2):
        wa_, wb_ = POOL_WINDOWS[2 * col], POOL_WINDOWS[2 * col + 1]
        sa = tail_sum(col, 0, wa_)
        sb = sa + tail_sum(col, wa_, wb_)
        ca = jnp.minimum(t_glob + 1, wa_).astype(F32)
        cb = jnp.minimum(t_glob + 1, wb_).astype(F32)
        pooled.append(jnp.where(low, sa, sb) / jnp.where(low, ca, cb))
    d = jnp.concatenate(pooled, axis=1) - u
    a = _dot(d.astype(BF16), pw_ref[...]) * ps_ref[...]

    av = _dot(a.astype(BF16), wa_ref[...])
    bv = _dot(b_ref[...], wb_ref[...])
    cv = _dot(c_ref[...], wc_ref[...])
    x = x_ref[...]
    h = _rms(x, gam_in_ref[...]).astype(BF16)

    def gate(j):
        cols = slice(D_MODEL * j, D_MODEL * (j + 1))
        return jax.nn.sigmoid(_dot(h, wg_ref[:, cols]) + bg_ref[:, cols])

    merged = gate(0) * av
    merged = merged + gate(1) * bv
    merged = merged + gate(2) * cv
    x1 = x + _dot(merged.astype(BF16), wo_ref[...])
    x1_ref[...] = x1
    h2_ref[...] = _rms(x1, gam_ref[...]).astype(BF16)


def _merge_call(upool, bn, cm, x2, gamma_in, wg, bg, pw, ps, wa, wb, wc, wo, gamma, B, S):
    N = B * S
    tm = ROW_TILE
    nt = S // tm
    hb = tm // POOL_MAXW
    row = lambda w_: pl.BlockSpec((tm, w_), lambda i: (i, 0))
    const = lambda shape: pl.BlockSpec(shape, lambda i: (0,) * len(shape))
    return pl.pallas_call(
        functools.partial(_merge_kernel, tiles_per_seq=nt),
        out_shape=(jax.ShapeDtypeStruct((N, D_MODEL), F32), jax.ShapeDtypeStruct((N, D_MODEL), BF16)),
        grid=(N // tm,),
        in_specs=[
            row(POOL_DIM),
            pl.BlockSpec((POOL_MAXW, POOL_DIM), lambda i: (jnp.maximum(i * hb - 1, 0), 0)),
            row(NSA_DIM),
            row(MOBA_DIM),
            row(D_MODEL),
            const((1, D_MODEL)),
            const((D_MODEL, 3 * D_MODEL)),
            const((1, 3 * D_MODEL)),
            const((POOL_DIM, POOL_DIM)),
            const((1, POOL_DIM)),
            const((POOL_DIM, D_MODEL)),
            const((NSA_DIM, D_MODEL)),
            const((MOBA_DIM, D_MODEL)),
            const((D_MODEL, D_MODEL)),
            const((1, D_MODEL)),
        ],
        out_specs=(row(D_MODEL), row(D_MODEL)),
        scratch_shapes=[pltpu.VMEM((POOL_MAXW + tm, POOL_DIM), F32)],
        compiler_params=pltpu.CompilerParams(
            dimension_semantics=("arbitrary",), vmem_limit_bytes=VMEM_LIMIT),
        name="merge",
    )(upool, upool, bn, cm, x2, gamma_in, wg, bg, pw, ps, wa, wb, wc, wo, gamma)


def _ffn_kernel(h_ref, x_ref, wg_ref, wu_ref, wd_ref, gam_ref, out_ref, acc_s, *, final_norm):
    f = pl.program_id(1)

    @pl.when(f == 0)
    def _():
        acc_s[...] = jnp.zeros_like(acc_s)

    h = h_ref[...]
    g = _dot(h, wg_ref[...])
    u = _dot(h, wu_ref[...])
    act = (g * jax.nn.sigmoid(g)) * u
    acc_s[...] += _dot(act.astype(BF16), wd_ref[...])

    @pl.when(f == pl.num_programs(1) - 1)
    def _():
        y = x_ref[...] + acc_s[...]
        if final_norm:
            y = _rms(y, gam_ref[...])
        out_ref[...] = y


def _ffn_call(h2, x1, wg, wu, wd, gamma, final_norm):
    N = x1.shape[0]
    F = wg.shape[1]
    tm, tf = FFN_ROW_TILE, FFN_COL_TILE
    return pl.pallas_call(
        functools.partial(_ffn_kernel, final_norm=final_norm),
        out_shape=jax.ShapeDtypeStruct((N, D_MODEL), F32),
        grid=(N // tm, F // tf),
        in_specs=[
            pl.BlockSpec((tm, D_MODEL), lambda i, f: (i, 0)),
            pl.BlockSpec((tm, D_MODEL), lambda i, f: (i, 0)),
            pl.BlockSpec((D_MODEL, tf), lambda i, f: (0, f)),
            pl.BlockSpec((D_MODEL, tf), lambda i, f: (0, f)),
            pl.BlockSpec((tf, D_MODEL), lambda i, f: (f, 0)),
            pl.BlockSpec((1, D_MODEL), lambda i, f: (0, 0)),
        ],
        out_specs=pl.BlockSpec((tm, D_MODEL), lambda i, f: (i, 0)),
        scratch_shapes=[pltpu.VMEM((tm, D_MODEL), F32)],
        compiler_params=pltpu.CompilerParams(
            dimension_semantics=("arbitrary", "arbitrary"), vmem_limit_bytes=VMEM_LIMIT),
        name="ffn",
    )(h2, x1, wg, wu, wd, gamma)


def _reorder_in_proj(w, b):
    o_qn = POOL_DIM
    o_kv = o_qn + NSA_DIM
    o_gn = o_kv + 6 * 128
    o_mo = o_gn + 3 * NSA_HEADS
    o_gb = o_mo + 3 * MOBA_DIM
    pad = 128 - 3 * NSA_HEADS

    def pick(a):
        parts = [a[..., 0:o_qn], a[..., o_kv:o_kv + 256], a[..., o_qn:o_kv], a[..., o_kv + 256:o_gn],
                 a[..., o_gn:o_mo], jnp.zeros(a.shape[:-1] + (pad,), a.dtype), a[..., o_mo:o_gb]]
        return jnp.concatenate(parts, axis=-1)

    return pick(w), pick(b), w[..., o_gb:], b[..., o_gb:]


def _rope_tables(S):
    pos = jnp.arange(S, dtype=F32)
    inv_freq = ROPE_THETA ** (-jnp.arange(0, HEAD_DIM, 2, dtype=F32) / HEAD_DIM)
    ang = pos[:, None] * inv_freq[None, :]
    cos, sin = jnp.cos(ang), jnp.sin(ang)
    cos_t = jnp.tile(cos, (1, 4))
    sin_t = jnp.tile(jnp.concatenate([-sin, sin], axis=1), (1, 2))
    return cos_t, sin_t


def kernel(x, attn_norm, w_in, b_in, pool_w, pool_scale, cmp_pos, cmp_w1, cmp_b1, cmp_w2, cmp_b2,
           w_br_pool, w_br_nsa, w_br_moba, w_out, ffn_norm, w_gate, w_up, w_down, final_norm):
    B, S, D = x.shape
    depth = w_in.shape[0]
    assert D == D_MODEL and S % KV_CHUNK == 0 and S % ROW_TILE == 0 and S >= WINDOW + Q_TILE
    assert (B * S) % FFN_ROW_TILE == 0 and w_gate.shape[2] % FFN_COL_TILE == 0
    assert S % (128 * CMP_STRIDE) == 0 and S % MOBA_Q_TILE == 0 and S // SEL_BLOCK >= SEL_COUNT
    N = B * S
    R = S // CMP_STRIDE
    cos_t, sin_t = _rope_tables(S)
    x2 = x.reshape(N, D)
    for l in range(depth):
        w_all, b_all, w_gbr, b_gbr = _reorder_in_proj(w_in[l].astype(BF16), b_in[l])
        (upool, cmpk, cmpv, qtn, ksel, vselt, kwin, vwint, gt, qtm, km, vtm, kmean) = _proj_call(
            x2, attn_norm[l][None, :], w_all, b_all[None, :], cos_t, sin_t, B, S)

        kc, vct = _compress_call(
            cmpk, cmpv, cmp_pos[l], cmp_w1[l].astype(BF16),
            cmp_b1[l][:, None, :], cmp_w2[l].astype(BF16), cmp_b2[l][:, None, :], B, R)

        bn = _nsa_call(qtn, gt, kc, vct, ksel, vselt, kwin, vwint, B, S)
        cm = _moba_call(qtm, kmean.reshape(B, S // MOBA_BLOCK, MOBA_DIM), km, vtm, B, S)

        pw_bd = jax.scipy.linalg.block_diag(*[pool_w[l, g] for g in range(len(POOL_WINDOWS))])
        x1, h2 = _merge_call(
            upool, bn, cm, x2, attn_norm[l][None, :], w_gbr, b_gbr[None, :], pw_bd.astype(BF16), pool_scale[l][None, :],
            w_br_pool[l].astype(BF16), w_br_nsa[l].astype(BF16), w_br_moba[l].astype(BF16),
            w_out[l].astype(BF16), ffn_norm[l][None, :], B, S)

        x2 = _ffn_call(h2, x1, w_gate[l].astype(BF16), w_up[l].astype(BF16), w_down[l].astype(BF16),
                       final_norm[None, :], final_norm=(l == depth - 1))
    return x2.reshape(B, S, D)
```

```python
import functools

import numpy as np
import jax
import jax.numpy as jnp
from jax import lax
from jax.experimental import pallas as pl
from jax.experimental.pallas import tpu as pltpu

F32 = jnp.float32
BF16 = jnp.bfloat16

D_MODEL = 1024
HEAD_DIM = 64
ROPE_THETA = 10000.0
RMS_EPS = 1e-6
NEG_INF = -1e30
LOG2E = 1.4426950408889634
LOOKAHEAD = 2
NSA_TRIP_UNROLL = 2
MOBA_TRIP_UNROLL = 2
ONES_ROWS = 16
REMOVED = -(2.0 ** 127)

POOL_WINDOWS = (2, 4, 8, 16)
POOL_GROUP_DIM = 64
POOL_DIM = 256
POOL_MAXW = 16

NSA_HEADS = 8
NSA_KV_GROUPS = 2
NSA_HPG = 4
NSA_DIM = 512
CMP_BLOCK = 32
CMP_STRIDE = 16
CMP_HIDDEN = 256
SEL_BLOCK = 64
SEL_COUNT = 16
N_FORCED = 3
WINDOW = 512

MOBA_HEADS = 4
MOBA_DIM = 256
MOBA_BLOCK = 256
MOBA_TOPK = 3

Q_TILE = 256
LANES = 512
MOBA_Q_TILE = 512
KV_CHUNK = 512
KV_SUB = 256
ROW_TILE = 512
FFN_ROW_TILE = 1024
FFN_COL_TILE = 1408
VMEM_LIMIT = 56 * 1024 * 1024

C_POOL = 0
C_CMP = 256
C_QN = 512
C_KS = 1024
C_VS = 1152
C_KW = 1280
C_VW = 1408
C_GN = 1536
C_QM = 1664
C_KM = 1920
C_VM = 2176
C_TOTAL = 2432


def _dot(a, b):
    return jnp.dot(a, b, preferred_element_type=F32)


def _rms(x, gamma):
    return x * lax.rsqrt(jnp.mean(x * x, axis=-1, keepdims=True) + RMS_EPS) * gamma


def _proj_kernel(x_ref, gam_ref, w_ref, b_ref, cos_ref, sin_ref,
                 upool_ref, cmpk_ref, cmpv_ref, qtn_ref, ksel_ref, vselt_ref, kwin_ref, vwint_ref, gt_ref,
                 qtm_ref, km_ref, vtm_ref, kmean_ref):
    tm = x_ref.shape[0]
    h = _rms(x_ref[...], gam_ref[...]).astype(BF16)
    cos = cos_ref[...]
    sin = sin_ref[...]
    lane = lax.broadcasted_iota(jnp.int32, (tm, 128), 1)
    first_half = (lane & 32) == 0
    scale = HEAD_DIM ** -0.5 * LOG2E

    def seg(a, b):
        return _dot(h, w_ref[:, a:b]) + b_ref[:, a:b]

    def rope(y):
        swap = jnp.where(first_half, pltpu.roll(y, 96, 1), pltpu.roll(y, 32, 1))
        return y * cos + swap * sin

    y = seg(C_POOL, C_POOL + 512)
    upool_ref[...] = y[:, :256]
    cmpk_ref[...] = rope(y[:, 256:384])
    cmpv_ref[...] = y[:, 384:512]

    y = seg(C_QN, C_QN + 512)
    for j in range(4):
        q = rope(y[:, 128 * j:128 * j + 128]) * scale
        qtn_ref[128 * j:128 * j + 128, :] = q.T.astype(BF16)

    y = seg(C_KS, C_KS + 512)
    ksel_ref[...] = rope(y[:, 0:128]).astype(BF16)
    vselt_ref[0] = y[:, 128:256].T.astype(BF16)
    kwin_ref[...] = rope(y[:, 256:384]).astype(BF16)
    vwt = y[:, 384:512].T.astype(BF16)
    for j in range(tm // 128):
        vwint_ref[j] = vwt[:, 128 * j:128 * j + 128]

    y = seg(C_GN, C_GN + 128)
    gt_ref[...] = y.T

    y = seg(C_QM, C_QM + 256)
    for j in range(2):
        q = rope(y[:, 128 * j:128 * j + 128]) * scale
        qtm_ref[128 * j:128 * j + 128, :] = q.T.astype(BF16)

    y = seg(C_KM, C_KM + 256)
    km = jnp.concatenate([rope(y[:, 0:128]), rope(y[:, 128:256])], axis=1)
    km_ref[...] = km.astype(BF16)
    nblk = tm // MOBA_BLOCK
    means = [jnp.sum(km[MOBA_BLOCK * j:MOBA_BLOCK * (j + 1), :], axis=0, keepdims=True) * (1.0 / MOBA_BLOCK)
             for j in range(nblk)]
    kmean_ref[...] = jnp.concatenate(means, axis=0)

    y = seg(C_VM, C_VM + 256)
    vtm_ref[0] = y.T.astype(BF16)


def _proj_call(x2, gamma, w, bias, cos_t, sin_t, B, S):
    N = B * S
    tm = ROW_TILE
    nt = S // tm
    f = lambda shape, dt: jax.ShapeDtypeStruct(shape, dt)
    out_shape = (
        f((N, 256), F32),
        f((N, 128), F32),
        f((N, 128), F32),
        f((B, NSA_DIM, S), BF16),
        f((N, 128), BF16),
        f((B, S // KV_CHUNK, 128, KV_CHUNK), BF16),
        f((N, 128), BF16),
        f((B, S // 128, 128, 128), BF16),
        f((B, 128, S), F32),
        f((B, MOBA_DIM, S), BF16),
        f((N, MOBA_DIM), BF16),
        f((B, S // KV_CHUNK, MOBA_DIM, KV_CHUNK), BF16),
        f((B, nt, tm // MOBA_BLOCK, MOBA_DIM), F32),
    )
    row = lambda w_: pl.BlockSpec((tm, w_), lambda i: (i, 0))
    const = lambda shape: pl.BlockSpec(shape, lambda i: (0,) * len(shape))
    in_specs = [
        row(D_MODEL),
        const((1, D_MODEL)),
        const((D_MODEL, C_TOTAL)),
        const((1, C_TOTAL)),
        pl.BlockSpec((tm, 128), lambda i: (i % nt, 0)),
        pl.BlockSpec((tm, 128), lambda i: (i % nt, 0)),
    ]
    out_specs = (
        row(256),
        row(128),
        row(128),
        pl.BlockSpec((None, NSA_DIM, tm), lambda i: (i // nt, 0, i % nt)),
        row(128),
        pl.BlockSpec((None, tm // KV_CHUNK, 128, KV_CHUNK), lambda i: (i // nt, i % nt, 0, 0)),
        row(128),
        pl.BlockSpec((None, tm // 128, 128, 128), lambda i: (i // nt, i % nt, 0, 0)),
        pl.BlockSpec((None, 128, tm), lambda i: (i // nt, 0, i % nt)),
        pl.BlockSpec((None, MOBA_DIM, tm), lambda i: (i // nt, 0, i % nt)),
        row(MOBA_DIM),
        pl.BlockSpec((None, tm // KV_CHUNK, MOBA_DIM, KV_CHUNK), lambda i: (i // nt, i % nt, 0, 0)),
        pl.BlockSpec((None, None, tm // MOBA_BLOCK, MOBA_DIM), lambda i: (i // nt, i % nt, 0, 0)),
    )
    return pl.pallas_call(
        _proj_kernel,
        out_shape=out_shape,
        grid=(N // tm,),
        in_specs=in_specs,
        out_specs=out_specs,
        compiler_params=pltpu.CompilerParams(
            dimension_semantics=("arbitrary",), vmem_limit_bytes=VMEM_LIMIT),
        name="proj",
    )(x2, gamma, w, bias, cos_t, sin_t)


def _gelu_tanh(x):
    return x * (0.5 * (1.0 + jnp.tanh(np.sqrt(2.0 / np.pi).astype(np.float32) * (x + 0.044715 * (x * x * x)))))


def _compress_kernel(xk_ref, xv_ref, pos_ref, w1_ref, b1_ref, w2_ref, b2_ref, kc_ref, vct_ref):
    R = xk_ref.shape[0] // CMP_STRIDE
    top = [None] * 4
    bot = [None] * 4
    for l in range(CMP_STRIDE):
        xl = [r[pl.ds(l, R, stride=CMP_STRIDE), :] for r in (xk_ref, xv_ref)]
        for j in range(4):
            t, g = divmod(j, NSA_KV_GROUPS)
            piece = xl[t][:, HEAD_DIM * g:HEAD_DIM * (g + 1)]
            lo = (piece + pos_ref[t, l:l + 1, :]).astype(BF16)
            hi = (piece + pos_ref[t, CMP_STRIDE + l:CMP_STRIDE + l + 1, :]).astype(BF16)
            a = _dot(lo, w1_ref[t, HEAD_DIM * l:HEAD_DIM * (l + 1), :])
            b = _dot(hi, w1_ref[t, HEAD_DIM * (CMP_STRIDE + l):HEAD_DIM * (CMP_STRIDE + l + 1), :])
            top[j] = a if top[j] is None else top[j] + a
            bot[j] = b if bot[j] is None else bot[j] + b
    outs = []
    for j in range(4):
        t = j // NSA_KV_GROUPS
        hid = _gelu_tanh(top[j] + pltpu.roll(bot[j], R - 1, 0) + b1_ref[t])
        outs.append(_dot(hid.astype(BF16), w2_ref[t]) + b2_ref[t])
    kc_ref[...] = jnp.concatenate(outs[0:2], axis=1).astype(BF16)
    vct_ref[...] = jnp.concatenate(outs[2:4], axis=1).T.astype(BF16)


def _compress_call(xk, xv, pos, w1, b1, w2, b2, B, R):
    S = R * CMP_STRIDE
    return pl.pallas_call(
        _compress_kernel,
        out_shape=(jax.ShapeDtypeStruct((B, R, 128), BF16), jax.ShapeDtypeStruct((B, 128, R), BF16)),
        grid=(B,),
        in_specs=[
            pl.BlockSpec((S, 128), lambda b: (b, 0)),
            pl.BlockSpec((S, 128), lambda b: (b, 0)),
            pl.BlockSpec((2, CMP_BLOCK, HEAD_DIM), lambda b: (0, 0, 0)),
            pl.BlockSpec((2, CMP_BLOCK * HEAD_DIM, CMP_HIDDEN), lambda b: (0, 0, 0)),
            pl.BlockSpec((2, 1, CMP_HIDDEN), lambda b: (0, 0, 0)),
            pl.BlockSpec((2, CMP_HIDDEN, HEAD_DIM), lambda b: (0, 0, 0)),
            pl.BlockSpec((2, 1, HEAD_DIM), lambda b: (0, 0, 0)),
        ],
        out_specs=(pl.BlockSpec((None, R, 128), lambda b: (b, 0, 0)),
                   pl.BlockSpec((None, 128, R), lambda b: (b, 0, 0))),
        compiler_params=pltpu.CompilerParams(
            dimension_semantics=("arbitrary",), vmem_limit_bytes=VMEM_LIMIT),
        name="nsa_compress",
    )(xk, xv, pos, w1, b1, w2, b2)


def _softmax_block(s):
    m = jnp.max(s, axis=0, keepdims=True)
    m_use = jnp.where(m < 0.5 * NEG_INF, 0.0, m)
    p = jnp.exp2(s - m_use)
    return p, jnp.sum(p, axis=0, keepdims=True)


def _flash_update(sc, rows, block, vt, m, acc_ref):
    n, L = sc.shape[0] // block, sc.shape[1]
    parts = [sc[block * j:block * (j + 1), :] for j in range(n)]
    tops = [jnp.max(parts[j].reshape(block // 8, 8, L), axis=0) for j in range(n)]
    if rows is not None:
        tops = [tops[j] + rows[j:j + 1, :] for j in range(n)]
    top = tops[0]
    for t in tops[1:]:
        top = jnp.maximum(top, t)
    m_new = jnp.maximum(m, jnp.max(top, axis=0, keepdims=True))
    m_use = jnp.where(m_new < 0.5 * NEG_INF, 0.0, m_new)
    alpha = jnp.exp2(m - m_use)
    if rows is None:
        p = jnp.exp2(sc - m_use)
    else:
        p = jnp.concatenate([jnp.exp2(parts[j] + (rows[j:j + 1, :] - m_use)) for j in range(n)], axis=0)
    acc_ref[...] = alpha * acc_ref[...] + _dot(vt, p.astype(BF16))
    return m_new


def _run_steps(steps, pending, tail):
    pending = list(pending)
    fns = [s for s, _ in steps] + list(tail)
    for t, (_, consume) in enumerate(steps):
        s = pending.pop(0)
        if t + LOOKAHEAD < len(fns):
            pending.append(fns[t + LOOKAHEAD]())
        consume(s)
    return tuple(pending)


def _chunk_loop(body, n, carry, unroll):
    def multi(i, cr):
        for j in range(unroll):
            cr = body(unroll * i + j, cr)
        return cr

    carry = lax.fori_loop(0, n // unroll, multi, carry)
    return lax.fori_loop(unroll * (n // unroll), n, body, carry)


def _mask_bias(ok, reps):
    return jnp.concatenate([jnp.where(ok, 0.0, NEG_INF)] * reps, axis=1)


def _topk_mask(scores, rows_f, k):
    work = scores
    for _ in range(k):
        cm = jnp.max(work, axis=0, keepdims=True)
        idx = jnp.min(jnp.where(work == cm, rows_f, 1e9), axis=0, keepdims=True)
        work = jnp.where(rows_f == idx, REMOVED, work)
    return jnp.where(work == REMOVED, 1.0, 0.0)


def _nsa_kernel(qt_ref, gt_ref, kc_ref, vct_ref, ksel_ref, vselt_ref, kwin_ref, vwint_ref, out_ref,
                qt_s, pg_s, bias_s, acc_s, accw_s, ocmp_s):
    qi = pl.program_id(1)
    q0 = qi * Q_TILE
    G = NSA_KV_GROUPS
    HPL = LANES // Q_TILE
    PARTS = NSA_HPG // HPL
    NLG = G * PARTS
    n_cmp_rows = kc_ref.shape[0]
    n_sel = bias_s.shape[1]
    group_of = lambda lg: lg // PARTS
    rows_of = lambda g: slice(HEAD_DIM * g, HEAD_DIM * (g + 1))

    qt_s[...] = jnp.zeros_like(qt_s)
    for h in range(NSA_HEADS):
        g, hh = divmod(h, NSA_HPG)
        part, hl = divmod(hh, HPL)
        qt_s[g * PARTS + part, rows_of(g), Q_TILE * hl:Q_TILE * (hl + 1)] = qt_ref[HEAD_DIM * h:HEAD_DIM * (h + 1), :]
    qts = [qt_s[lg] for lg in range(NLG)]
    tq1 = q0 + lax.broadcasted_iota(jnp.int32, (1, Q_TILE), 1)

    def ones_under(vt):
        return jnp.concatenate([vt, jnp.ones((ONES_ROWS, vt.shape[1]), BF16)], axis=0)

    def cmp_branch(nr):
        n_idx = lax.broadcasted_iota(jnp.int32, (nr, 1), 0)
        cbias = _mask_bias(n_idx * CMP_STRIDE + (CMP_BLOCK - 1) <= tq1, HPL)
        for g in range(G):
            pg = None
            for part in range(PARTS):
                lg = g * PARTS + part
                s = jnp.concatenate([_dot(kc_ref[r:min(r + KV_SUB, nr), :], qts[lg])
                                     for r in range(0, nr, KV_SUB)], axis=0) + cbias
                p, l = _softmax_block(s)
                den = jnp.maximum(l, 1e-30)
                ocmp_s[lg] = _dot(vct_ref[rows_of(g), 0:nr], p.astype(BF16)) / den
                pn = p / den
                for hl in range(HPL):
                    ph = pn[:, hl * Q_TILE:(hl + 1) * Q_TILE]
                    pg = ph if pg is None else pg + ph
            for sl in range(Q_TILE // 128):
                slab = g * (Q_TILE // 128) + sl
                pg_s[slab, 0:8, :] = jnp.zeros((8, 128), F32)
                pg_s[slab, 8:8 + nr, :] = pg[:, 128 * sl:128 * (sl + 1)]
                if nr < n_cmp_rows:
                    pg_s[slab, 8 + nr:8 + n_cmp_rows, :] = jnp.zeros((n_cmp_rows - nr, 128), F32)

    n_visible = (q0 + Q_TILE - CMP_BLOCK) // CMP_STRIDE + 1
    sizes = list(range(128, n_cmp_rows + 1, 128))
    for i, nr in enumerate(sizes):
        lo_ok = n_visible > sizes[i - 1] if i > 0 else True
        hi_ok = n_visible <= nr if i + 1 < len(sizes) else True
        pl.when(jnp.logical_and(lo_ok, hi_ok))(functools.partial(cmp_branch, nr))

    W = WINDOW + Q_TILE
    wstart = jnp.maximum(q0 - WINDOW, 0)
    wc0 = wstart // 128
    kposw = wstart + lax.broadcasted_iota(jnp.int32, (W, 1), 0)
    wbias = _mask_bias((kposw <= tq1) & (kposw > tq1 - WINDOW), HPL)
    vtw = jnp.concatenate([vwint_ref[wc0 + j] for j in range(W // 128)], axis=1)
    accw_s[...] = jnp.zeros_like(accw_s)
    mw = [jnp.full((1, LANES), NEG_INF, F32)] * NLG
    win_steps = []
    for r0 in range(0, W, KV_SUB):
        r1 = min(r0 + KV_SUB, W)
        for lg in range(NLG):
            def score(r0=r0, r1=r1, lg=lg):
                k = kwin_ref[pl.ds(pl.multiple_of(wstart + r0, 128), r1 - r0), :]
                return _dot(k, qts[lg]) + wbias[r0:r1, :]

            def consume(s, r0=r0, r1=r1, lg=lg):
                vt = ones_under(vtw[rows_of(group_of(lg)), r0:r1])
                mw[lg] = _flash_update(s, None, r1 - r0, vt, mw[lg], accw_s.at[lg])

            win_steps.append((score, consume))
    _run_steps(win_steps, [fn() for fn, _ in win_steps[:LOOKAHEAD]], [])

    blk_i = lax.broadcasted_iota(jnp.int32, (n_sel, 1), 0)
    cur = tq1 // SEL_BLOCK
    forced = (blk_i == 0) | (blk_i == cur) | (blk_i == cur - 1)
    started = blk_i <= cur
    n_top = min(SEL_COUNT, n_sel)
    slc = []
    for g in range(G):
        slabs = []
        for sl in range(Q_TILE // 128):
            def tap(w):
                return pg_s[g * (Q_TILE // 128) + sl, pl.ds(8 + w, n_sel, stride=SEL_BLOCK // CMP_STRIDE), :]

            t = tap(-1) + 2.0 * tap(0)
            t = t + 2.0 * tap(1)
            t = t + 2.0 * tap(2)
            t = t + tap(3)
            slabs.append(t)
        t = slabs[0] if len(slabs) == 1 else jnp.concatenate(slabs, axis=1)
        slc.append(jnp.where(forced, NEG_INF, jnp.where(started, t, NEG_INF)))
    free = _topk_mask(jnp.concatenate(slc, axis=1), jnp.broadcast_to(blk_i.astype(F32), (n_sel, G * Q_TILE)),
                      n_top - N_FORCED)
    for g in range(G):
        bias = jnp.where(forced, 0.0, jnp.where(free[:, Q_TILE * g:Q_TILE * (g + 1)] > 0.5, 0.0, NEG_INF))
        bias_s[g] = jnp.concatenate([bias] * HPL, axis=1)

    bpc = KV_CHUNK // SEL_BLOCK
    bps = KV_SUB // SEL_BLOCK
    c_last = q0 // KV_CHUNK

    def chunk_steps(c, ms, causal):
        steps = []
        for sub in range(KV_CHUNK // KV_SUB):
            for lg in range(NLG):
                def score(sub=sub, lg=lg):
                    base = pl.multiple_of(c * KV_CHUNK, KV_CHUNK) + sub * KV_SUB
                    s = _dot(ksel_ref[pl.ds(pl.multiple_of(base, KV_SUB), KV_SUB), :], qts[lg])
                    return s if causal is None else s + causal[sub]

                def consume(s, sub=sub, lg=lg):
                    g = group_of(lg)
                    rows = bias_s[g, pl.ds(pl.multiple_of(c * bpc, bpc), bpc), :][bps * sub:bps * (sub + 1), :]
                    vt = vselt_ref[c, rows_of(g), KV_SUB * sub:KV_SUB * (sub + 1)]
                    ms[lg] = _flash_update(s, rows, SEL_BLOCK, ones_under(vt), ms[lg], acc_s.at[lg])

                steps.append((score, consume))
        return steps

    acc_s[...] = jnp.zeros_like(acc_s)

    def body(c, carry):
        ms = list(carry[:NLG])
        nxt = [fn for fn, _ in chunk_steps(c + 1, ms, None)[:LOOKAHEAD]]
        pending = _run_steps(chunk_steps(c, ms, None), carry[NLG:], nxt)
        return tuple(ms) + pending

    first = [fn() for fn, _ in chunk_steps(0, None, None)[:LOOKAHEAD]]
    carry = (jnp.full((1, LANES), NEG_INF, F32),) * NLG + tuple(first)
    carry = _chunk_loop(body, c_last, carry, NSA_TRIP_UNROLL)
    kpos = c_last * KV_CHUNK + lax.broadcasted_iota(jnp.int32, (KV_CHUNK, 1), 0)
    causal = _mask_bias(kpos <= tq1, HPL)
    n_sub = KV_CHUNK // KV_SUB
    needed = ((q0 % KV_CHUNK) + Q_TILE + KV_SUB - 1) // KV_SUB

    def run_diag(n_pieces):
        steps = chunk_steps(c_last, list(carry[:NLG]), [causal[KV_SUB * sub:KV_SUB * (sub + 1), :]
                                                       for sub in range(n_sub)])[:n_pieces * NLG]
        _run_steps(steps, [fn() for fn, _ in steps[:LOOKAHEAD]], [])

    for n_pieces in range(1, n_sub + 1):
        cond = needed == n_pieces if n_pieces < n_sub else needed >= n_pieces
        pl.when(cond)(functools.partial(run_diag, n_pieces))

    gates = jax.nn.sigmoid(gt_ref[0:3 * NSA_HEADS, :])
    outs = []
    for h in range(NSA_HEADS):
        g, hh = divmod(h, NSA_HPG)
        part, hl = divmod(hh, HPL)
        lg = g * PARTS + part
        cols = slice(Q_TILE * hl, Q_TILE * (hl + 1))
        o_sel = acc_s[lg, 0:HEAD_DIM, cols] / jnp.maximum(acc_s[lg, HEAD_DIM:HEAD_DIM + 1, cols], 1e-30)
        o_win = accw_s[lg, 0:HEAD_DIM, cols] / jnp.maximum(accw_s[lg, HEAD_DIM:HEAD_DIM + 1, cols], 1e-30)
        o = gates[3 * h:3 * h + 1, :] * ocmp_s[lg, :, cols]
        o = o + gates[3 * h + 1:3 * h + 2, :] * o_sel
        o = o + gates[3 * h + 2:3 * h + 3, :] * o_win
        outs.append(o)
    out_ref[...] = jnp.concatenate(outs, axis=0).T.astype(BF16)


def _nsa_call(qtn, gt, kc, vct, ksel, vselt, kwin, vwint, B, S):
    R = S // CMP_STRIDE
    n_sel = S // SEL_BLOCK
    nlg = NSA_HEADS * Q_TILE // LANES
    return pl.pallas_call(
        _nsa_kernel,
        out_shape=jax.ShapeDtypeStruct((B * S, NSA_DIM), BF16),
        grid=(B, S // Q_TILE),
        in_specs=[
            pl.BlockSpec((None, NSA_DIM, Q_TILE), lambda b, q: (b, 0, q)),
            pl.BlockSpec((None, 128, Q_TILE), lambda b, q: (b, 0, q)),
            pl.BlockSpec((None, R, 128), lambda b, q: (b, 0, 0)),
            pl.BlockSpec((None, 128, R), lambda b, q: (b, 0, 0)),
            pl.BlockSpec((S, 128), lambda b, q: (b, 0)),
            pl.BlockSpec((None, S // KV_CHUNK, 128, KV_CHUNK), lambda b, q: (b, 0, 0, 0)),
            pl.BlockSpec((S, 128), lambda b, q: (b, 0)),
            pl.BlockSpec((None, S // 128, 128, 128), lambda b, q: (b, 0, 0, 0)),
        ],
        out_specs=pl.BlockSpec((Q_TILE, NSA_DIM), lambda b, q: (b * (S // Q_TILE) + q, 0)),
        scratch_shapes=[
            pltpu.VMEM((nlg, 128, LANES), BF16),
            pltpu.VMEM((NSA_KV_GROUPS * (Q_TILE // 128), 8 + R, 128), F32),
            pltpu.VMEM((NSA_KV_GROUPS, n_sel, LANES), F32),
            pltpu.VMEM((nlg, HEAD_DIM + ONES_ROWS, LANES), F32),
            pltpu.VMEM((nlg, HEAD_DIM + ONES_ROWS, LANES), F32),
            pltpu.VMEM((nlg, HEAD_DIM, LANES), F32),
        ],
        compiler_params=pltpu.CompilerParams(
            dimension_semantics=("arbitrary", "arbitrary"), vmem_limit_bytes=VMEM_LIMIT),
        name="nsa_attn",
    )(qtn, gt, kc, vct, ksel, vselt, kwin, vwint)


def _moba_kernel(qt_ref, kmean_ref, k_ref, vt_ref, out_ref, qt_s, bias_s, acc_s):
    qi = pl.program_id(1)
    q0 = qi * MOBA_Q_TILE
    L = MOBA_Q_TILE
    H = MOBA_HEADS
    n_blk = kmean_ref.shape[0]
    n_top = max(1, min(MOBA_TOPK, n_blk - 1))

    qt_s[...] = jnp.zeros_like(qt_s)
    for h in range(H):
        rows = slice(HEAD_DIM * h, HEAD_DIM * (h + 1))
        qt_s[h, rows, :] = qt_ref[rows, :]
    qts = [qt_s[h] for h in range(H)]
    tq = q0 + lax.broadcasted_iota(jnp.int32, (1, L), 1)

    own = tq // MOBA_BLOCK
    blk_i = lax.broadcasted_iota(jnp.int32, (n_blk, 1), 0)
    blk_f = jnp.broadcast_to(blk_i.astype(F32), (n_blk, L))
    kmean = kmean_ref[...].astype(BF16)
    for h in range(H):
        sg = jnp.where(blk_i < own, _dot(kmean, qts[h]), NEG_INF)
        sel = _topk_mask(sg, blk_f, n_top)
        allow = ((sel > 0.5) & (blk_i < own)) | (blk_i == own)
        bias = jnp.where(allow, 0.0, NEG_INF)
        for j in range(n_blk):
            bias_s[h, j] = jnp.broadcast_to(bias[j:j + 1, :], (8, L))

    bpc = KV_CHUNK // MOBA_BLOCK

    steps = [(j, h) for j in range(bpc) for h in range(H)]

    ones = jnp.ones((ONES_ROWS, MOBA_BLOCK), BF16)

    def scores(c, j, h, diagonal):
        base = pl.multiple_of(c * KV_CHUNK, KV_CHUNK) + j * MOBA_BLOCK
        k = k_ref[pl.ds(pl.multiple_of(base, MOBA_BLOCK), MOBA_BLOCK), :]
        s = _dot(k, qts[h])
        if diagonal:
            kpos = base + lax.broadcasted_iota(jnp.int32, (MOBA_BLOCK, 1), 0)
            s = jnp.where(kpos <= tq, s, NEG_INF)
        return s

    def chunk_steps(c, ms, pending, diagonal):
        ms, pending = list(ms), list(pending)
        for t, (j, h) in enumerate(steps):
            s = pending.pop(0)
            u = t + LOOKAHEAD
            if u < len(steps):
                pending.append(scores(c, *steps[u], diagonal))
            elif not diagonal:
                pending.append(scores(c + 1, *steps[u - len(steps)], False))
            vt = vt_ref[c, HEAD_DIM * h:HEAD_DIM * (h + 1), MOBA_BLOCK * j:MOBA_BLOCK * (j + 1)]
            ms[h] = _flash_update(s, bias_s[h, c * bpc + j], MOBA_BLOCK, jnp.concatenate([vt, ones], axis=0),
                                  ms[h], acc_s.at[h])
        return tuple(ms), tuple(pending)

    acc_s[...] = jnp.zeros_like(acc_s)
    c_last = q0 // KV_CHUNK
    init = (jnp.full((1, L), NEG_INF, F32),) * H

    def body(c, carry):
        ms, pending = chunk_steps(c, carry[:H], carry[H:], False)
        return ms + pending

    carry = init + tuple(scores(0, *steps[i], False) for i in range(LOOKAHEAD))
    carry = _chunk_loop(body, c_last, carry, MOBA_TRIP_UNROLL)
    chunk_steps(c_last, carry[:H], [scores(c_last, *steps[i], True) for i in range(LOOKAHEAD)], True)
    outs = [acc_s[h, 0:HEAD_DIM, :] / jnp.maximum(acc_s[h, HEAD_DIM:HEAD_DIM + 1, :], 1e-30) for h in range(H)]
    out_ref[...] = jnp.concatenate(outs, axis=0).T.astype(BF16)


def _moba_call(qtm, kmean, km, vtm, B, S):
    n_blk = S // MOBA_BLOCK
    L = MOBA_Q_TILE
    return pl.pallas_call(
        _moba_kernel,
        out_shape=jax.ShapeDtypeStruct((B * S, MOBA_DIM), BF16),
        grid=(B, S // L),
        in_specs=[
            pl.BlockSpec((None, MOBA_DIM, L), lambda b, q: (b, 0, q)),
            pl.BlockSpec((None, n_blk, MOBA_DIM), lambda b, q: (b, 0, 0)),
            pl.BlockSpec((S, MOBA_DIM), lambda b, q: (b, 0)),
            pl.BlockSpec((None, S // KV_CHUNK, MOBA_DIM, KV_CHUNK), lambda b, q: (b, 0, 0, 0)),
        ],
        out_specs=pl.BlockSpec((L, MOBA_DIM), lambda b, q: (b * (S // L) + q, 0)),
        scratch_shapes=[
            pltpu.VMEM((MOBA_HEADS, MOBA_DIM, L), BF16),
            pltpu.VMEM((MOBA_HEADS, n_blk, 8, L), F32),
            pltpu.VMEM((MOBA_HEADS, HEAD_DIM + ONES_ROWS, L), F32),
        ],
        compiler_params=pltpu.CompilerParams(
            dimension_semantics=("arbitrary", "arbitrary"), vmem_limit_bytes=VMEM_LIMIT),
        name="moba_attn",
    )(qtm, kmean, km, vtm)


def _merge_kernel(u_ref, halo_ref, b_ref, c_ref, x_ref, gam_in_ref, wg_ref, bg_ref, pw_ref, ps_ref,
                  wa_ref, wb_ref, wc_ref, wo_ref, gam_ref, x1_ref, h2_ref, ext_s, *, tiles_per_seq):
    tm = u_ref.shape[0]
    i = pl.program_id(0)
    first = (i % tiles_per_seq) == 0
    u = u_ref[...]
    ext_s[0:POOL_MAXW, :] = jnp.where(first, 0.0, halo_ref[...])
    ext_s[POOL_MAXW:POOL_MAXW + tm, :] = u

    def tail_sum(col, k0, k1):
        tot = None
        for k in range(k0, k1):
            v = ext_s[pl.ds(POOL_MAXW - k, tm), 128 * col:128 * col + 128]
            tot = v if tot is None else tot + v
        return tot

    t_glob = (i % tiles_per_seq) * tm + lax.broadcasted_iota(jnp.int32, (tm, 1), 0)
    lane = lax.broadcasted_iota(jnp.int32, (1, 128), 1)
    low = lane < POOL_GROUP_DIM
    pooled = []
    for col in range(2):
        wa_, wb_ = POOL_WINDOWS[2 * col], POOL_WINDOWS[2 * col + 1]
        sa = tail_sum(col, 0, wa_)
        sb = sa + tail_sum(col, wa_, wb_)
        ca = jnp.minimum(t_glob + 1, wa_).astype(F32)
        cb = jnp.minimum(t_glob + 1, wb_).astype(F32)
        pooled.append(jnp.where(low, sa, sb) / jnp.where(low, ca, cb))
    d = jnp.concatenate(pooled, axis=1) - u
    a = _dot(d.astype(BF16), pw_ref[...]) * ps_ref[...]

    av = _dot(a.astype(BF16), wa_ref[...])
    bv = _dot(b_ref[...], wb_ref[...])
    cv = _dot(c_ref[...], wc_ref[...])
    x = x_ref[...]
    h = _rms(x, gam_in_ref[...]).astype(BF16)

    def gate(j):
        cols = slice(D_MODEL * j, D_MODEL * (j + 1))
        return jax.nn.sigmoid(_dot(h, wg_ref[:, cols]) + bg_ref[:, cols])

    merged = gate(0) * av
    merged = merged + gate(1) * bv
    merged = merged + gate(2) * cv
    x1 = x + _dot(merged.astype(BF16), wo_ref[...])
    x1_ref[...] = x1
    h2_ref[...] = _rms(x1, gam_ref[...]).astype(BF16)


def _merge_call(upool, bn, cm, x2, gamma_in, wg, bg, pw, ps, wa, wb, wc, wo, gamma, B, S):
    N = B * S
    tm = ROW_TILE
    nt = S // tm
    hb = tm // POOL_MAXW
    row = lambda w_: pl.BlockSpec((tm, w_), lambda i: (i, 0))
    const = lambda shape: pl.BlockSpec(shape, lambda i: (0,) * len(shape))
    return pl.pallas_call(
        functools.partial(_merge_kernel, tiles_per_seq=nt),
        out_shape=(jax.ShapeDtypeStruct((N, D_MODEL), F32), jax.ShapeDtypeStruct((N, D_MODEL), BF16)),
        grid=(N // tm,),
        in_specs=[
            row(POOL_DIM),
            pl.BlockSpec((POOL_MAXW, POOL_DIM), lambda i: (jnp.maximum(i * hb - 1, 0), 0)),
            row(NSA_DIM),
            row(MOBA_DIM),
            row(D_MODEL),
            const((1, D_MODEL)),
            const((D_MODEL, 3 * D_MODEL)),
            const((1, 3 * D_MODEL)),
            const((POOL_DIM, POOL_DIM)),
            const((1, POOL_DIM)),
            const((POOL_DIM, D_MODEL)),
            const((NSA_DIM, D_MODEL)),
            const((MOBA_DIM, D_MODEL)),
            const((D_MODEL, D_MODEL)),
            const((1, D_MODEL)),
        ],
        out_specs=(row(D_MODEL), row(D_MODEL)),
        scratch_shapes=[pltpu.VMEM((POOL_MAXW + tm, POOL_DIM), F32)],
        compiler_params=pltpu.CompilerParams(
            dimension_semantics=("arbitrary",), vmem_limit_bytes=VMEM_LIMIT),
        name="merge",
    )(upool, upool, bn, cm, x2, gamma_in, wg, bg, pw, ps, wa, wb, wc, wo, gamma)


def _ffn_kernel(h_ref, x_ref, wg_ref, wu_ref, wd_ref, gam_ref, out_ref, acc_s, *, final_norm):
    f = pl.program_id(1)

    @pl.when(f == 0)
    def _():
        acc_s[...] = jnp.zeros_like(acc_s)

    h = h_ref[...]
    g = _dot(h, wg_ref[...])
    u = _dot(h, wu_ref[...])
    act = (g * jax.nn.sigmoid(g)) * u
    acc_s[...] += _dot(act.astype(BF16), wd_ref[...])

    @pl.when(f == pl.num_programs(1) - 1)
    def _():
        y = x_ref[...] + acc_s[...]
        if final_norm:
            y = _rms(y, gam_ref[...])
        out_ref[...] = y


def _ffn_call(h2, x1, wg, wu, wd, gamma, final_norm):
    N = x1.shape[0]
    F = wg.shape[1]
    tm, tf = FFN_ROW_TILE, FFN_COL_TILE
    return pl.pallas_call(
        functools.partial(_ffn_kernel, final_norm=final_norm),
        out_shape=jax.ShapeDtypeStruct((N, D_MODEL), F32),
        grid=(N // tm, F // tf),
        in_specs=[
            pl.BlockSpec((tm, D_MODEL), lambda i, f: (i, 0)),
            pl.BlockSpec((tm, D_MODEL), lambda i, f: (i, 0)),
            pl.BlockSpec((D_MODEL, tf), lambda i, f: (0, f)),
            pl.BlockSpec((D_MODEL, tf), lambda i, f: (0, f)),
            pl.BlockSpec((tf, D_MODEL), lambda i, f: (f, 0)),
            pl.BlockSpec((1, D_MODEL), lambda i, f: (0, 0)),
        ],
        out_specs=pl.BlockSpec((tm, D_MODEL), lambda i, f: (i, 0)),
        scratch_shapes=[pltpu.VMEM((tm, D_MODEL), F32)],
        compiler_params=pltpu.CompilerParams(
            dimension_semantics=("arbitrary", "arbitrary"), vmem_limit_bytes=VMEM_LIMIT),
        name="ffn",
    )(h2, x1, wg, wu, wd, gamma)


def _reorder_in_proj(w, b):
    o_qn = POOL_DIM
    o_kv = o_qn + NSA_DIM
    o_gn = o_kv + 6 * 128
    o_mo = o_gn + 3 * NSA_HEADS
    o_gb = o_mo + 3 * MOBA_DIM
    pad = 128 - 3 * NSA_HEADS

    def pick(a):
        parts = [a[..., 0:o_qn], a[..., o_kv:o_kv + 256], a[..., o_qn:o_kv], a[..., o_kv + 256:o_gn],
                 a[..., o_gn:o_mo], jnp.zeros(a.shape[:-1] + (pad,), a.dtype), a[..., o_mo:o_gb]]
        return jnp.concatenate(parts, axis=-1)

    return pick(w), pick(b), w[..., o_gb:], b[..., o_gb:]


def _rope_tables(S):
    pos = jnp.arange(S, dtype=F32)
    inv_freq = ROPE_THETA ** (-jnp.arange(0, HEAD_DIM, 2, dtype=F32) / HEAD_DIM)
    ang = pos[:, None] * inv_freq[None, :]
    cos, sin = jnp.cos(ang), jnp.sin(ang)
    cos_t = jnp.tile(cos, (1, 4))
    sin_t = jnp.tile(jnp.concatenate([-sin, sin], axis=1), (1, 2))
    return cos_t, sin_t


def kernel(x, attn_norm, w_in, b_in, pool_w, pool_scale, cmp_pos, cmp_w1, cmp_b1, cmp_w2, cmp_b2,
           w_br_pool, w_br_nsa, w_br_moba, w_out, ffn_norm, w_gate, w_up, w_down, final_norm):
    B, S, D = x.shape
    depth = w_in.shape[0]
    assert D == D_MODEL and S % KV_CHUNK == 0 and S % ROW_TILE == 0 and S >= WINDOW + Q_TILE
    assert (B * S) % FFN_ROW_TILE == 0 and w_gate.shape[2] % FFN_COL_TILE == 0
    assert S % (128 * CMP_STRIDE) == 0 and S % MOBA_Q_TILE == 0 and S // SEL_BLOCK >= SEL_COUNT
    N = B * S
    R = S // CMP_STRIDE
    cos_t, sin_t = _rope_tables(S)
    x2 = x.reshape(N, D)
    for l in range(depth):
        w_all, b_all, w_gbr, b_gbr = _reorder_in_proj(w_in[l].astype(BF16), b_in[l])
        (upool, cmpk, cmpv, qtn, ksel, vselt, kwin, vwint, gt, qtm, km, vtm, kmean) = _proj_call(
            x2, attn_norm[l][None, :], w_all, b_all[None, :], cos_t, sin_t, B, S)

        kc, vct = _compress_call(
            cmpk, cmpv, cmp_pos[l], cmp_w1[l].astype(BF16),
            cmp_b1[l][:, None, :], cmp_w2[l].astype(BF16), cmp_b2[l][:, None, :], B, R)

        bn = _nsa_call(qtn, gt, kc, vct, ksel, vselt, kwin, vwint, B, S)
        cm = _moba_call(qtm, kmean.reshape(B, S // MOBA_BLOCK, MOBA_DIM), km, vtm, B, S)

        pw_bd = jax.scipy.linalg.block_diag(*[pool_w[l, g] for g in range(len(POOL_WINDOWS))])
        x1, h2 = _merge_call(
            upool, bn, cm, x2, attn_norm[l][None, :], w_gbr, b_gbr[None, :], pw_bd.astype(BF16), pool_scale[l][None, :],
            w_br_pool[l].astype(BF16), w_br_nsa[l].astype(BF16), w_br_moba[l].astype(BF16),
            w_out[l].astype(BF16), ffn_norm[l][None, :], B, S)

        x2 = _ffn_call(h2, x1, w_gate[l].astype(BF16), w_up[l].astype(BF16), w_down[l].astype(BF16),
                       final_norm[None, :], final_norm=(l == depth - 1))
    return x2.reshape(B, S, D)
```

```python
import functools

import numpy as np
import jax
import jax.numpy as jnp
from jax import lax
from jax.experimental import pallas as pl
from jax.experimental.pallas import tpu as pltpu

F32 = jnp.float32
BF16 = jnp.bfloat16

D_MODEL = 1024
HEAD_DIM = 64
ROPE_THETA = 10000.0
RMS_EPS = 1e-6
NEG_INF = -1e30
LOG2E = 1.4426950408889634
LOOKAHEAD = 2
NSA_TRIP_UNROLL = 4
MOBA_TRIP_UNROLL = 4
ONES_ROWS = 16
REMOVED = -(2.0 ** 127)

POOL_WINDOWS = (2, 4, 8, 16)
POOL_GROUP_DIM = 64
POOL_DIM = 256
POOL_MAXW = 16

NSA_HEADS = 8
NSA_KV_GROUPS = 2
NSA_HPG = 4
NSA_DIM = 512
CMP_BLOCK = 32
CMP_STRIDE = 16
CMP_HIDDEN = 256
SEL_BLOCK = 64
SEL_COUNT = 16
N_FORCED = 3
WINDOW = 512

MOBA_HEADS = 4
MOBA_DIM = 256
MOBA_BLOCK = 256
MOBA_TOPK = 3

Q_TILE = 256
LANES = 512
MOBA_Q_TILE = 512
KV_CHUNK = 512
KV_SUB = 256
ROW_TILE = 512
FFN_ROW_TILE = 1024
FFN_COL_TILE = 1408
VMEM_LIMIT = 56 * 1024 * 1024

C_POOL = 0
C_CMP = 256
C_QN = 512
C_KS = 1024
C_VS = 1152
C_KW = 1280
C_VW = 1408
C_GN = 1536
C_QM = 1664
C_KM = 1920
C_VM = 2176
C_TOTAL = 2432


def _dot(a, b):
    return jnp.dot(a, b, preferred_element_type=F32)


def _rms(x, gamma):
    return x * lax.rsqrt(jnp.mean(x * x, axis=-1, keepdims=True) + RMS_EPS) * gamma


def _proj_kernel(x_ref, gam_ref, w_ref, b_ref, cos_ref, sin_ref,
                 upool_ref, cmpk_ref, cmpv_ref, qtn_ref, ksel_ref, vselt_ref, kwin_ref, vwint_ref, gt_ref,
                 qtm_ref, km_ref, vtm_ref, kmean_ref):
    tm = x_ref.shape[0]
    h = _rms(x_ref[...], gam_ref[...]).astype(BF16)
    cos = cos_ref[...]
    sin = sin_ref[...]
    lane = lax.broadcasted_iota(jnp.int32, (tm, 128), 1)
    first_half = (lane & 32) == 0
    scale = HEAD_DIM ** -0.5 * LOG2E

    def seg(a, b):
        return _dot(h, w_ref[:, a:b]) + b_ref[:, a:b]

    def rope(y):
        swap = jnp.where(first_half, pltpu.roll(y, 96, 1), pltpu.roll(y, 32, 1))
        return y * cos + swap * sin

    y = seg(C_POOL, C_POOL + 512)
    upool_ref[...] = y[:, :256]
    cmpk_ref[...] = rope(y[:, 256:384])
    cmpv_ref[...] = y[:, 384:512]

    y = seg(C_QN, C_QN + 512)
    for j in range(4):
        q = rope(y[:, 128 * j:128 * j + 128]) * scale
        qtn_ref[128 * j:128 * j + 128, :] = q.T.astype(BF16)

    y = seg(C_KS, C_KS + 512)
    ksel_ref[...] = rope(y[:, 0:128]).astype(BF16)
    vselt_ref[0] = y[:, 128:256].T.astype(BF16)
    kwin_ref[...] = rope(y[:, 256:384]).astype(BF16)
    vwt = y[:, 384:512].T.astype(BF16)
    for j in range(tm // 128):
        vwint_ref[j] = vwt[:, 128 * j:128 * j + 128]

    y = seg(C_GN, C_GN + 128)
    gt_ref[...] = y.T

    y = seg(C_QM, C_QM + 256)
    for j in range(2):
        q = rope(y[:, 128 * j:128 * j + 128]) * scale
        qtm_ref[128 * j:128 * j + 128, :] = q.T.astype(BF16)

    y = seg(C_KM, C_KM + 256)
    km = jnp.concatenate([rope(y[:, 0:128]), rope(y[:, 128:256])], axis=1)
    km_ref[...] = km.astype(BF16)
    nblk = tm // MOBA_BLOCK
    means = [jnp.sum(km[MOBA_BLOCK * j:MOBA_BLOCK * (j + 1), :], axis=0, keepdims=True) * (1.0 / MOBA_BLOCK)
             for j in range(nblk)]
    kmean_ref[...] = jnp.concatenate(means, axis=0)

    y = seg(C_VM, C_VM + 256)
    vtm_ref[0] = y.T.astype(BF16)


def _proj_call(x2, gamma, w, bias, cos_t, sin_t, B, S):
    N = B * S
    tm = ROW_TILE
    nt = S // tm
    f = lambda shape, dt: jax.ShapeDtypeStruct(shape, dt)
    out_shape = (
        f((N, 256), F32),
        f((N, 128), F32),
        f((N, 128), F32),
        f((B, NSA_DIM, S), BF16),
        f((N, 128), BF16),
        f((B, S // KV_CHUNK, 128, KV_CHUNK), BF16),
        f((N, 128), BF16),
        f((B, S // 128, 128, 128), BF16),
        f((B, 128, S), F32),
        f((B, MOBA_DIM, S), BF16),
        f((N, MOBA_DIM), BF16),
        f((B, S // KV_CHUNK, MOBA_DIM, KV_CHUNK), BF16),
        f((B, nt, tm // MOBA_BLOCK, MOBA_DIM), F32),
    )
    row = lambda w_: pl.BlockSpec((tm, w_), lambda i: (i, 0))
    const = lambda shape: pl.BlockSpec(shape, lambda i: (0,) * len(shape))
    in_specs = [
        row(D_MODEL),
        const((1, D_MODEL)),
        const((D_MODEL, C_TOTAL)),
        const((1, C_TOTAL)),
        pl.BlockSpec((tm, 128), lambda i: (i % nt, 0)),
        pl.BlockSpec((tm, 128), lambda i: (i % nt, 0)),
    ]
    out_specs = (
        row(256),
        row(128),
        row(128),
        pl.BlockSpec((None, NSA_DIM, tm), lambda i: (i // nt, 0, i % nt)),
        row(128),
        pl.BlockSpec((None, tm // KV_CHUNK, 128, KV_CHUNK), lambda i: (i // nt, i % nt, 0, 0)),
        row(128),
        pl.BlockSpec((None, tm // 128, 128, 128), lambda i: (i // nt, i % nt, 0, 0)),
        pl.BlockSpec((None, 128, tm), lambda i: (i // nt, 0, i % nt)),
        pl.BlockSpec((None, MOBA_DIM, tm), lambda i: (i // nt, 0, i % nt)),
        row(MOBA_DIM),
        pl.BlockSpec((None, tm // KV_CHUNK, MOBA_DIM, KV_CHUNK), lambda i: (i // nt, i % nt, 0, 0)),
        pl.BlockSpec((None, None, tm // MOBA_BLOCK, MOBA_DIM), lambda i: (i // nt, i % nt, 0, 0)),
    )
    return pl.pallas_call(
        _proj_kernel,
        out_shape=out_shape,
        grid=(N // tm,),
        in_specs=in_specs,
        out_specs=out_specs,
        compiler_params=pltpu.CompilerParams(
            dimension_semantics=("arbitrary",), vmem_limit_bytes=VMEM_LIMIT),
        name="proj",
    )(x2, gamma, w, bias, cos_t, sin_t)


def _gelu_tanh(x):
    return x * (0.5 * (1.0 + jnp.tanh(np.sqrt(2.0 / np.pi).astype(np.float32) * (x + 0.044715 * (x * x * x)))))


def _compress_kernel(xk_ref, xv_ref, pos_ref, w1_ref, b1_ref, w2_ref, b2_ref, kc_ref, vct_ref):
    R = xk_ref.shape[0] // CMP_STRIDE
    top = [None] * 4
    bot = [None] * 4
    for l in range(CMP_STRIDE):
        xl = [r[pl.ds(l, R, stride=CMP_STRIDE), :] for r in (xk_ref, xv_ref)]
        for j in range(4):
            t, g = divmod(j, NSA_KV_GROUPS)
            piece = xl[t][:, HEAD_DIM * g:HEAD_DIM * (g + 1)]
            lo = (piece + pos_ref[t, l:l + 1, :]).astype(BF16)
            hi = (piece + pos_ref[t, CMP_STRIDE + l:CMP_STRIDE + l + 1, :]).astype(BF16)
            a = _dot(lo, w1_ref[t, HEAD_DIM * l:HEAD_DIM * (l + 1), :])
            b = _dot(hi, w1_ref[t, HEAD_DIM * (CMP_STRIDE + l):HEAD_DIM * (CMP_STRIDE + l + 1), :])
            top[j] = a if top[j] is None else top[j] + a
            bot[j] = b if bot[j] is None else bot[j] + b
    outs = []
    for j in range(4):
        t = j // NSA_KV_GROUPS
        hid = _gelu_tanh(top[j] + pltpu.roll(bot[j], R - 1, 0) + b1_ref[t])
        outs.append(_dot(hid.astype(BF16), w2_ref[t]) + b2_ref[t])
    kc_ref[...] = jnp.concatenate(outs[0:2], axis=1).astype(BF16)
    vct_ref[...] = jnp.concatenate(outs[2:4], axis=1).T.astype(BF16)


def _compress_call(xk, xv, pos, w1, b1, w2, b2, B, R):
    S = R * CMP_STRIDE
    return pl.pallas_call(
        _compress_kernel,
        out_shape=(jax.ShapeDtypeStruct((B, R, 128), BF16), jax.ShapeDtypeStruct((B, 128, R), BF16)),
        grid=(B,),
        in_specs=[
            pl.BlockSpec((S, 128), lambda b: (b, 0)),
            pl.BlockSpec((S, 128), lambda b: (b, 0)),
            pl.BlockSpec((2, CMP_BLOCK, HEAD_DIM), lambda b: (0, 0, 0)),
            pl.BlockSpec((2, CMP_BLOCK * HEAD_DIM, CMP_HIDDEN), lambda b: (0, 0, 0)),
            pl.BlockSpec((2, 1, CMP_HIDDEN), lambda b: (0, 0, 0)),
            pl.BlockSpec((2, CMP_HIDDEN, HEAD_DIM), lambda b: (0, 0, 0)),
            pl.BlockSpec((2, 1, HEAD_DIM), lambda b: (0, 0, 0)),
        ],
        out_specs=(pl.BlockSpec((None, R, 128), lambda b: (b, 0, 0)),
                   pl.BlockSpec((None, 128, R), lambda b: (b, 0, 0))),
        compiler_params=pltpu.CompilerParams(
            dimension_semantics=("arbitrary",), vmem_limit_bytes=VMEM_LIMIT),
        name="nsa_compress",
    )(xk, xv, pos, w1, b1, w2, b2)


def _softmax_block(s):
    m = jnp.max(s, axis=0, keepdims=True)
    m_use = jnp.where(m < 0.5 * NEG_INF, 0.0, m)
    p = jnp.exp2(s - m_use)
    return p, jnp.sum(p, axis=0, keepdims=True)


def _flash_update(sc, rows, block, vt, m, acc_ref):
    n, L = sc.shape[0] // block, sc.shape[1]
    parts = [sc[block * j:block * (j + 1), :] for j in range(n)]
    tops = [jnp.max(parts[j].reshape(block // 8, 8, L), axis=0) for j in range(n)]
    if rows is not None:
        tops = [tops[j] + rows[j:j + 1, :] for j in range(n)]
    top = tops[0]
    for t in tops[1:]:
        top = jnp.maximum(top, t)
    m_new = jnp.maximum(m, jnp.max(top, axis=0, keepdims=True))
    m_use = jnp.where(m_new < 0.5 * NEG_INF, 0.0, m_new)
    alpha = jnp.exp2(m - m_use)
    if rows is None:
        p = jnp.exp2(sc - m_use)
    else:
        p = jnp.concatenate([jnp.exp2(parts[j] + (rows[j:j + 1, :] - m_use)) for j in range(n)], axis=0)
    acc_ref[...] = alpha * acc_ref[...] + _dot(vt, p.astype(BF16))
    return m_new


def _run_steps(steps, pending, tail):
    pending = list(pending)
    fns = [s for s, _ in steps] + list(tail)
    for t, (_, consume) in enumerate(steps):
        s = pending.pop(0)
        if t + LOOKAHEAD < len(fns):
            pending.append(fns[t + LOOKAHEAD]())
        consume(s)
    return tuple(pending)


def _chunk_loop(body, n, carry, unroll):
    def multi(i, cr):
        for j in range(unroll):
            cr = body(unroll * i + j, cr)
        return cr

    carry = lax.fori_loop(0, n // unroll, multi, carry)
    return lax.fori_loop(unroll * (n // unroll), n, body, carry)


def _mask_bias(ok, reps):
    return jnp.concatenate([jnp.where(ok, 0.0, NEG_INF)] * reps, axis=1)


def _topk_mask(scores, rows_f, k):
    work = scores
    for _ in range(k):
        cm = jnp.max(work, axis=0, keepdims=True)
        idx = jnp.min(jnp.where(work == cm, rows_f, 1e9), axis=0, keepdims=True)
        work = jnp.where(rows_f == idx, REMOVED, work)
    return jnp.where(work == REMOVED, 1.0, 0.0)


def _nsa_kernel(qt_ref, gt_ref, kc_ref, vct_ref, ksel_ref, vselt_ref, kwin_ref, vwint_ref, out_ref,
                qt_s, pg_s, bias_s, acc_s, accw_s, ocmp_s):
    qi = pl.program_id(1)
    q0 = qi * Q_TILE
    G = NSA_KV_GROUPS
    HPL = LANES // Q_TILE
    PARTS = NSA_HPG // HPL
    NLG = G * PARTS
    n_cmp_rows = kc_ref.shape[0]
    n_sel = bias_s.shape[1]
    group_of = lambda lg: lg // PARTS
    rows_of = lambda g: slice(HEAD_DIM * g, HEAD_DIM * (g + 1))

    qt_s[...] = jnp.zeros_like(qt_s)
    for h in range(NSA_HEADS):
        g, hh = divmod(h, NSA_HPG)
        part, hl = divmod(hh, HPL)
        qt_s[g * PARTS + part, rows_of(g), Q_TILE * hl:Q_TILE * (hl + 1)] = qt_ref[HEAD_DIM * h:HEAD_DIM * (h + 1), :]
    qts = [qt_s[lg] for lg in range(NLG)]
    tq1 = q0 + lax.broadcasted_iota(jnp.int32, (1, Q_TILE), 1)

    def ones_under(vt):
        return jnp.concatenate([vt, jnp.ones((ONES_ROWS, vt.shape[1]), BF16)], axis=0)

    def cmp_branch(nr):
        n_idx = lax.broadcasted_iota(jnp.int32, (nr, 1), 0)
        cbias = _mask_bias(n_idx * CMP_STRIDE + (CMP_BLOCK - 1) <= tq1, HPL)
        for g in range(G):
            pg = None
            for part in range(PARTS):
                lg = g * PARTS + part
                s = jnp.concatenate([_dot(kc_ref[r:min(r + KV_SUB, nr), :], qts[lg])
                                     for r in range(0, nr, KV_SUB)], axis=0) + cbias
                p, l = _softmax_block(s)
                den = jnp.maximum(l, 1e-30)
                ocmp_s[lg] = _dot(vct_ref[rows_of(g), 0:nr], p.astype(BF16)) / den
                pn = p / den
                for hl in range(HPL):
                    ph = pn[:, hl * Q_TILE:(hl + 1) * Q_TILE]
                    pg = ph if pg is None else pg + ph
            for sl in range(Q_TILE // 128):
                slab = g * (Q_TILE // 128) + sl
                pg_s[slab, 0:8, :] = jnp.zeros((8, 128), F32)
                pg_s[slab, 8:8 + nr, :] = pg[:, 128 * sl:128 * (sl + 1)]
                if nr < n_cmp_rows:
                    pg_s[slab, 8 + nr:8 + n_cmp_rows, :] = jnp.zeros((n_cmp_rows - nr, 128), F32)

    n_visible = (q0 + Q_TILE - CMP_BLOCK) // CMP_STRIDE + 1
    sizes = list(range(128, n_cmp_rows + 1, 128))
    for i, nr in enumerate(sizes):
        lo_ok = n_visible > sizes[i - 1] if i > 0 else True
        hi_ok = n_visible <= nr if i + 1 < len(sizes) else True
        pl.when(jnp.logical_and(lo_ok, hi_ok))(functools.partial(cmp_branch, nr))

    W = WINDOW + Q_TILE
    wstart = jnp.maximum(q0 - WINDOW, 0)
    wc0 = wstart // 128
    kposw = wstart + lax.broadcasted_iota(jnp.int32, (W, 1), 0)
    wbias = _mask_bias((kposw <= tq1) & (kposw > tq1 - WINDOW), HPL)
    vtw = jnp.concatenate([vwint_ref[wc0 + j] for j in range(W // 128)], axis=1)
    accw_s[...] = jnp.zeros_like(accw_s)
    mw = [jnp.full((1, LANES), NEG_INF, F32)] * NLG
    win_steps = []
    for r0 in range(0, W, KV_SUB):
        r1 = min(r0 + KV_SUB, W)
        for lg in range(NLG):
            def score(r0=r0, r1=r1, lg=lg):
                k = kwin_ref[pl.ds(pl.multiple_of(wstart + r0, 128), r1 - r0), :]
                return _dot(k, qts[lg]) + wbias[r0:r1, :]

            def consume(s, r0=r0, r1=r1, lg=lg):
                vt = ones_under(vtw[rows_of(group_of(lg)), r0:r1])
                mw[lg] = _flash_update(s, None, r1 - r0, vt, mw[lg], accw_s.at[lg])

            win_steps.append((score, consume))
    _run_steps(win_steps, [fn() for fn, _ in win_steps[:LOOKAHEAD]], [])

    blk_i = lax.broadcasted_iota(jnp.int32, (n_sel, 1), 0)
    cur = tq1 // SEL_BLOCK
    forced = (blk_i == 0) | (blk_i == cur) | (blk_i == cur - 1)
    started = blk_i <= cur
    n_top = min(SEL_COUNT, n_sel)
    slc = []
    for g in range(G):
        slabs = []
        for sl in range(Q_TILE // 128):
            def tap(w):
                return pg_s[g * (Q_TILE // 128) + sl, pl.ds(8 + w, n_sel, stride=SEL_BLOCK // CMP_STRIDE), :]

            t = tap(-1) + 2.0 * tap(0)
            t = t + 2.0 * tap(1)
            t = t + 2.0 * tap(2)
            t = t + tap(3)
            slabs.append(t)
        t = slabs[0] if len(slabs) == 1 else jnp.concatenate(slabs, axis=1)
        slc.append(jnp.where(forced, NEG_INF, jnp.where(started, t, NEG_INF)))
    free = _topk_mask(jnp.concatenate(slc, axis=1), jnp.broadcast_to(blk_i.astype(F32), (n_sel, G * Q_TILE)),
                      n_top - N_FORCED)
    for g in range(G):
        bias = jnp.where(forced, 0.0, jnp.where(free[:, Q_TILE * g:Q_TILE * (g + 1)] > 0.5, 0.0, NEG_INF))
        bias_s[g] = jnp.concatenate([bias] * HPL, axis=1)

    bpc = KV_CHUNK // SEL_BLOCK
    bps = KV_SUB // SEL_BLOCK
    c_last = q0 // KV_CHUNK

    def chunk_steps(c, ms, causal):
        steps = []
        for sub in range(KV_CHUNK // KV_SUB):
            for lg in range(NLG):
                def score(sub=sub, lg=lg):
                    base = pl.multiple_of(c * KV_CHUNK, KV_CHUNK) + sub * KV_SUB
                    s = _dot(ksel_ref[pl.ds(pl.multiple_of(base, KV_SUB), KV_SUB), :], qts[lg])
                    return s if causal is None else s + causal[sub]

                def consume(s, sub=sub, lg=lg):
                    g = group_of(lg)
                    rows = bias_s[g, pl.ds(pl.multiple_of(c * bpc, bpc), bpc), :][bps * sub:bps * (sub + 1), :]
                    vt = vselt_ref[c, rows_of(g), KV_SUB * sub:KV_SUB * (sub + 1)]
                    ms[lg] = _flash_update(s, rows, SEL_BLOCK, ones_under(vt), ms[lg], acc_s.at[lg])

                steps.append((score, consume))
        return steps

    acc_s[...] = jnp.zeros_like(acc_s)

    def body(c, carry):
        ms = list(carry[:NLG])
        nxt = [fn for fn, _ in chunk_steps(c + 1, ms, None)[:LOOKAHEAD]]
        pending = _run_steps(chunk_steps(c, ms, None), carry[NLG:], nxt)
        return tuple(ms) + pending

    first = [fn() for fn, _ in chunk_steps(0, None, None)[:LOOKAHEAD]]
    carry = (jnp.full((1, LANES), NEG_INF, F32),) * NLG + tuple(first)
    carry = _chunk_loop(body, c_last, carry, NSA_TRIP_UNROLL)
    kpos = c_last * KV_CHUNK + lax.broadcasted_iota(jnp.int32, (KV_CHUNK, 1), 0)
    causal = _mask_bias(kpos <= tq1, HPL)
    n_sub = KV_CHUNK // KV_SUB
    needed = ((q0 % KV_CHUNK) + Q_TILE + KV_SUB - 1) // KV_SUB

    def run_diag(n_pieces):
        steps = chunk_steps(c_last, list(carry[:NLG]), [causal[KV_SUB * sub:KV_SUB * (sub + 1), :]
                                                       for sub in range(n_sub)])[:n_pieces * NLG]
        _run_steps(steps, [fn() for fn, _ in steps[:LOOKAHEAD]], [])

    for n_pieces in range(1, n_sub + 1):
        cond = needed == n_pieces if n_pieces < n_sub else needed >= n_pieces
        pl.when(cond)(functools.partial(run_diag, n_pieces))

    gates = jax.nn.sigmoid(gt_ref[0:3 * NSA_HEADS, :])
    outs = []
    for h in range(NSA_HEADS):
        g, hh = divmod(h, NSA_HPG)
        part, hl = divmod(hh, HPL)
        lg = g * PARTS + part
        cols = slice(Q_TILE * hl, Q_TILE * (hl + 1))
        o_sel = acc_s[lg, 0:HEAD_DIM, cols] / jnp.maximum(acc_s[lg, HEAD_DIM:HEAD_DIM + 1, cols], 1e-30)
        o_win = accw_s[lg, 0:HEAD_DIM, cols] / jnp.maximum(accw_s[lg, HEAD_DIM:HEAD_DIM + 1, cols], 1e-30)
        o = gates[3 * h:3 * h + 1, :] * ocmp_s[lg, :, cols]
        o = o + gates[3 * h + 1:3 * h + 2, :] * o_sel
        o = o + gates[3 * h + 2:3 * h + 3, :] * o_win
        outs.append(o)
    out_ref[...] = jnp.concatenate(outs, axis=0).T.astype(BF16)


def _nsa_call(qtn, gt, kc, vct, ksel, vselt, kwin, vwint, B, S):
    R = S // CMP_STRIDE
    n_sel = S // SEL_BLOCK
    nlg = NSA_HEADS * Q_TILE // LANES
    return pl.pallas_call(
        _nsa_kernel,
        out_shape=jax.ShapeDtypeStruct((B * S, NSA_DIM), BF16),
        grid=(B, S // Q_TILE),
        in_specs=[
            pl.BlockSpec((None, NSA_DIM, Q_TILE), lambda b, q: (b, 0, q)),
            pl.BlockSpec((None, 128, Q_TILE), lambda b, q: (b, 0, q)),
            pl.BlockSpec((None, R, 128), lambda b, q: (b, 0, 0)),
            pl.BlockSpec((None, 128, R), lambda b, q: (b, 0, 0)),
            pl.BlockSpec((S, 128), lambda b, q: (b, 0)),
            pl.BlockSpec((None, S // KV_CHUNK, 128, KV_CHUNK), lambda b, q: (b, 0, 0, 0)),
            pl.BlockSpec((S, 128), lambda b, q: (b, 0)),
            pl.BlockSpec((None, S // 128, 128, 128), lambda b, q: (b, 0, 0, 0)),
        ],
        out_specs=pl.BlockSpec((Q_TILE, NSA_DIM), lambda b, q: (b * (S // Q_TILE) + q, 0)),
        scratch_shapes=[
            pltpu.VMEM((nlg, 128, LANES), BF16),
            pltpu.VMEM((NSA_KV_GROUPS * (Q_TILE // 128), 8 + R, 128), F32),
            pltpu.VMEM((NSA_KV_GROUPS, n_sel, LANES), F32),
            pltpu.VMEM((nlg, HEAD_DIM + ONES_ROWS, LANES), F32),
            pltpu.VMEM((nlg, HEAD_DIM + ONES_ROWS, LANES), F32),
            pltpu.VMEM((nlg, HEAD_DIM, LANES), F32),
        ],
        compiler_params=pltpu.CompilerParams(
            dimension_semantics=("arbitrary", "arbitrary"), vmem_limit_bytes=VMEM_LIMIT),
        name="nsa_attn",
    )(qtn, gt, kc, vct, ksel, vselt, kwin, vwint)


def _moba_kernel(qt_ref, kmean_ref, k_ref, vt_ref, out_ref, qt_s, bias_s, acc_s):
    qi = pl.program_id(1)
    q0 = qi * MOBA_Q_TILE
    L = MOBA_Q_TILE
    H = MOBA_HEADS
    n_blk = kmean_ref.shape[0]
    n_top = max(1, min(MOBA_TOPK, n_blk - 1))

    qt_s[...] = jnp.zeros_like(qt_s)
    for h in range(H):
        rows = slice(HEAD_DIM * h, HEAD_DIM * (h + 1))
        qt_s[h, rows, :] = qt_ref[rows, :]
    qts = [qt_s[h] for h in range(H)]
    tq = q0 + lax.broadcasted_iota(jnp.int32, (1, L), 1)

    own = tq // MOBA_BLOCK
    blk_i = lax.broadcasted_iota(jnp.int32, (n_blk, 1), 0)
    blk_f = jnp.broadcast_to(blk_i.astype(F32), (n_blk, L))
    kmean = kmean_ref[...].astype(BF16)
    for h in range(H):
        sg = jnp.where(blk_i < own, _dot(kmean, qts[h]), NEG_INF)
        sel = _topk_mask(sg, blk_f, n_top)
        allow = ((sel > 0.5) & (blk_i < own)) | (blk_i == own)
        bias = jnp.where(allow, 0.0, NEG_INF)
        for j in range(n_blk):
            bias_s[h, j] = jnp.broadcast_to(bias[j:j + 1, :], (8, L))

    bpc = KV_CHUNK // MOBA_BLOCK

    steps = [(j, h) for j in range(bpc) for h in range(H)]

    ones = jnp.ones((ONES_ROWS, MOBA_BLOCK), BF16)

    def scores(c, j, h, diagonal):
        base = pl.multiple_of(c * KV_CHUNK, KV_CHUNK) + j * MOBA_BLOCK
        k = k_ref[pl.ds(pl.multiple_of(base, MOBA_BLOCK), MOBA_BLOCK), :]
        s = _dot(k, qts[h])
        if diagonal:
            kpos = base + lax.broadcasted_iota(jnp.int32, (MOBA_BLOCK, 1), 0)
            s = jnp.where(kpos <= tq, s, NEG_INF)
        return s

    def chunk_steps(c, ms, pending, diagonal):
        ms, pending = list(ms), list(pending)
        for t, (j, h) in enumerate(steps):
            s = pending.pop(0)
            u = t + LOOKAHEAD
            if u < len(steps):
                pending.append(scores(c, *steps[u], diagonal))
            elif not diagonal:
                pending.append(scores(c + 1, *steps[u - len(steps)], False))
            vt = vt_ref[c, HEAD_DIM * h:HEAD_DIM * (h + 1), MOBA_BLOCK * j:MOBA_BLOCK * (j + 1)]
            ms[h] = _flash_update(s, bias_s[h, c * bpc + j], MOBA_BLOCK, jnp.concatenate([vt, ones], axis=0),
                                  ms[h], acc_s.at[h])
        return tuple(ms), tuple(pending)

    acc_s[...] = jnp.zeros_like(acc_s)
    c_last = q0 // KV_CHUNK
    init = (jnp.full((1, L), NEG_INF, F32),) * H

    def body(c, carry):
        ms, pending = chunk_steps(c, carry[:H], carry[H:], False)
        return ms + pending

    carry = init + tuple(scores(0, *steps[i], False) for i in range(LOOKAHEAD))
    carry = _chunk_loop(body, c_last, carry, MOBA_TRIP_UNROLL)
    chunk_steps(c_last, carry[:H], [scores(c_last, *steps[i], True) for i in range(LOOKAHEAD)], True)
    outs = [acc_s[h, 0:HEAD_DIM, :] / jnp.maximum(acc_s[h, HEAD_DIM:HEAD_DIM + 1, :], 1e-30) for h in range(H)]
    out_ref[...] = jnp.concatenate(outs, axis=0).T.astype(BF16)


def _moba_call(qtm, kmean, km, vtm, B, S):
    n_blk = S // MOBA_BLOCK
    L = MOBA_Q_TILE
    return pl.pallas_call(
        _moba_kernel,
        out_shape=jax.ShapeDtypeStruct((B * S, MOBA_DIM), BF16),
        grid=(B, S // L),
        in_specs=[
            pl.BlockSpec((None, MOBA_DIM, L), lambda b, q: (b, 0, q)),
            pl.BlockSpec((None, n_blk, MOBA_DIM), lambda b, q: (b, 0, 0)),
            pl.BlockSpec((S, MOBA_DIM), lambda b, q: (b, 0)),
            pl.BlockSpec((None, S // KV_CHUNK, MOBA_DIM, KV_CHUNK), lambda b, q: (b, 0, 0, 0)),
        ],
        out_specs=pl.BlockSpec((L, MOBA_DIM), lambda b, q: (b * (S // L) + q, 0)),
        scratch_shapes=[
            pltpu.VMEM((MOBA_HEADS, MOBA_DIM, L), BF16),
            pltpu.VMEM((MOBA_HEADS, n_blk, 8, L), F32),
            pltpu.VMEM((MOBA_HEADS, HEAD_DIM + ONES_ROWS, L), F32),
        ],
        compiler_params=pltpu.CompilerParams(
            dimension_semantics=("arbitrary", "arbitrary"), vmem_limit_bytes=VMEM_LIMIT),
        name="moba_attn",
    )(qtm, kmean, km, vtm)


def _merge_kernel(u_ref, halo_ref, b_ref, c_ref, x_ref, gam_in_ref, wg_ref, bg_ref, pw_ref, ps_ref,
                  wa_ref, wb_ref, wc_ref, wo_ref, x1_ref, ext_s, *, tiles_per_seq):
    tm = u_ref.shape[0]
    i = pl.program_id(0)
    first = (i % tiles_per_seq) == 0
    u = u_ref[...]
    ext_s[0:POOL_MAXW, :] = jnp.where(first, 0.0, halo_ref[...])
    ext_s[POOL_MAXW:POOL_MAXW + tm, :] = u

    def tail_sum(col, k0, k1):
        tot = None
        for k in range(k0, k1):
            v = ext_s[pl.ds(POOL_MAXW - k, tm), 128 * col:128 * col + 128]
            tot = v if tot is None else tot + v
        return tot

    t_glob = (i % tiles_per_seq) * tm + lax.broadcasted_iota(jnp.int32, (tm, 1), 0)
    lane = lax.broadcasted_iota(jnp.int32, (1, 128), 1)
    low = lane < POOL_GROUP_DIM
    pooled = []
    for col in range(2):
        wa_, wb_ = POOL_WINDOWS[2 * col], POOL_WINDOWS[2 * col + 1]
        sa = tail_sum(col, 0, wa_)
        sb = sa + tail_sum(col, wa_, wb_)
        ca = jnp.minimum(t_glob + 1, wa_).astype(F32)
        cb = jnp.minimum(t_glob + 1, wb_).astype(F32)
        pooled.append(jnp.where(low, sa, sb) / jnp.where(low, ca, cb))
    d = jnp.concatenate(pooled, axis=1) - u
    a = _dot(d.astype(BF16), pw_ref[...]) * ps_ref[...]

    av = _dot(a.astype(BF16), wa_ref[...])
    bv = _dot(b_ref[...], wb_ref[...])
    cv = _dot(c_ref[...], wc_ref[...])
    x = x_ref[...]
    h = _rms(x, gam_in_ref[...]).astype(BF16)

    def gate(j):
        cols = slice(D_MODEL * j, D_MODEL * (j + 1))
        return jax.nn.sigmoid(_dot(h, wg_ref[:, cols]) + bg_ref[:, cols])

    merged = gate(0) * av
    merged = merged + gate(1) * bv
    merged = merged + gate(2) * cv
    x1_ref[...] = x + _dot(merged.astype(BF16), wo_ref[...])


def _merge_call(upool, bn, cm, x2, gamma_in, wg, bg, pw, ps, wa, wb, wc, wo, B, S):
    N = B * S
    tm = ROW_TILE
    nt = S // tm
    hb = tm // POOL_MAXW
    row = lambda w_: pl.BlockSpec((tm, w_), lambda i: (i, 0))
    const = lambda shape: pl.BlockSpec(shape, lambda i: (0,) * len(shape))
    return pl.pallas_call(
        functools.partial(_merge_kernel, tiles_per_seq=nt),
        out_shape=jax.ShapeDtypeStruct((N, D_MODEL), F32),
        grid=(N // tm,),
        in_specs=[
            row(POOL_DIM),
            pl.BlockSpec((POOL_MAXW, POOL_DIM), lambda i: (jnp.maximum(i * hb - 1, 0), 0)),
            row(NSA_DIM),
            row(MOBA_DIM),
            row(D_MODEL),
            const((1, D_MODEL)),
            const((D_MODEL, 3 * D_MODEL)),
            const((1, 3 * D_MODEL)),
            const((POOL_DIM, POOL_DIM)),
            const((1, POOL_DIM)),
            const((POOL_DIM, D_MODEL)),
            const((NSA_DIM, D_MODEL)),
            const((MOBA_DIM, D_MODEL)),
            const((D_MODEL, D_MODEL)),
        ],
        out_specs=row(D_MODEL),
        scratch_shapes=[pltpu.VMEM((POOL_MAXW + tm, POOL_DIM), F32)],
        compiler_params=pltpu.CompilerParams(
            dimension_semantics=("arbitrary",), vmem_limit_bytes=VMEM_LIMIT),
        name="merge",
    )(upool, upool, bn, cm, x2, gamma_in, wg, bg, pw, ps, wa, wb, wc, wo)


def _ffn_kernel(x_ref, gam_in_ref, wg_ref, wu_ref, wd_ref, gam_ref, out_ref, acc_s, h_s, *, final_norm):
    f = pl.program_id(1)

    @pl.when(f == 0)
    def _():
        acc_s[...] = jnp.zeros_like(acc_s)
        h_s[...] = _rms(x_ref[...], gam_in_ref[...]).astype(BF16)

    h = h_s[...]
    g = _dot(h, wg_ref[...])
    u = _dot(h, wu_ref[...])
    act = (g * jax.nn.sigmoid(g)) * u
    acc_s[...] += _dot(act.astype(BF16), wd_ref[...])

    @pl.when(f == pl.num_programs(1) - 1)
    def _():
        y = x_ref[...] + acc_s[...]
        if final_norm:
            y = _rms(y, gam_ref[...])
        out_ref[...] = y


def _ffn_call(x1, gamma_in, wg, wu, wd, gamma, final_norm):
    N = x1.shape[0]
    F = wg.shape[1]
    tm, tf = FFN_ROW_TILE, FFN_COL_TILE
    return pl.pallas_call(
        functools.partial(_ffn_kernel, final_norm=final_norm),
        out_shape=jax.ShapeDtypeStruct((N, D_MODEL), F32),
        grid=(N // tm, F // tf),
        in_specs=[
            pl.BlockSpec((tm, D_MODEL), lambda i, f: (i, 0)),
            pl.BlockSpec((1, D_MODEL), lambda i, f: (0, 0)),
            pl.BlockSpec((D_MODEL, tf), lambda i, f: (0, f)),
            pl.BlockSpec((D_MODEL, tf), lambda i, f: (0, f)),
            pl.BlockSpec((tf, D_MODEL), lambda i, f: (f, 0)),
            pl.BlockSpec((1, D_MODEL), lambda i, f: (0, 0)),
        ],
        out_specs=pl.BlockSpec((tm, D_MODEL), lambda i, f: (i, 0)),
        scratch_shapes=[pltpu.VMEM((tm, D_MODEL), F32), pltpu.VMEM((tm, D_MODEL), BF16)],
        compiler_params=pltpu.CompilerParams(
            dimension_semantics=("arbitrary", "arbitrary"), vmem_limit_bytes=VMEM_LIMIT),
        name="ffn",
    )(x1, gamma_in, wg, wu, wd, gamma)


def _reorder_in_proj(w, b):
    o_qn = POOL_DIM
    o_kv = o_qn + NSA_DIM
    o_gn = o_kv + 6 * 128
    o_mo = o_gn + 3 * NSA_HEADS
    o_gb = o_mo + 3 * MOBA_DIM
    pad = 128 - 3 * NSA_HEADS

    def pick(a):
        parts = [a[..., 0:o_qn], a[..., o_kv:o_kv + 256], a[..., o_qn:o_kv], a[..., o_kv + 256:o_gn],
                 a[..., o_gn:o_mo], jnp.zeros(a.shape[:-1] + (pad,), a.dtype), a[..., o_mo:o_gb]]
        return jnp.concatenate(parts, axis=-1)

    return pick(w), pick(b), w[..., o_gb:], b[..., o_gb:]


def _rope_tables(S):
    pos = jnp.arange(S, dtype=F32)
    inv_freq = ROPE_THETA ** (-jnp.arange(0, HEAD_DIM, 2, dtype=F32) / HEAD_DIM)
    ang = pos[:, None] * inv_freq[None, :]
    cos, sin = jnp.cos(ang), jnp.sin(ang)
    cos_t = jnp.tile(cos, (1, 4))
    sin_t = jnp.tile(jnp.concatenate([-sin, sin], axis=1), (1, 2))
    return cos_t, sin_t


def kernel(x, attn_norm, w_in, b_in, pool_w, pool_scale, cmp_pos, cmp_w1, cmp_b1, cmp_w2, cmp_b2,
           w_br_pool, w_br_nsa, w_br_moba, w_out, ffn_norm, w_gate, w_up, w_down, final_norm):
    B, S, D = x.shape
    depth = w_in.shape[0]
    assert D == D_MODEL and S % KV_CHUNK == 0 and S % ROW_TILE == 0 and S >= WINDOW + Q_TILE
    assert (B * S) % FFN_ROW_TILE == 0 and w_gate.shape[2] % FFN_COL_TILE == 0
    assert S % (128 * CMP_STRIDE) == 0 and S % MOBA_Q_TILE == 0 and S // SEL_BLOCK >= SEL_COUNT
    N = B * S
    R = S // CMP_STRIDE
    cos_t, sin_t = _rope_tables(S)
    x2 = x.reshape(N, D)
    for l in range(depth):
        w_all, b_all, w_gbr, b_gbr = _reorder_in_proj(w_in[l].astype(BF16), b_in[l])
        (upool, cmpk, cmpv, qtn, ksel, vselt, kwin, vwint, gt, qtm, km, vtm, kmean) = _proj_call(
            x2, attn_norm[l][None, :], w_all, b_all[None, :], cos_t, sin_t, B, S)

        kc, vct = _compress_call(
            cmpk, cmpv, cmp_pos[l], cmp_w1[l].astype(BF16),
            cmp_b1[l][:, None, :], cmp_w2[l].astype(BF16), cmp_b2[l][:, None, :], B, R)

        bn = _nsa_call(qtn, gt, kc, vct, ksel, vselt, kwin, vwint, B, S)
        cm = _moba_call(qtm, kmean.reshape(B, S // MOBA_BLOCK, MOBA_DIM), km, vtm, B, S)

        pw_bd = jax.scipy.linalg.block_diag(*[pool_w[l, g] for g in range(len(POOL_WINDOWS))])
        x1 = _merge_call(
            upool, bn, cm, x2, attn_norm[l][None, :], w_gbr, b_gbr[None, :], pw_bd.astype(BF16), pool_scale[l][None, :],
            w_br_pool[l].astype(BF16), w_br_nsa[l].astype(BF16), w_br_moba[l].astype(BF16),
            w_out[l].astype(BF16), B, S)

        x2 = _ffn_call(x1, ffn_norm[l][None, :], w_gate[l].astype(BF16), w_up[l].astype(BF16),
                       w_down[l].astype(BF16), final_norm[None, :], final_norm=(l == depth - 1))
    return x2.reshape(B, S, D)
```

```python
import functools

import numpy as np
import jax
import jax.numpy as jnp
from jax import lax
from jax.experimental import pallas as pl
from jax.experimental.pallas import tpu as pltpu

F32 = jnp.float32
BF16 = jnp.bfloat16

D_MODEL = 1024
HEAD_DIM = 64
ROPE_THETA = 10000.0
RMS_EPS = 1e-6
NEG_INF = -1e30
LOG2E = 1.4426950408889634
LOOKAHEAD = 2
NSA_TRIP_UNROLL = 4
MOBA_TRIP_UNROLL = 4
ONES_ROWS = 16
REMOVED = -(2.0 ** 127)

POOL_WINDOWS = (2, 4, 8, 16)
POOL_GROUP_DIM = 64
POOL_DIM = 256
POOL_MAXW = 16

NSA_HEADS = 8
NSA_KV_GROUPS = 2
NSA_HPG = 4
NSA_DIM = 512
CMP_BLOCK = 32
CMP_STRIDE = 16
CMP_HIDDEN = 256
SEL_BLOCK = 64
SEL_COUNT = 16
N_FORCED = 3
WINDOW = 512

MOBA_HEADS = 4
MOBA_DIM = 256
MOBA_BLOCK = 256
MOBA_TOPK = 3

Q_TILE = 256
LANES = 512
MOBA_Q_TILE = 512
KV_CHUNK = 512
KV_SUB = 256
ROW_TILE = 512
FFN_ROW_TILE = 1024
FFN_COL_TILE = 1408
VMEM_LIMIT = 56 * 1024 * 1024

C_POOL = 0
C_CMP = 256
C_QN = 512
C_KS = 1024
C_VS = 1152
C_KW = 1280
C_VW = 1408
C_GN = 1536
C_QM = 1664
C_KM = 1920
C_VM = 2176
C_TOTAL = 2432


def _dot(a, b):
    return jnp.dot(a, b, preferred_element_type=F32)


def _rms(x, gamma):
    return x * lax.rsqrt(jnp.mean(x * x, axis=-1, keepdims=True) + RMS_EPS) * gamma


def _proj_kernel(x_ref, gam_ref, w_ref, b_ref, cos_ref, sin_ref,
                 upool_ref, cmpk_ref, cmpv_ref, qtn_ref, ksel_ref, vselt_ref, kwin_ref, vwint_ref, gt_ref,
                 qtm_ref, km_ref, vtm_ref, kmean_ref):
    tm = x_ref.shape[0]
    h = _rms(x_ref[...], gam_ref[...]).astype(BF16)
    cos = cos_ref[...]
    sin = sin_ref[...]
    lane = lax.broadcasted_iota(jnp.int32, (tm, 128), 1)
    first_half = (lane & 32) == 0
    scale = HEAD_DIM ** -0.5 * LOG2E

    def seg(a, b):
        return _dot(h, w_ref[:, a:b]) + b_ref[:, a:b]

    def rope(y):
        swap = jnp.where(first_half, pltpu.roll(y, 96, 1), pltpu.roll(y, 32, 1))
        return y * cos + swap * sin

    y = seg(C_POOL, C_POOL + 512)
    upool_ref[...] = y[:, :256]
    cmpk_ref[...] = rope(y[:, 256:384])
    cmpv_ref[...] = y[:, 384:512]

    y = seg(C_QN, C_QN + 512)
    for j in range(4):
        q = rope(y[:, 128 * j:128 * j + 128]) * scale
        qtn_ref[128 * j:128 * j + 128, :] = q.T.astype(BF16)

    y = seg(C_KS, C_KS + 512)
    ksel_ref[...] = rope(y[:, 0:128]).astype(BF16)
    vselt_ref[0] = y[:, 128:256].T.astype(BF16)
    kwin_ref[...] = rope(y[:, 256:384]).astype(BF16)
    vwt = y[:, 384:512].T.astype(BF16)
    for j in range(tm // 128):
        vwint_ref[j] = vwt[:, 128 * j:128 * j + 128]

    y = seg(C_GN, C_GN + 128)
    gt_ref[...] = y.T

    y = seg(C_QM, C_QM + 256)
    for j in range(2):
        q = rope(y[:, 128 * j:128 * j + 128]) * scale
        qtm_ref[128 * j:128 * j + 128, :] = q.T.astype(BF16)

    y = seg(C_KM, C_KM + 256)
    km = jnp.concatenate([rope(y[:, 0:128]), rope(y[:, 128:256])], axis=1)
    km_ref[...] = km.astype(BF16)
    nblk = tm // MOBA_BLOCK
    means = [jnp.sum(km[MOBA_BLOCK * j:MOBA_BLOCK * (j + 1), :], axis=0, keepdims=True) * (1.0 / MOBA_BLOCK)
             for j in range(nblk)]
    kmean_ref[...] = jnp.concatenate(means, axis=0)

    y = seg(C_VM, C_VM + 256)
    vtm_ref[0] = y.T.astype(BF16)


def _proj_call(x2, gamma, w, bias, cos_t, sin_t, B, S):
    N = B * S
    tm = ROW_TILE
    nt = S // tm
    f = lambda shape, dt: jax.ShapeDtypeStruct(shape, dt)
    out_shape = (
        f((N, 256), F32),
        f((N, 128), F32),
        f((N, 128), F32),
        f((B, NSA_DIM, S), BF16),
        f((N, 128), BF16),
        f((B, S // KV_CHUNK, 128, KV_CHUNK), BF16),
        f((N, 128), BF16),
        f((B, S // 128, 128, 128), BF16),
        f((B, 128, S), F32),
        f((B, MOBA_DIM, S), BF16),
        f((N, MOBA_DIM), BF16),
        f((B, S // KV_CHUNK, MOBA_DIM, KV_CHUNK), BF16),
        f((B, nt, tm // MOBA_BLOCK, MOBA_DIM), F32),
    )
    row = lambda w_: pl.BlockSpec((tm, w_), lambda i: (i, 0))
    const = lambda shape: pl.BlockSpec(shape, lambda i: (0,) * len(shape))
    in_specs = [
        row(D_MODEL),
        const((1, D_MODEL)),
        const((D_MODEL, C_TOTAL)),
        const((1, C_TOTAL)),
        pl.BlockSpec((tm, 128), lambda i: (i % nt, 0)),
        pl.BlockSpec((tm, 128), lambda i: (i % nt, 0)),
    ]
    out_specs = (
        row(256),
        row(128),
        row(128),
        pl.BlockSpec((None, NSA_DIM, tm), lambda i: (i // nt, 0, i % nt)),
        row(128),
        pl.BlockSpec((None, tm // KV_CHUNK, 128, KV_CHUNK), lambda i: (i // nt, i % nt, 0, 0)),
        row(128),
        pl.BlockSpec((None, tm // 128, 128, 128), lambda i: (i // nt, i % nt, 0, 0)),
        pl.BlockSpec((None, 128, tm), lambda i: (i // nt, 0, i % nt)),
        pl.BlockSpec((None, MOBA_DIM, tm), lambda i: (i // nt, 0, i % nt)),
        row(MOBA_DIM),
        pl.BlockSpec((None, tm // KV_CHUNK, MOBA_DIM, KV_CHUNK), lambda i: (i // nt, i % nt, 0, 0)),
        pl.BlockSpec((None, None, tm // MOBA_BLOCK, MOBA_DIM), lambda i: (i // nt, i % nt, 0, 0)),
    )
    return pl.pallas_call(
        _proj_kernel,
        out_shape=out_shape,
        grid=(N // tm,),
        in_specs=in_specs,
        out_specs=out_specs,
        compiler_params=pltpu.CompilerParams(
            dimension_semantics=("arbitrary",), vmem_limit_bytes=VMEM_LIMIT),
        name="proj",
    )(x2, gamma, w, bias, cos_t, sin_t)


def _gelu_tanh(x):
    return x * (0.5 * (1.0 + jnp.tanh(np.sqrt(2.0 / np.pi).astype(np.float32) * (x + 0.044715 * (x * x * x)))))


def _compress_kernel(xk_ref, xv_ref, pos_ref, w1_ref, b1_ref, w2_ref, b2_ref, kc_ref, vct_ref):
    R = xk_ref.shape[0] // CMP_STRIDE
    top = [None] * 4
    bot = [None] * 4
    for l in range(CMP_STRIDE):
        xl = [r[pl.ds(l, R, stride=CMP_STRIDE), :] for r in (xk_ref, xv_ref)]
        for j in range(4):
            t, g = divmod(j, NSA_KV_GROUPS)
            piece = xl[t][:, HEAD_DIM * g:HEAD_DIM * (g + 1)]
            lo = (piece + pos_ref[t, l:l + 1, :]).astype(BF16)
            hi = (piece + pos_ref[t, CMP_STRIDE + l:CMP_STRIDE + l + 1, :]).astype(BF16)
            a = _dot(lo, w1_ref[t, HEAD_DIM * l:HEAD_DIM * (l + 1), :])
            b = _dot(hi, w1_ref[t, HEAD_DIM * (CMP_STRIDE + l):HEAD_DIM * (CMP_STRIDE + l + 1), :])
            top[j] = a if top[j] is None else top[j] + a
            bot[j] = b if bot[j] is None else bot[j] + b
    outs = []
    for j in range(4):
        t = j // NSA_KV_GROUPS
        hid = _gelu_tanh(top[j] + pltpu.roll(bot[j], R - 1, 0) + b1_ref[t])
        outs.append(_dot(hid.astype(BF16), w2_ref[t]) + b2_ref[t])
    kc_ref[...] = jnp.concatenate(outs[0:2], axis=1).astype(BF16)
    vct_ref[...] = jnp.concatenate(outs[2:4], axis=1).T.astype(BF16)


def _compress_call(xk, xv, pos, w1, b1, w2, b2, B, R):
    S = R * CMP_STRIDE
    return pl.pallas_call(
        _compress_kernel,
        out_shape=(jax.ShapeDtypeStruct((B, R, 128), BF16), jax.ShapeDtypeStruct((B, 128, R), BF16)),
        grid=(B,),
        in_specs=[
            pl.BlockSpec((S, 128), lambda b: (b, 0)),
            pl.BlockSpec((S, 128), lambda b: (b, 0)),
            pl.BlockSpec((2, CMP_BLOCK, HEAD_DIM), lambda b: (0, 0, 0)),
            pl.BlockSpec((2, CMP_BLOCK * HEAD_DIM, CMP_HIDDEN), lambda b: (0, 0, 0)),
            pl.BlockSpec((2, 1, CMP_HIDDEN), lambda b: (0, 0, 0)),
            pl.BlockSpec((2, CMP_HIDDEN, HEAD_DIM), lambda b: (0, 0, 0)),
            pl.BlockSpec((2, 1, HEAD_DIM), lambda b: (0, 0, 0)),
        ],
        out_specs=(pl.BlockSpec((None, R, 128), lambda b: (b, 0, 0)),
                   pl.BlockSpec((None, 128, R), lambda b: (b, 0, 0))),
        compiler_params=pltpu.CompilerParams(
            dimension_semantics=("arbitrary",), vmem_limit_bytes=VMEM_LIMIT),
        name="nsa_compress",
    )(xk, xv, pos, w1, b1, w2, b2)


def _softmax_block(s):
    m = jnp.max(s, axis=0, keepdims=True)
    m_use = jnp.where(m < 0.5 * NEG_INF, 0.0, m)
    p = jnp.exp2(s - m_use)
    return p, jnp.sum(p, axis=0, keepdims=True)


def _flash_update(sc, rows, block, vt, m, acc_ref):
    n, L = sc.shape[0] // block, sc.shape[1]
    sb = sc.astype(BF16)
    parts = [sb[block * j:block * (j + 1), :] for j in range(n)]
    tops = [jnp.max(parts[j].reshape(block // 16, 16, L), axis=0) for j in range(n)]
    if rows is not None:
        tops = [tops[j] + rows[j:j + 1, :].astype(BF16) for j in range(n)]
    top = tops[0]
    for t in tops[1:]:
        top = jnp.maximum(top, t)
    m_new = jnp.maximum(m, jnp.max(top, axis=0, keepdims=True).astype(F32))
    m_use = jnp.where(m_new < 0.5 * NEG_INF, 0.0, m_new)
    alpha = jnp.exp2(m - m_use)
    if rows is None:
        p = jnp.exp2(sb - m_use.astype(BF16))
    else:
        p = jnp.concatenate([jnp.exp2(parts[j] + (rows[j:j + 1, :] - m_use).astype(BF16)) for j in range(n)], axis=0)
    acc_ref[...] = alpha * acc_ref[...] + _dot(vt, p)
    return m_new


def _run_steps(steps, pending, tail):
    pending = list(pending)
    fns = [s for s, _ in steps] + list(tail)
    for t, (_, consume) in enumerate(steps):
        s = pending.pop(0)
        if t + LOOKAHEAD < len(fns):
            pending.append(fns[t + LOOKAHEAD]())
        consume(s)
    return tuple(pending)


def _chunk_loop(body, n, carry, unroll):
    def multi(i, cr):
        for j in range(unroll):
            cr = body(unroll * i + j, cr)
        return cr

    carry = lax.fori_loop(0, n // unroll, multi, carry)
    return lax.fori_loop(unroll * (n // unroll), n, body, carry)


def _mask_bias(ok, reps):
    return jnp.concatenate([jnp.where(ok, 0.0, NEG_INF)] * reps, axis=1)


def _topk_mask(scores, rows_f, k):
    work = scores
    for _ in range(k):
        cm = jnp.max(work, axis=0, keepdims=True)
        idx = jnp.min(jnp.where(work == cm, rows_f, 1e9), axis=0, keepdims=True)
        work = jnp.where(rows_f == idx, REMOVED, work)
    return jnp.where(work == REMOVED, 1.0, 0.0)


def _nsa_kernel(qt_ref, gt_ref, kc_ref, vct_ref, ksel_ref, vselt_ref, kwin_ref, vwint_ref, out_ref,
                qt_s, pg_s, bias_s, acc_s, accw_s, ocmp_s):
    qi = pl.program_id(1)
    q0 = qi * Q_TILE
    G = NSA_KV_GROUPS
    HPL = LANES // Q_TILE
    PARTS = NSA_HPG // HPL
    NLG = G * PARTS
    n_cmp_rows = kc_ref.shape[0]
    n_sel = bias_s.shape[1]
    group_of = lambda lg: lg // PARTS
    rows_of = lambda g: slice(HEAD_DIM * g, HEAD_DIM * (g + 1))

    qt_s[...] = jnp.zeros_like(qt_s)
    for h in range(NSA_HEADS):
        g, hh = divmod(h, NSA_HPG)
        part, hl = divmod(hh, HPL)
        qt_s[g * PARTS + part, rows_of(g), Q_TILE * hl:Q_TILE * (hl + 1)] = qt_ref[HEAD_DIM * h:HEAD_DIM * (h + 1), :]
    qts = [qt_s[lg] for lg in range(NLG)]
    tq1 = q0 + lax.broadcasted_iota(jnp.int32, (1, Q_TILE), 1)

    def ones_under(vt):
        return jnp.concatenate([vt, jnp.ones((ONES_ROWS, vt.shape[1]), BF16)], axis=0)

    def cmp_branch(nr):
        n_idx = lax.broadcasted_iota(jnp.int32, (nr, 1), 0)
        cbias = _mask_bias(n_idx * CMP_STRIDE + (CMP_BLOCK - 1) <= tq1, HPL)
        for g in range(G):
            pg = None
            for part in range(PARTS):
                lg = g * PARTS + part
                s = jnp.concatenate([_dot(kc_ref[r:min(r + KV_SUB, nr), :], qts[lg])
                                     for r in range(0, nr, KV_SUB)], axis=0) + cbias
                p, l = _softmax_block(s)
                den = jnp.maximum(l, 1e-30)
                ocmp_s[lg] = _dot(vct_ref[rows_of(g), 0:nr], p.astype(BF16)) / den
                pn = p / den
                for hl in range(HPL):
                    ph = pn[:, hl * Q_TILE:(hl + 1) * Q_TILE]
                    pg = ph if pg is None else pg + ph
            for sl in range(Q_TILE // 128):
                slab = g * (Q_TILE // 128) + sl
                pg_s[slab, 0:8, :] = jnp.zeros((8, 128), F32)
                pg_s[slab, 8:8 + nr, :] = pg[:, 128 * sl:128 * (sl + 1)]
                if nr < n_cmp_rows:
                    pg_s[slab, 8 + nr:8 + n_cmp_rows, :] = jnp.zeros((n_cmp_rows - nr, 128), F32)

    n_visible = (q0 + Q_TILE - CMP_BLOCK) // CMP_STRIDE + 1
    sizes = list(range(128, n_cmp_rows + 1, 128))
    for i, nr in enumerate(sizes):
        lo_ok = n_visible > sizes[i - 1] if i > 0 else True
        hi_ok = n_visible <= nr if i + 1 < len(sizes) else True
        pl.when(jnp.logical_and(lo_ok, hi_ok))(functools.partial(cmp_branch, nr))

    W = WINDOW + Q_TILE
    wstart = jnp.maximum(q0 - WINDOW, 0)
    wc0 = wstart // 128
    kposw = wstart + lax.broadcasted_iota(jnp.int32, (W, 1), 0)
    wbias = _mask_bias((kposw <= tq1) & (kposw > tq1 - WINDOW), HPL)
    vtw = jnp.concatenate([vwint_ref[wc0 + j] for j in range(W // 128)], axis=1)
    accw_s[...] = jnp.zeros_like(accw_s)
    mw = [jnp.full((1, LANES), NEG_INF, F32)] * NLG
    win_steps = []
    for r0 in range(0, W, KV_SUB):
        r1 = min(r0 + KV_SUB, W)
        for lg in range(NLG):
            def score(r0=r0, r1=r1, lg=lg):
                k = kwin_ref[pl.ds(pl.multiple_of(wstart + r0, 128), r1 - r0), :]
                return _dot(k, qts[lg]) + wbias[r0:r1, :]

            def consume(s, r0=r0, r1=r1, lg=lg):
                vt = ones_under(vtw[rows_of(group_of(lg)), r0:r1])
                mw[lg] = _flash_update(s, None, r1 - r0, vt, mw[lg], accw_s.at[lg])

            win_steps.append((score, consume))
    _run_steps(win_steps, [fn() for fn, _ in win_steps[:LOOKAHEAD]], [])

    blk_i = lax.broadcasted_iota(jnp.int32, (n_sel, 1), 0)
    cur = tq1 // SEL_BLOCK
    forced = (blk_i == 0) | (blk_i == cur) | (blk_i == cur - 1)
    started = blk_i <= cur
    n_top = min(SEL_COUNT, n_sel)
    slc = []
    for g in range(G):
        slabs = []
        for sl in range(Q_TILE // 128):
            def tap(w):
                return pg_s[g * (Q_TILE // 128) + sl, pl.ds(8 + w, n_sel, stride=SEL_BLOCK // CMP_STRIDE), :]

            t = tap(-1) + 2.0 * tap(0)
            t = t + 2.0 * tap(1)
            t = t + 2.0 * tap(2)
            t = t + tap(3)
            slabs.append(t)
        t = slabs[0] if len(slabs) == 1 else jnp.concatenate(slabs, axis=1)
        slc.append(jnp.where(forced, NEG_INF, jnp.where(started, t, NEG_INF)))
    free = _topk_mask(jnp.concatenate(slc, axis=1), jnp.broadcast_to(blk_i.astype(F32), (n_sel, G * Q_TILE)),
                      n_top - N_FORCED)
    for g in range(G):
        bias = jnp.where(forced, 0.0, jnp.where(free[:, Q_TILE * g:Q_TILE * (g + 1)] > 0.5, 0.0, NEG_INF))
        bias_s[g] = jnp.concatenate([bias] * HPL, axis=1)

    bpc = KV_CHUNK // SEL_BLOCK
    bps = KV_SUB // SEL_BLOCK
    c_last = q0 // KV_CHUNK

    def chunk_steps(c, ms, causal):
        steps = []
        for sub in range(KV_CHUNK // KV_SUB):
            for lg in range(NLG):
                def score(sub=sub, lg=lg):
                    base = pl.multiple_of(c * KV_CHUNK, KV_CHUNK) + sub * KV_SUB
                    s = _dot(ksel_ref[pl.ds(pl.multiple_of(base, KV_SUB), KV_SUB), :], qts[lg])
                    return s if causal is None else s + causal[sub]

                def consume(s, sub=sub, lg=lg):
                    g = group_of(lg)
                    rows = bias_s[g, pl.ds(pl.multiple_of(c * bpc, bpc), bpc), :][bps * sub:bps * (sub + 1), :]
                    vt = vselt_ref[c, rows_of(g), KV_SUB * sub:KV_SUB * (sub + 1)]
                    ms[lg] = _flash_update(s, rows, SEL_BLOCK, ones_under(vt), ms[lg], acc_s.at[lg])

                steps.append((score, consume))
        return steps

    acc_s[...] = jnp.zeros_like(acc_s)

    def body(c, carry):
        ms = list(carry[:NLG])
        nxt = [fn for fn, _ in chunk_steps(c + 1, ms, None)[:LOOKAHEAD]]
        pending = _run_steps(chunk_steps(c, ms, None), carry[NLG:], nxt)
        return tuple(ms) + pending

    first = [fn() for fn, _ in chunk_steps(0, None, None)[:LOOKAHEAD]]
    carry = (jnp.full((1, LANES), NEG_INF, F32),) * NLG + tuple(first)
    carry = _chunk_loop(body, c_last, carry, NSA_TRIP_UNROLL)
    kpos = c_last * KV_CHUNK + lax.broadcasted_iota(jnp.int32, (KV_CHUNK, 1), 0)
    causal = _mask_bias(kpos <= tq1, HPL)
    n_sub = KV_CHUNK // KV_SUB
    needed = ((q0 % KV_CHUNK) + Q_TILE + KV_SUB - 1) // KV_SUB

    def run_diag(n_pieces):
        steps = chunk_steps(c_last, list(carry[:NLG]), [causal[KV_SUB * sub:KV_SUB * (sub + 1), :]
                                                       for sub in range(n_sub)])[:n_pieces * NLG]
        _run_steps(steps, [fn() for fn, _ in steps[:LOOKAHEAD]], [])

    for n_pieces in range(1, n_sub + 1):
        cond = needed == n_pieces if n_pieces < n_sub else needed >= n_pieces
        pl.when(cond)(functools.partial(run_diag, n_pieces))

    gates = jax.nn.sigmoid(gt_ref[0:3 * NSA_HEADS, :])
    outs = []
    for h in range(NSA_HEADS):
        g, hh = divmod(h, NSA_HPG)
        part, hl = divmod(hh, HPL)
        lg = g * PARTS + part
        cols = slice(Q_TILE * hl, Q_TILE * (hl + 1))
        o_sel = acc_s[lg, 0:HEAD_DIM, cols] / jnp.maximum(acc_s[lg, HEAD_DIM:HEAD_DIM + 1, cols], 1e-30)
        o_win = accw_s[lg, 0:HEAD_DIM, cols] / jnp.maximum(accw_s[lg, HEAD_DIM:HEAD_DIM + 1, cols], 1e-30)
        o = gates[3 * h:3 * h + 1, :] * ocmp_s[lg, :, cols]
        o = o + gates[3 * h + 1:3 * h + 2, :] * o_sel
        o = o + gates[3 * h + 2:3 * h + 3, :] * o_win
        outs.append(o)
    out_ref[...] = jnp.concatenate(outs, axis=0).T.astype(BF16)


def _nsa_call(qtn, gt, kc, vct, ksel, vselt, kwin, vwint, B, S):
    R = S // CMP_STRIDE
    n_sel = S // SEL_BLOCK
    nlg = NSA_HEADS * Q_TILE // LANES
    return pl.pallas_call(
        _nsa_kernel,
        out_shape=jax.ShapeDtypeStruct((B * S, NSA_DIM), BF16),
        grid=(B, S // Q_TILE),
        in_specs=[
            pl.BlockSpec((None, NSA_DIM, Q_TILE), lambda b, q: (b, 0, q)),
            pl.BlockSpec((None, 128, Q_TILE), lambda b, q: (b, 0, q)),
            pl.BlockSpec((None, R, 128), lambda b, q: (b, 0, 0)),
            pl.BlockSpec((None, 128, R), lambda b, q: (b, 0, 0)),
            pl.BlockSpec((S, 128), lambda b, q: (b, 0)),
            pl.BlockSpec((None, S // KV_CHUNK, 128, KV_CHUNK), lambda b, q: (b, 0, 0, 0)),
            pl.BlockSpec((S, 128), lambda b, q: (b, 0)),
            pl.BlockSpec((None, S // 128, 128, 128), lambda b, q: (b, 0, 0, 0)),
        ],
        out_specs=pl.BlockSpec((Q_TILE, NSA_DIM), lambda b, q: (b * (S // Q_TILE) + q, 0)),
        scratch_shapes=[
            pltpu.VMEM((nlg, 128, LANES), BF16),
            pltpu.VMEM((NSA_KV_GROUPS * (Q_TILE // 128), 8 + R, 128), F32),
            pltpu.VMEM((NSA_KV_GROUPS, n_sel, LANES), F32),
            pltpu.VMEM((nlg, HEAD_DIM + ONES_ROWS, LANES), F32),
            pltpu.VMEM((nlg, HEAD_DIM + ONES_ROWS, LANES), F32),
            pltpu.VMEM((nlg, HEAD_DIM, LANES), F32),
        ],
        compiler_params=pltpu.CompilerParams(
            dimension_semantics=("arbitrary", "arbitrary"), vmem_limit_bytes=VMEM_LIMIT),
        name="nsa_attn",
    )(qtn, gt, kc, vct, ksel, vselt, kwin, vwint)


def _moba_kernel(qt_ref, kmean_ref, k_ref, vt_ref, out_ref, qt_s, bias_s, acc_s):
    qi = pl.program_id(1)
    q0 = qi * MOBA_Q_TILE
    L = MOBA_Q_TILE
    H = MOBA_HEADS
    n_blk = kmean_ref.shape[0]
    n_top = max(1, min(MOBA_TOPK, n_blk - 1))

    qt_s[...] = jnp.zeros_like(qt_s)
    for h in range(H):
        rows = slice(HEAD_DIM * h, HEAD_DIM * (h + 1))
        qt_s[h, rows, :] = qt_ref[rows, :]
    qts = [qt_s[h] for h in range(H)]
    tq = q0 + lax.broadcasted_iota(jnp.int32, (1, L), 1)

    own = tq // MOBA_BLOCK
    blk_i = lax.broadcasted_iota(jnp.int32, (n_blk, 1), 0)
    blk_f = jnp.broadcast_to(blk_i.astype(F32), (n_blk, L))
    kmean = kmean_ref[...].astype(BF16)
    for h in range(H):
        sg = jnp.where(blk_i < own, _dot(kmean, qts[h]), NEG_INF)
        sel = _topk_mask(sg, blk_f, n_top)
        allow = ((sel > 0.5) & (blk_i < own)) | (blk_i == own)
        bias = jnp.where(allow, 0.0, NEG_INF)
        for j in range(n_blk):
            bias_s[h, j] = jnp.broadcast_to(bias[j:j + 1, :], (8, L))

    bpc = KV_CHUNK // MOBA_BLOCK

    steps = [(j, h) for j in range(bpc) for h in range(H)]

    ones = jnp.ones((ONES_ROWS, MOBA_BLOCK), BF16)

    def scores(c, j, h, diagonal):
        base = pl.multiple_of(c * KV_CHUNK, KV_CHUNK) + j * MOBA_BLOCK
        k = k_ref[pl.ds(pl.multiple_of(base, MOBA_BLOCK), MOBA_BLOCK), :]
        s = _dot(k, qts[h])
        if diagonal:
            kpos = base + lax.broadcasted_iota(jnp.int32, (MOBA_BLOCK, 1), 0)
            s = jnp.where(kpos <= tq, s, NEG_INF)
        return s

    def chunk_steps(c, ms, pending, diagonal):
        ms, pending = list(ms), list(pending)
        for t, (j, h) in enumerate(steps):
            s = pending.pop(0)
            u = t + LOOKAHEAD
            if u < len(steps):
                pending.append(scores(c, *steps[u], diagonal))
            elif not diagonal:
                pending.append(scores(c + 1, *steps[u - len(steps)], False))
            vt = vt_ref[c, HEAD_DIM * h:HEAD_DIM * (h + 1), MOBA_BLOCK * j:MOBA_BLOCK * (j + 1)]
            ms[h] = _flash_update(s, bias_s[h, c * bpc + j], MOBA_BLOCK, jnp.concatenate([vt, ones], axis=0),
                                  ms[h], acc_s.at[h])
        return tuple(ms), tuple(pending)

    acc_s[...] = jnp.zeros_like(acc_s)
    c_last = q0 // KV_CHUNK
    init = (jnp.full((1, L), NEG_INF, F32),) * H

    def body(c, carry):
        ms, pending = chunk_steps(c, carry[:H], carry[H:], False)
        return ms + pending

    carry = init + tuple(scores(0, *steps[i], False) for i in range(LOOKAHEAD))
    carry = _chunk_loop(body, c_last, carry, MOBA_TRIP_UNROLL)
    chunk_steps(c_last, carry[:H], [scores(c_last, *steps[i], True) for i in range(LOOKAHEAD)], True)
    outs = [acc_s[h, 0:HEAD_DIM, :] / jnp.maximum(acc_s[h, HEAD_DIM:HEAD_DIM + 1, :], 1e-30) for h in range(H)]
    out_ref[...] = jnp.concatenate(outs, axis=0).T.astype(BF16)


def _moba_call(qtm, kmean, km, vtm, B, S):
    n_blk = S // MOBA_BLOCK
    L = MOBA_Q_TILE
    return pl.pallas_call(
        _moba_kernel,
        out_shape=jax.ShapeDtypeStruct((B * S, MOBA_DIM), BF16),
        grid=(B, S // L),
        in_specs=[
            pl.BlockSpec((None, MOBA_DIM, L), lambda b, q: (b, 0, q)),
            pl.BlockSpec((None, n_blk, MOBA_DIM), lambda b, q: (b, 0, 0)),
            pl.BlockSpec((S, MOBA_DIM), lambda b, q: (b, 0)),
            pl.BlockSpec((None, S // KV_CHUNK, MOBA_DIM, KV_CHUNK), lambda b, q: (b, 0, 0, 0)),
        ],
        out_specs=pl.BlockSpec((L, MOBA_DIM), lambda b, q: (b * (S // L) + q, 0)),
        scratch_shapes=[
            pltpu.VMEM((MOBA_HEADS, MOBA_DIM, L), BF16),
            pltpu.VMEM((MOBA_HEADS, n_blk, 8, L), F32),
            pltpu.VMEM((MOBA_HEADS, HEAD_DIM + ONES_ROWS, L), F32),
        ],
        compiler_params=pltpu.CompilerParams(
            dimension_semantics=("arbitrary", "arbitrary"), vmem_limit_bytes=VMEM_LIMIT),
        name="moba_attn",
    )(qtm, kmean, km, vtm)


def _merge_kernel(u_ref, halo_ref, b_ref, c_ref, x_ref, gam_in_ref, wg_ref, bg_ref, pw_ref, ps_ref,
                  wa_ref, wb_ref, wc_ref, wo_ref, gam_ref, x1_ref, h2_ref, ext_s, *, tiles_per_seq):
    tm = u_ref.shape[0]
    i = pl.program_id(0)
    first = (i % tiles_per_seq) == 0
    u = u_ref[...]
    ext_s[0:POOL_MAXW, :] = jnp.where(first, 0.0, halo_ref[...])
    ext_s[POOL_MAXW:POOL_MAXW + tm, :] = u

    def tail_sum(col, k0, k1):
        tot = None
        for k in range(k0, k1):
            v = ext_s[pl.ds(POOL_MAXW - k, tm), 128 * col:128 * col + 128]
            tot = v if tot is None else tot + v
        return tot

    t_glob = (i % tiles_per_seq) * tm + lax.broadcasted_iota(jnp.int32, (tm, 1), 0)
    lane = lax.broadcasted_iota(jnp.int32, (1, 128), 1)
    low = lane < POOL_GROUP_DIM
    pooled = []
    for col in range(2):
        wa_, wb_ = POOL_WINDOWS[2 * col], POOL_WINDOWS[2 * col + 1]
        sa = tail_sum(col, 0, wa_)
        sb = sa + tail_sum(col, wa_, wb_)
        ca = jnp.minimum(t_glob + 1, wa_).astype(F32)
        cb = jnp.minimum(t_glob + 1, wb_).astype(F32)
        pooled.append(jnp.where(low, sa, sb) / jnp.where(low, ca, cb))
    d = jnp.concatenate(pooled, axis=1) - u
    a = _dot(d.astype(BF16), pw_ref[...]) * ps_ref[...]

    av = _dot(a.astype(BF16), wa_ref[...])
    bv = _dot(b_ref[...], wb_ref[...])
    cv = _dot(c_ref[...], wc_ref[...])
    x = x_ref[...]
    h = _rms(x, gam_in_ref[...]).astype(BF16)

    def gate(j):
        cols = slice(D_MODEL * j, D_MODEL * (j + 1))
        return jax.nn.sigmoid(_dot(h, wg_ref[:, cols]) + bg_ref[:, cols])

    merged = gate(0) * av
    merged = merged + gate(1) * bv
    merged = merged + gate(2) * cv
    x1 = x + _dot(merged.astype(BF16), wo_ref[...])
    x1_ref[...] = x1
    h2_ref[...] = _rms(x1, gam_ref[...]).astype(BF16)


def _merge_call(upool, bn, cm, x2, gamma_in, wg, bg, pw, ps, wa, wb, wc, wo, gamma, B, S):
    N = B * S
    tm = ROW_TILE
    nt = S // tm
    hb = tm // POOL_MAXW
    row = lambda w_: pl.BlockSpec((tm, w_), lambda i: (i, 0))
    const = lambda shape: pl.BlockSpec(shape, lambda i: (0,) * len(shape))
    return pl.pallas_call(
        functools.partial(_merge_kernel, tiles_per_seq=nt),
        out_shape=(jax.ShapeDtypeStruct((N, D_MODEL), F32), jax.ShapeDtypeStruct((N, D_MODEL), BF16)),
        grid=(N // tm,),
        in_specs=[
            row(POOL_DIM),
            pl.BlockSpec((POOL_MAXW, POOL_DIM), lambda i: (jnp.maximum(i * hb - 1, 0), 0)),
            row(NSA_DIM),
            row(MOBA_DIM),
            row(D_MODEL),
            const((1, D_MODEL)),
            const((D_MODEL, 3 * D_MODEL)),
            const((1, 3 * D_MODEL)),
            const((POOL_DIM, POOL_DIM)),
            const((1, POOL_DIM)),
            const((POOL_DIM, D_MODEL)),
            const((NSA_DIM, D_MODEL)),
            const((MOBA_DIM, D_MODEL)),
            const((D_MODEL, D_MODEL)),
            const((1, D_MODEL)),
        ],
        out_specs=(row(D_MODEL), row(D_MODEL)),
        scratch_shapes=[pltpu.VMEM((POOL_MAXW + tm, POOL_DIM), F32)],
        compiler_params=pltpu.CompilerParams(
            dimension_semantics=("arbitrary",), vmem_limit_bytes=VMEM_LIMIT),
        name="merge",
    )(upool, upool, bn, cm, x2, gamma_in, wg, bg, pw, ps, wa, wb, wc, wo, gamma)


def _ffn_kernel(h_ref, x_ref, wg_ref, wu_ref, wd_ref, gam_ref, out_ref, acc_s, *, final_norm):
    f = pl.program_id(1)

    @pl.when(f == 0)
    def _():
        acc_s[...] = jnp.zeros_like(acc_s)

    h = h_ref[...]
    g = _dot(h, wg_ref[...])
    u = _dot(h, wu_ref[...])
    act = (g * jax.nn.sigmoid(g)) * u
    acc_s[...] += _dot(act.astype(BF16), wd_ref[...])

    @pl.when(f == pl.num_programs(1) - 1)
    def _():
        y = x_ref[...] + acc_s[...]
        if final_norm:
            y = _rms(y, gam_ref[...])
        out_ref[...] = y


def _ffn_call(h2, x1, wg, wu, wd, gamma, final_norm):
    N = x1.shape[0]
    F = wg.shape[1]
    tm, tf = FFN_ROW_TILE, FFN_COL_TILE
    return pl.pallas_call(
        functools.partial(_ffn_kernel, final_norm=final_norm),
        out_shape=jax.ShapeDtypeStruct((N, D_MODEL), F32),
        grid=(N // tm, F // tf),
        in_specs=[
            pl.BlockSpec((tm, D_MODEL), lambda i, f: (i, 0)),
            pl.BlockSpec((tm, D_MODEL), lambda i, f: (i, 0)),
            pl.BlockSpec((D_MODEL, tf), lambda i, f: (0, f)),
            pl.BlockSpec((D_MODEL, tf), lambda i, f: (0, f)),
            pl.BlockSpec((tf, D_MODEL), lambda i, f: (f, 0)),
            pl.BlockSpec((1, D_MODEL), lambda i, f: (0, 0)),
        ],
        out_specs=pl.BlockSpec((tm, D_MODEL), lambda i, f: (i, 0)),
        scratch_shapes=[pltpu.VMEM((tm, D_MODEL), F32)],
        compiler_params=pltpu.CompilerParams(
            dimension_semantics=("arbitrary", "arbitrary"), vmem_limit_bytes=VMEM_LIMIT),
        name="ffn",
    )(h2, x1, wg, wu, wd, gamma)


def _reorder_in_proj(w, b):
    o_qn = POOL_DIM
    o_kv = o_qn + NSA_DIM
    o_gn = o_kv + 6 * 128
    o_mo = o_gn + 3 * NSA_HEADS
    o_gb = o_mo + 3 * MOBA_DIM
    pad = 128 - 3 * NSA_HEADS

    def pick(a):
        parts = [a[..., 0:o_qn], a[..., o_kv:o_kv + 256], a[..., o_qn:o_kv], a[..., o_kv + 256:o_gn],
                 a[..., o_gn:o_mo], jnp.zeros(a.shape[:-1] + (pad,), a.dtype), a[..., o_mo:o_gb]]
        return jnp.concatenate(parts, axis=-1)

    return pick(w), pick(b), w[..., o_gb:], b[..., o_gb:]


def _rope_tables(S):
    pos = jnp.arange(S, dtype=F32)
    inv_freq = ROPE_THETA ** (-jnp.arange(0, HEAD_DIM, 2, dtype=F32) / HEAD_DIM)
    ang = pos[:, None] * inv_freq[None, :]
    cos, sin = jnp.cos(ang), jnp.sin(ang)
    cos_t = jnp.tile(cos, (1, 4))
    sin_t = jnp.tile(jnp.concatenate([-sin, sin], axis=1), (1, 2))
    return cos_t, sin_t


def kernel(x, attn_norm, w_in, b_in, pool_w, pool_scale, cmp_pos, cmp_w1, cmp_b1, cmp_w2, cmp_b2,
           w_br_pool, w_br_nsa, w_br_moba, w_out, ffn_norm, w_gate, w_up, w_down, final_norm):
    B, S, D = x.shape
    depth = w_in.shape[0]
    assert D == D_MODEL and S % KV_CHUNK == 0 and S % ROW_TILE == 0 and S >= WINDOW + Q_TILE
    assert (B * S) % FFN_ROW_TILE == 0 and w_gate.shape[2] % FFN_COL_TILE == 0
    assert S % (128 * CMP_STRIDE) == 0 and S % MOBA_Q_TILE == 0 and S // SEL_BLOCK >= SEL_COUNT
    N = B * S
    R = S // CMP_STRIDE
    cos_t, sin_t = _rope_tables(S)
    x2 = x.reshape(N, D)
    for l in range(depth):
        w_all, b_all, w_gbr, b_gbr = _reorder_in_proj(w_in[l].astype(BF16), b_in[l])
        (upool, cmpk, cmpv, qtn, ksel, vselt, kwin, vwint, gt, qtm, km, vtm, kmean) = _proj_call(
            x2, attn_norm[l][None, :], w_all, b_all[None, :], cos_t, sin_t, B, S)

        kc, vct = _compress_call(
            cmpk, cmpv, cmp_pos[l], cmp_w1[l].astype(BF16),
            cmp_b1[l][:, None, :], cmp_w2[l].astype(BF16), cmp_b2[l][:, None, :], B, R)

        bn = _nsa_call(qtn, gt, kc, vct, ksel, vselt, kwin, vwint, B, S)
        cm = _moba_call(qtm, kmean.reshape(B, S // MOBA_BLOCK, MOBA_DIM), km, vtm, B, S)

        pw_bd = jax.scipy.linalg.block_diag(*[pool_w[l, g] for g in range(len(POOL_WINDOWS))])
        x1, h2 = _merge_call(
            upool, bn, cm, x2, attn_norm[l][None, :], w_gbr, b_gbr[None, :], pw_bd.astype(BF16), pool_scale[l][None, :],
            w_br_pool[l].astype(BF16), w_br_nsa[l].astype(BF16), w_br_moba[l].astype(BF16),
            w_out[l].astype(BF16), ffn_norm[l][None, :], B, S)

        x2 = _ffn_call(h2, x1, w_gate[l].astype(BF16), w_up[l].astype(BF16), w_down[l].astype(BF16),
                       final_norm[None, :], final_norm=(l == depth - 1))
    return x2.reshape(B, S, D)
```
